```python
import math
import jax
import jax.numpy as jnp
from jax import lax
import numpy as np

D_MODEL = 4096
BATCH = 1
SEQ = 8192
DEPTH = 1
DEC_BATCH = 32
DEC_SEQ = 8
PAST_LEN = 8192
PAGE_SIZE = 128

DK = 64
DV = 2 * DK
H_A = (D_MODEL // 2) // DV
ATT_Q_W = H_A * 2 * DK
ATT_K_W = H_A * 2 * DK
ATT_V_W = H_A * DV
ATT_W = ATT_Q_W + ATT_K_W + ATT_V_W
Q_BLOCK = 128
ATTN_SCALE = DK ** -0.5
SUBLN_EPS = 1e-5
NEG_INF = -1e30
HS_R = 64
RWKV_W = D_MODEL // 2
H_R = RWKV_W // HS_R
W_LORA = 96
A_LORA = 96
G_LORA = 256
RW_IN = 3 * RWKV_W + W_LORA + A_LORA + G_LORA
IN_W = ATT_W + RW_IN
GN_EPS = 64e-5
D_FF = ((8 * D_MODEL // 3 + 255) // 256) * 256
ALPHA = (2 * DEPTH) ** 0.25
BETA = (8 * DEPTH) ** -0.25
LN_EPS = 1e-5

kernel_name = "diffattn_rwkv7_gated_hybrid_step"


def split_cols(u, widths):
    offsets = [int(o) for o in np.cumsum(widths)[:-1]]
    return jnp.split(u, offsets, axis=-1)


def layer_norm(x, g, b):
    xf = x.astype(jnp.float32)
    mu = jnp.mean(xf, axis=-1, keepdims=True)
    var = jnp.mean(jnp.square(xf - mu), axis=-1, keepdims=True)
    return ((xf - mu) * lax.rsqrt(var + LN_EPS)).astype(x.dtype) * g + b


def rms_norm(x, g, eps):
    xf = x.astype(jnp.float32)
    ms = jnp.mean(jnp.square(xf), axis=-1, keepdims=True)
    return (xf * lax.rsqrt(ms + eps)).astype(x.dtype) * g


def diff_lambda(lq1, lk1, lq2, lk2, lam_init):
    f = jnp.float32
    return (jnp.exp(jnp.sum(lq1.astype(f) * lk1.astype(f)))
            - jnp.exp(jnp.sum(lq2.astype(f) * lk2.astype(f))) + lam_init)


def diff_attend(q, k, v, mask, lam):
    B, Tq = q.shape[:2]
    Tk = k.shape[1]
    qc = q.reshape(B, Tq, H_A, 2, DK)
    kc = k.reshape(B, Tk, H_A, 2, DK)
    s = jnp.einsum('bqhcd,bkhcd->bhcqk', qc, kc, preferred_element_type=jnp.float32) * ATTN_SCALE
    s = jnp.where(mask, s, NEG_INF)
    p = jax.nn.softmax(s, axis=-1)
    att = p[:, :, 0] - lam * p[:, :, 1]
    o = jnp.einsum('bhqk,bkhd->bqhd', att.astype(v.dtype), v, preferred_element_type=jnp.float32)
    return o.astype(v.dtype)


def diff_attn_prompt(q, k, v, lam):
    B, S = q.shape[:2]
    nb = S // Q_BLOCK
    qb = q.reshape(B, nb, Q_BLOCK, H_A, 2 * DK).swapaxes(0, 1)
    kpos = jnp.arange(S)

    def one_block(args):
        i, q_i = args
        qpos = i * Q_BLOCK + jnp.arange(Q_BLOCK)
        mask = kpos[None, :] <= qpos[:, None]
        return diff_attend(q_i, k, v, mask, lam)

    o = lax.map(one_block, (jnp.arange(nb), qb))
    return o.swapaxes(0, 1).reshape(B, S, H_A, DV)


def diff_attn_sample(q, k, v, cache_k, cache_v, layer, page_table, lam):
    Tn = q.shape[1]
    past = page_table.shape[1] * PAGE_SIZE
    kpos = jnp.arange(past + Tn)
    qpos = past + jnp.arange(Tn)
    mask = kpos[None, :] <= qpos[:, None]

    def one_seq(args):
        q_b, k_b, v_b, pt_b = args
        k_all = jnp.concatenate([cache_k[layer, pt_b].reshape(past, H_A, 2 * DK).astype(k_b.dtype), k_b], axis=0)
        v_all = jnp.concatenate([cache_v[layer, pt_b].reshape(past, H_A, DV).astype(v_b.dtype), v_b], axis=0)
        return diff_attend(q_b[None], k_all[None], v_all[None], mask, lam)[0]

    return lax.map(one_seq, (q, k, v, page_table))


def wkv7_scan(s0, r, decay, k, v, kk, a):
    def step(s, inp):
        r_t, w_t, k_t, v_t, kk_t, a_t = inp
        sa = jnp.einsum('bhvk,bhk->bhv', s, -kk_t)
        s = (s * w_t[:, :, None, :] + sa[..., None] * (kk_t * a_t)[:, :, None, :]
             + v_t[..., None] * k_t[:, :, None, :])
        y = jnp.einsum('bhvk,bhk->bhv', s, r_t)
        return s, y

    xs = tuple(jnp.moveaxis(t, 1, 0) for t in (r, decay, k, v, kk, a))
    s_final, ys = lax.scan(step, s0, xs)
    return jnp.moveaxis(ys, 0, 1), s_final


def rwkv7_time_mix(u_rw, prev_row, s0, mu, w0, w2, a0, a2, g2, k_k, k_a, r_k, lnx_w, lnx_b):
    B, T, _ = u_rw.shape
    f32 = jnp.float32
    shifted = jnp.concatenate([prev_row[:, None, :].astype(u_rw.dtype), u_rw[:, :-1]], axis=1)
    xm = u_rw + (shifted - u_rw) * mu
    r, w_l, k, v, a_l, g_l = split_cols(xm, (RWKV_W, W_LORA, RWKV_W, RWKV_W, A_LORA, G_LORA))
    w = -jax.nn.softplus(-(w0 + jnp.tanh(w_l) @ w2)) - 0.5
    decay = jnp.exp(-jnp.exp(w.astype(f32)))
    a = jax.nn.sigmoid(a0 + a_l @ a2)
    g = jax.nn.sigmoid(g_l) @ g2

    def heads(t):
        return t.reshape(B, T, H_R, HS_R).astype(f32)

    kk = heads(k * k_k)
    kk = kk / jnp.maximum(jnp.sqrt(jnp.sum(kk * kk, axis=-1, keepdims=True)), 1e-12)
    k = k * (1.0 + (a - 1.0) * k_a)
    r_h, k_h, v_h, a_h, w_h = heads(r), heads(k), heads(v), heads(a), heads(decay)
    y, s_final = wkv7_scan(s0.astype(f32), r_h, w_h, k_h, v_h, kk, a_h)
    mean = jnp.mean(y, axis=-1, keepdims=True)
    var = jnp.mean(jnp.square(y - mean), axis=-1, keepdims=True)
    y = ((y - mean) * lax.rsqrt(var + GN_EPS)).reshape(B, T, RWKV_W) * lnx_w + lnx_b
    bonus = jnp.sum(r_h * k_h * r_k, axis=-1, keepdims=True) * v_h
    out = (y + bonus.reshape(B, T, RWKV_W)).astype(g.dtype) * g
    return out, s_final.astype(s0.dtype), u_rw[:, -1]


def merge_and_ffn(x, o_att, o_rw, lam_init, subln_w, w_pa, w_pb, w_gate, b_gate, w_o,
                  ln1_g, ln1_b, w_ffn_gate, w_ffn_up, w_ffn_down, ln2_g, ln2_b):
    B, T, _ = x.shape
    o_att = rms_norm(o_att, subln_w, SUBLN_EPS) * (1.0 - lam_init)
    branch_a = o_att.reshape(B, T, ATT_V_W) @ w_pa
    branch_b = o_rw @ w_pb
    gate_a, gate_b = jnp.split(jax.nn.sigmoid(x @ w_gate + b_gate), 2, axis=-1)
    mix = (gate_a * branch_a + gate_b * branch_b) @ w_o
    x1 = layer_norm(ALPHA * x + mix, ln1_g, ln1_b)
    h = (jax.nn.silu(x1 @ w_ffn_gate) * (x1 @ w_ffn_up)) @ w_ffn_down
    return layer_norm(ALPHA * x1 + h, ln2_g, ln2_b)


def setup_inputs(seed: int = 0) -> dict:
    key = jax.random.key(seed)
    ks = iter(jax.random.split(key, 48))
    f32 = jnp.float32

    def nrm(shape, scale):
        return jax.random.normal(next(ks), shape, f32) * scale

    def unif(shape, lo, hi):
        return jax.random.uniform(next(ks), shape, f32, lo, hi)

    n_pages = PAST_LEN // PAGE_SIZE
    n_used = DEC_BATCH * n_pages
    n_pool = n_used + (n_used + 3) // 4
    perm = jax.random.permutation(next(ks), n_pool)
    page_table = perm[:n_used].reshape(DEC_BATCH, n_pages).astype(jnp.int32)
    L = DEPTH
    return {
        "x_prompt": nrm((BATCH, SEQ, D_MODEL), 1.0),
        "x_sample": nrm((DEC_BATCH, DEC_SEQ, D_MODEL), 1.0),
        "cache_k": nrm((L, n_pool, PAGE_SIZE, H_A, 2 * DK), 1.0),
        "cache_v": nrm((L, n_pool, PAGE_SIZE, H_A, DV), 1.0),
        "state_wkv": nrm((L, DEC_BATCH, H_R, HS_R, HS_R), 0.5),
        "state_shift": nrm((L, DEC_BATCH, RW_IN), 1.0),
        "page_table": page_table,
        "w_in": nrm((L, D_MODEL, IN_W), D_MODEL ** -0.5),
        "mu_shift": unif((L, RW_IN), 0.0, 1.0),
        "w_decay0": unif((L, RWKV_W), -6.0, -1.0),
        "w_decay2": nrm((L, W_LORA, RWKV_W), 0.5 * W_LORA ** -0.5),
        "a0": nrm((L, RWKV_W), 0.1),
        "a2": nrm((L, A_LORA, RWKV_W), 0.5 * A_LORA ** -0.5),
        "g2": nrm((L, G_LORA, RWKV_W), G_LORA ** -0.5),
        "k_k": 0.85 + nrm((L, RWKV_W), 0.02),
        "k_a": 1.0 + nrm((L, RWKV_W), 0.02),
        "r_k": nrm((L, H_R, HS_R), 0.1),
        "lnx_w": 1.0 + nrm((L, RWKV_W), 0.02),
        "lnx_b": nrm((L, RWKV_W), 0.02),
        "lambda_q1": nrm((L, DK), 0.1),
        "lambda_k1": nrm((L, DK), 0.1),
        "lambda_q2": nrm((L, DK), 0.1),
        "lambda_k2": nrm((L, DK), 0.1),
        "subln_w": 1.0 + nrm((L, DV), 0.02),
        "w_pa": nrm((L, ATT_V_W, D_MODEL), ATT_V_W ** -0.5),
        "w_pb": nrm((L, RWKV_W, D_MODEL), RWKV_W ** -0.5),
        "w_gate": nrm((L, D_MODEL, 2 * D_MODEL), D_MODEL ** -0.5),
        "b_gate": nrm((L, 2 * D_MODEL), 0.02),
        "w_o": nrm((L, D_MODEL, D_MODEL), BETA * D_MODEL ** -0.5),
        "ln1_g": 1.0 + nrm((L, D_MODEL), 0.02),
        "ln1_b": nrm((L, D_MODEL), 0.02),
        "w_ffn_gate": nrm((L, D_MODEL, D_FF), D_MODEL ** -0.5),
        "w_ffn_up": nrm((L, D_MODEL, D_FF), D_MODEL ** -0.5),
        "w_ffn_down": nrm((L, D_FF, D_MODEL), BETA * D_FF ** -0.5),
        "ln2_g": 1.0 + nrm((L, D_MODEL), 0.02),
        "ln2_b": nrm((L, D_MODEL), 0.02),
    }


def reference(x_prompt, x_sample, cache_k, cache_v, state_wkv, state_shift, page_table,
              w_in, mu_shift, w_decay0, w_decay2, a0, a2, g2, k_k, k_a, r_k, lnx_w, lnx_b,
              lambda_q1, lambda_k1, lambda_q2, lambda_k2, subln_w,
              w_pa, w_pb, w_gate, b_gate, w_o, ln1_g, ln1_b,
              w_ffn_gate, w_ffn_up, w_ffn_down, ln2_g, ln2_b):
    xp, xs = x_prompt, x_sample
    kp_l, vp_l, wp_l, sp_l = [], [], [], []
    ks_l, vs_l, ws_l, ss_l = [], [], [], []
    for l in range(DEPTH):
        lam_init = 0.8 - 0.6 * math.exp(-0.3 * l)
        lam = diff_lambda(lambda_q1[l], lambda_k1[l], lambda_q2[l], lambda_k2[l], lam_init)

        def mixers_in(x, wkv0, shift0):
            B, T, _ = x.shape
            u = x @ w_in[l]
            q, k, v, u_rw = split_cols(u, (ATT_Q_W, ATT_K_W, ATT_V_W, RW_IN))
            q = q.reshape(B, T, H_A, 2 * DK)
            k = k.reshape(B, T, H_A, 2 * DK)
            v = v.reshape(B, T, H_A, DV)
            o_rw, wkv, shift = rwkv7_time_mix(u_rw, shift0, wkv0, mu_shift[l], w_decay0[l], w_decay2[l],
                                              a0[l], a2[l], g2[l], k_k[l], k_a[l], r_k[l], lnx_w[l], lnx_b[l])
            return q, k, v, o_rw, wkv, shift

        def finish(x, o_att, o_rw):
            return merge_and_ffn(x, o_att, o_rw, lam_init, subln_w[l], w_pa[l], w_pb[l], w_gate[l], b_gate[l],
                                 w_o[l], ln1_g[l], ln1_b[l], w_ffn_gate[l], w_ffn_up[l], w_ffn_down[l],
                                 ln2_g[l], ln2_b[l])

        Bp = xp.shape[0]
        q, k, v, o_rw, wkv, shift = mixers_in(xp, jnp.zeros((Bp, H_R, HS_R, HS_R), jnp.float32),
                                              jnp.zeros((Bp, RW_IN), xp.dtype))
        o_att = diff_attn_prompt(q, k, v, lam)
        xp = finish(xp, o_att, o_rw)
        kp_l.append(k)
        vp_l.append(v)
        wp_l.append(wkv)
        sp_l.append(shift)

        q, k, v, o_rw, wkv, shift = mixers_in(xs, state_wkv[l], state_shift[l])
        o_att = diff_attn_sample(q, k, v, cache_k, cache_v, l, page_table, lam)
        xs = finish(xs, o_att, o_rw)
        ks_l.append(k)
        vs_l.append(v)
        ws_l.append(wkv)
        ss_l.append(shift)

    new_k_prompt = jnp.stack(kp_l)
    new_v_prompt = jnp.stack(vp_l)
    new_wkv_prompt = jnp.stack(wp_l)
    new_shift_prompt = jnp.stack(sp_l)
    new_k_sample = jnp.stack(ks_l)
    new_v_sample = jnp.stack(vs_l)
    new_wkv_sample = jnp.stack(ws_l)
    new_shift_sample = jnp.stack(ss_l)
    return (xp, xs, new_k_prompt, new_v_prompt, new_wkv_prompt, new_shift_prompt,
            new_k_sample, new_v_sample, new_wkv_sample, new_shift_sample)
```

```python
import functools
import math

import jax
import jax.numpy as jnp
from jax import lax
from jax.experimental import pallas as pl
from jax.experimental.pallas import tpu as pltpu

F32 = jnp.float32
BF16 = jnp.bfloat16

LANES = 128
VMEM_LIMIT = 56 * 1024 * 1024

DK = 64
DV = 128
HS_R = 64
W_LORA = 96
A_LORA = 96
G_LORA = 256
LORA_PAD = 128
ATTN_SCALE = DK ** -0.5
SUBLN_EPS = 1e-5
NEG_INF = -1e30
GN_EPS = 64e-5
LN_EPS = 1e-5
PAGE = 128
WKV_CHUNK = 64


def _cparams(sem):
    return pltpu.CompilerParams(dimension_semantics=sem, vmem_limit_bytes=VMEM_LIMIT)


def _dot(a, b):
    return jnp.dot(a, b, preferred_element_type=F32)


def _dot_nt(a, b):
    return lax.dot_general(a, b, (((1,), (1,)), ((), ())), preferred_element_type=F32)


def _dot_tn(a, b):
    return lax.dot_general(a, b, (((0,), (0,)), ((), ())), preferred_element_type=F32)


def _sigmoid(x):
    return 1.0 / (1.0 + jnp.exp(-x))


def _mm_body(x_ref, w_ref, o_ref):
    o_ref[...] = _dot(x_ref[...], w_ref[...]).astype(o_ref.dtype)


def _matmul(x, w, *, tm, tn, out_dtype, name):
    m, k = x.shape
    n = w.shape[1]
    return pl.pallas_call(
        _mm_body,
        grid=(m // tm, n // tn),
        in_specs=[pl.BlockSpec((tm, k), lambda i, j: (i, 0)),
                  pl.BlockSpec((k, tn), lambda i, j: (0, j))],
        out_specs=pl.BlockSpec((tm, tn), lambda i, j: (i, j)),
        out_shape=jax.ShapeDtypeStruct((m, n), out_dtype),
        compiler_params=_cparams(("parallel", "parallel")),
        name=name,
    )(x, w)


def _mm_resid_body(x_ref, w_ref, r_ref, o_ref, *, alpha):
    o_ref[...] = alpha * r_ref[...] + _dot(x_ref[...], w_ref[...])


def _matmul_resid(x, w, resid, *, alpha, tm, tn, name):
    m, k = x.shape
    n = w.shape[1]
    return pl.pallas_call(
        functools.partial(_mm_resid_body, alpha=alpha),
        grid=(m // tm, n // tn),
        in_specs=[pl.BlockSpec((tm, k), lambda i, j: (i, 0)),
                  pl.BlockSpec((k, tn), lambda i, j: (0, j)),
                  pl.BlockSpec((tm, tn), lambda i, j: (i, j))],
        out_specs=pl.BlockSpec((tm, tn), lambda i, j: (i, j)),
        out_shape=jax.ShapeDtypeStruct((m, n), F32),
        compiler_params=_cparams(("parallel", "parallel")),
        name=name,
    )(x, w, resid)


def _gated_merge_body(x_ref, oa_ref, ob_ref, wga_ref, wgb_ref, ba_ref, bb_ref, wpa_ref, wpb_ref, o_ref):
    x = x_ref[...]
    ga = _sigmoid(_dot(x, wga_ref[...]) + ba_ref[...])
    gb = _sigmoid(_dot(x, wgb_ref[...]) + bb_ref[...])
    o = ga * _dot(oa_ref[...], wpa_ref[...]) + gb * _dot(ob_ref[...], wpb_ref[...])
    o_ref[...] = o.astype(o_ref.dtype)


def _gated_merge(x, oa, ob, wga, wgb, ba, bb, wpa, wpb, *, tm, tn):
    m, d = x.shape
    ka = oa.shape[1]
    kb = ob.shape[1]
    n = wga.shape[1]
    row = lambda w: pl.BlockSpec((tm, w), lambda i, j: (i, 0))
    col = lambda k: pl.BlockSpec((k, tn), lambda i, j: (0, j))
    return pl.pallas_call(
        _gated_merge_body,
        grid=(m // tm, n // tn),
        in_specs=[row(d), row(ka), row(kb), col(d), col(d), col(1), col(1), col(ka), col(kb)],
        out_specs=pl.BlockSpec((tm, tn), lambda i, j: (i, j)),
        out_shape=jax.ShapeDtypeStruct((m, n), BF16),
        compiler_params=_cparams(("parallel", "parallel")),
        name="gated_merge",
    )(x, oa, ob, wga, wgb, ba, bb, wpa, wpb)


def _swiglu_body(x_ref, wg_ref, wu_ref, o_ref):
    x = x_ref[...]
    g = _dot(x, wg_ref[...])
    u = _dot(x, wu_ref[...])
    o_ref[...] = (g * _sigmoid(g) * u).astype(o_ref.dtype)


def _swiglu(x, wg, wu, *, tm, tn):
    m, k = x.shape
    n = wg.shape[1]
    return pl.pallas_call(
        _swiglu_body,
        grid=(m // tm, n // tn),
        in_specs=[pl.BlockSpec((tm, k), lambda i, j: (i, 0)),
                  pl.BlockSpec((k, tn), lambda i, j: (0, j)),
                  pl.BlockSpec((k, tn), lambda i, j: (0, j))],
        out_specs=pl.BlockSpec((tm, tn), lambda i, j: (i, j)),
        out_shape=jax.ShapeDtypeStruct((m, n), BF16),
        compiler_params=_cparams(("parallel", "parallel")),
        name="swiglu",
    )(x, wg, wu)


def _layer_norm_body(x_ref, g_ref, b_ref, o_ref, ob_ref):
    x = x_ref[...]
    mu = jnp.mean(x, axis=-1, keepdims=True)
    d = x - mu
    var = jnp.mean(d * d, axis=-1, keepdims=True)
    y = d * lax.rsqrt(var + LN_EPS) * g_ref[...] + b_ref[...]
    o_ref[...] = y
    ob_ref[...] = y.astype(BF16)


def _layer_norm(x, g, b, *, tm):
    m, d = x.shape
    return pl.pallas_call(
        _layer_norm_body,
        grid=(m // tm,),
        in_specs=[pl.BlockSpec((tm, d), lambda i: (i, 0)),
                  pl.BlockSpec((1, d), lambda i: (0, 0)),
                  pl.BlockSpec((1, d), lambda i: (0, 0))],
        out_specs=[pl.BlockSpec((tm, d), lambda i: (i, 0)),
                   pl.BlockSpec((tm, d), lambda i: (i, 0))],
        out_shape=[jax.ShapeDtypeStruct((m, d), F32), jax.ShapeDtypeStruct((m, d), BF16)],
        compiler_params=_cparams(("parallel",)),
        name="layer_norm",
    )(x, g, b)


def _rwkv_prep_body(ur_ref, uk_ref, uv_ref, ul_ref, prev_ref, mu_ref, w0_ref, w2_ref, a0_ref, a2_ref,
                    g2_ref, kk_ref, ka_ref,
                    r_out, lw_out, k_out, v_out, kk_out, a_out, g_out, *, tm, width):
    first = lax.broadcasted_iota(jnp.int32, (tm, 1), 0) == 0

    def mixed(u_ref, lo, hi):
        u = u_ref[...]
        prev = jnp.broadcast_to(prev_ref[:, lo:hi], (tm, hi - lo))
        shifted = jnp.where(first, prev, pltpu.roll(u, shift=1, axis=0))
        return u + (shifted - u) * mu_ref[:, lo:hi]

    w = width
    r = mixed(ur_ref, 0, w)
    k = mixed(uk_ref, w, 2 * w)
    v = mixed(uv_ref, 2 * w, 3 * w)
    lora = mixed(ul_ref, 3 * w, 3 * w + 2 * LORA_PAD + G_LORA)
    w_l = lora[:, 0:LORA_PAD]
    a_l = lora[:, LORA_PAD:2 * LORA_PAD]
    g_l = lora[:, 2 * LORA_PAD:]

    z = w0_ref[...] + _dot(jnp.tanh(w_l).astype(BF16), w2_ref[...])
    nz = -z
    softplus = jnp.maximum(nz, 0.0) + jnp.log(1.0 + jnp.exp(-jnp.abs(nz)))
    wdec = -softplus - 0.5
    a = _sigmoid(a0_ref[...] + _dot(a_l.astype(BF16), a2_ref[...]))
    g = _dot(_sigmoid(g_l).astype(BF16), g2_ref[...])

    r_out[...] = r
    lw_out[...] = -jnp.exp(wdec)
    k_out[...] = k * (1.0 + (a - 1.0) * ka_ref[...])
    v_out[...] = v
    kk_out[...] = k * kk_ref[...]
    a_out[...] = a
    g_out[...] = g


def _rwkv_prep(u, row0, rows, prev, mu, w0, w2, a0, a2, g2, k_k, k_a, *, tm, width, col0):
    nblk = rows // tm
    rb0 = row0 // tm
    cb = col0 // width
    lw = 2 * LORA_PAD + G_LORA
    lb = (col0 + 3 * width) // lw
    n_prev = prev.shape[1]
    rw_w = prev.shape[2]
    vec = lambda n: pl.BlockSpec((1, n), lambda i: (0, 0))
    out = jax.ShapeDtypeStruct((rows, width), F32)
    body = functools.partial(_rwkv_prep_body, tm=tm, width=width)
    return pl.pallas_call(
        body,
        grid=(nblk,),
        in_specs=[pl.BlockSpec((tm, width), lambda i: (rb0 + i, cb)),
                  pl.BlockSpec((tm, width), lambda i: (rb0 + i, cb + 1)),
                  pl.BlockSpec((tm, width), lambda i: (rb0 + i, cb + 2)),
                  pl.BlockSpec((tm, lw), lambda i: (rb0 + i, lb)),
                  pl.BlockSpec((None, n_prev, rw_w), lambda i: (i, 0, 0)),
                  vec(rw_w), vec(width),
                  pl.BlockSpec((LORA_PAD, width), lambda i: (0, 0)),
                  vec(width),
                  pl.BlockSpec((LORA_PAD, width), lambda i: (0, 0)),
                  pl.BlockSpec((G_LORA, width), lambda i: (0, 0)),
                  vec(width), vec(width)],
        out_specs=[pl.BlockSpec((tm, width), lambda i: (i, 0))] * 7,
        out_shape=[out] * 7,
        compiler_params=_cparams(("parallel",)),
        name="rwkv_prep",
    )(u, u, u, u, prev, mu, w0, w2, a0, a2, g2, k_k, k_a)


def _wkv_body(r_ref, lw_ref, k_ref, v_ref, kk_ref, a_ref, g_ref, s0_ref, lnw_ref, lnb_ref, rk_ref,
              o_ref, st_ref, st_scr, *, chunk):
    c = pl.program_id(2)
    C = chunk
    C2 = 2 * C

    @pl.when(c == 0)
    def _():
        st_scr[...] = s0_ref[...]

    lane = lax.broadcasted_iota(jnp.int32, (C2, LANES), 1)
    rowi = lax.broadcasted_iota(jnp.int32, (C2, LANES), 0)
    own = (lane >= HS_R) == (rowi >= C)

    def twice(x):
        return jnp.concatenate([x, x], axis=0)

    def stack(x):
        return jnp.where(own, twice(x), 0.0)

    lw = lw_ref[...]
    tr = lax.broadcasted_iota(jnp.int32, (C, C), 0)
    tc = lax.broadcasted_iota(jnp.int32, (C, C), 1)
    tri = (tr >= tc).astype(BF16)
    lw_hi = lw.astype(BF16)
    rem = lw - lw_hi.astype(F32)
    lw_mid = rem.astype(BF16)
    lw_lo = (rem - lw_mid.astype(F32)).astype(BF16)
    G = _dot(tri, lw_hi) + _dot(tri, lw_mid) + _dot(tri, lw_lo)
    GC = G[C - 1:C, :]
    e_prev = jnp.exp(G - lw)
    e_neg = jnp.exp(-G)
    e_pos = jnp.exp(G)
    e_rest = jnp.exp(GC - G)
    e_all = jnp.exp(GC)

    r = r_ref[...]
    k = k_ref[...]
    v = v_ref[...]
    a = a_ref[...]

    kk2 = stack(kk_ref[...])
    nrm = jnp.sqrt(jnp.sum(kk2 * kk2, axis=1, keepdims=True))
    kk2 = kk2 / jnp.maximum(nrm, 1e-12)

    A2 = (kk2 * twice(-e_prev)).astype(BF16)
    B2 = (kk2 * twice(a * e_neg)).astype(BF16)
    K2 = stack(k * e_neg).astype(BF16)
    R2 = stack(r * e_pos).astype(BF16)
    Bt2 = (kk2 * twice(a * e_rest)).astype(BF16)
    Kt2 = stack(k * e_rest).astype(BF16)
    V2f = stack(v)
    V2 = V2f.astype(BF16)

    mr = lax.broadcasted_iota(jnp.int32, (C2, C2), 0)
    mc = lax.broadcasted_iota(jnp.int32, (C2, C2), 1)
    strict = mr > mc
    incl = mr >= mc
    eye = (mr == mc).astype(F32)

    m_ab = jnp.where(strict, _dot_nt(A2, B2), 0.0)
    m_ak = jnp.where(strict, _dot_nt(A2, K2), 0.0).astype(BF16)
    m_rb = jnp.where(incl, _dot_nt(R2, B2), 0.0).astype(BF16)
    m_rk = jnp.where(incl, _dot_nt(R2, K2), 0.0).astype(BF16)

    t_inv = eye + m_ab
    p = m_ab
    n = 1
    while 2 * n < C:
        pb = p.astype(BF16)
        p = _dot(pb, pb)
        t_inv = _dot((eye + p).astype(BF16), t_inv.astype(BF16))
        n *= 2

    st = st_scr[...]
    stb = st.astype(BF16)
    x0 = _dot(A2, stb) + _dot(m_ak, V2)
    u2 = _dot(t_inv.astype(BF16), x0.astype(BF16))
    u2b = u2.astype(BF16)
    y2 = _dot(R2, stb) + _dot(m_rb, u2b) + _dot(m_rk, V2)

    decay_col = jnp.broadcast_to(e_all, (LANES, LANES)).T
    st_new = decay_col * st + _dot_tn(Bt2, u2b) + _dot_tn(Kt2, V2)
    st_scr[...] = st_new

    @pl.when(c == pl.num_programs(2) - 1)
    def _():
        st_ref[...] = st_new

    inv_hs = 1.0 / HS_R
    mean = jnp.sum(y2, axis=1, keepdims=True) * inv_hs
    d = jnp.where(own, y2 - mean, 0.0)
    var = jnp.sum(d * d, axis=1, keepdims=True) * inv_hs
    yn2 = d * lax.rsqrt(var + GN_EPS)
    bonus2 = jnp.sum(stack(r * k * rk_ref[...]), axis=1, keepdims=True) * V2f
    yn = yn2[:C] + yn2[C:]
    bonus = bonus2[:C] + bonus2[C:]
    o_ref[...] = ((yn * lnw_ref[...] + lnb_ref[...] + bonus) * g_ref[...]).astype(o_ref.dtype)


def _wkv(r, lw, k, v, kk, a, g, s0, lnw, lnb, rk, *, seqs, chunk):
    rows, width = r.shape
    t = rows // seqs
    nchunk = t // chunk
    npair = width // LANES
    blk = pl.BlockSpec((chunk, LANES), lambda b, p, c: (b * nchunk + c, p))
    vec = pl.BlockSpec((1, LANES), lambda b, p, c: (0, p))
    stspec = pl.BlockSpec((None, None, LANES, LANES), lambda b, p, c: (b, p, 0, 0))
    return pl.pallas_call(
        functools.partial(_wkv_body, chunk=chunk),
        grid=(seqs, npair, nchunk),
        in_specs=[blk] * 7 + [stspec, vec, vec, vec],
        out_specs=[blk, stspec],
        out_shape=[jax.ShapeDtypeStruct((rows, width), BF16),
                   jax.ShapeDtypeStruct((seqs, npair, LANES, LANES), F32)],
        scratch_shapes=[pltpu.VMEM((LANES, LANES), F32)],
        compiler_params=_cparams(("parallel", "parallel", "arbitrary")),
        name="wkv7_chunked",
    )(r, lw, k, v, kk, a, g, s0, lnw, lnb, rk)


def _two_maps(q):
    t = q.shape[0]
    lane = lax.broadcasted_iota(jnp.int32, (t, LANES), 1)
    q = q * ATTN_SCALE
    return jnp.concatenate([jnp.where(lane < DK, q, 0.0), jnp.where(lane >= DK, q, 0.0)], axis=0).astype(BF16)


def _sub_norm(acc, l, lam, subw, post_scale):
    t = acc.shape[0] // 2
    o = acc[:t] / l[:t] - lam * (acc[t:] / l[t:])
    ms = jnp.mean(o * o, axis=-1, keepdims=True)
    return o * lax.rsqrt(ms + SUBLN_EPS) * subw * post_scale


def _attn_prompt_body(lam_ref, q_ref, k_ref, v_ref, sw_ref, o_ref, q2_scr, m_scr, l_scr, acc_scr,
                      *, tq, post_scale):
    qi = pl.program_id(1)
    ki = pl.program_id(2)

    @pl.when(ki == 0)
    def _():
        q2_scr[...] = _two_maps(q_ref[...])
        m_scr[...] = jnp.full_like(m_scr, NEG_INF)
        l_scr[...] = jnp.zeros_like(l_scr)
        acc_scr[...] = jnp.zeros_like(acc_scr)

    @pl.when(ki <= qi)
    def _():
        s = _dot_nt(q2_scr[...], k_ref[...].astype(BF16))
        rowq = lax.broadcasted_iota(jnp.int32, (2 * tq, tq), 0) % tq
        col = lax.broadcasted_iota(jnp.int32, (2 * tq, tq), 1)
        s = jnp.where((ki < qi) | (col <= rowq), s, NEG_INF)
        m_old = m_scr[...]
        m_new = jnp.maximum(m_old, jnp.max(s, axis=1, keepdims=True))
        alpha = jnp.exp(m_old - m_new)
        p = jnp.exp(s - m_new)
        l_scr[...] = alpha * l_scr[...] + jnp.sum(p, axis=1, keepdims=True)
        acc_scr[...] = alpha * acc_scr[...] + _dot(p.astype(BF16), v_ref[...].astype(BF16))
        m_scr[...] = m_new

    @pl.when(ki == qi)
    def _():
        o = _sub_norm(acc_scr[...], l_scr[...], lam_ref[...], sw_ref[...], post_scale)
        o_ref[...] = o.astype(o_ref.dtype)


def _attn_prompt(u, lam, subw, *, t, heads, tq, post_scale):
    nq = t // tq
    return pl.pallas_call(
        functools.partial(_attn_prompt_body, tq=tq, post_scale=post_scale),
        grid=(heads, nq, nq),
        in_specs=[pl.BlockSpec((1, LANES), lambda h, i, j: (0, 0)),
                  pl.BlockSpec((tq, LANES), lambda h, i, j: (i, h)),
                  pl.BlockSpec((tq, LANES), lambda h, i, j: (jnp.minimum(i, j), heads + h)),
                  pl.BlockSpec((tq, LANES), lambda h, i, j: (jnp.minimum(i, j), 2 * heads + h)),
                  pl.BlockSpec((1, LANES), lambda h, i, j: (0, 0))],
        out_specs=pl.BlockSpec((tq, LANES), lambda h, i, j: (i, h)),
        out_shape=jax.ShapeDtypeStruct((t, heads * LANES), BF16),
        scratch_shapes=[pltpu.VMEM((2 * tq, LANES), BF16),
                        pltpu.VMEM((2 * tq, 1), F32),
                        pltpu.VMEM((2 * tq, 1), F32),
                        pltpu.VMEM((2 * tq, LANES), F32)],
        compiler_params=_cparams(("parallel", "parallel", "arbitrary")),
        name="diff_attn_prompt",
    )(lam, u, u, u, subw)


def _attn_sample_body(pt_ref, lam_ref, q_ref, kc_ref, vc_ref, kn_ref, vn_ref, sw_ref, o_ref,
                      q2_scr, m_scr, l_scr, acc_scr, *, heads, tn, post_scale):
    j = pl.program_id(1)
    n_pages = pl.num_programs(1) - 1
    rows = 2 * tn

    @pl.when(j == 0)
    def _():
        for h in range(heads):
            q2_scr[h] = _two_maps(q_ref[:, h * LANES:(h + 1) * LANES])
        m_scr[...] = jnp.full_like(m_scr, NEG_INF)
        l_scr[...] = jnp.zeros_like(l_scr)
        acc_scr[...] = jnp.zeros_like(acc_scr)

    def attend(k_ref, v_ref, masked):
        for h in range(heads):
            kh = k_ref[:, h, :].astype(BF16)
            vh = v_ref[:, h, :].astype(BF16)
            s = _dot_nt(q2_scr[h], kh)
            if masked:
                rowq = lax.broadcasted_iota(jnp.int32, (rows, PAGE), 0) % tn
                col = lax.broadcasted_iota(jnp.int32, (rows, PAGE), 1)
                s = jnp.where(col <= rowq, s, NEG_INF)
            sl = slice(h * rows, (h + 1) * rows)
            m_old = m_scr[sl]
            m_new = jnp.maximum(m_old, jnp.max(s, axis=1, keepdims=True))
            alpha = jnp.exp(m_old - m_new)
            p = jnp.exp(s - m_new)
            l_scr[sl] = alpha * l_scr[sl] + jnp.sum(p, axis=1, keepdims=True)
            acc_scr[sl] = alpha * acc_scr[sl] + _dot(p.astype(BF16), vh)
            m_scr[sl] = m_new

    @pl.when(j < n_pages)
    def _():
        attend(kc_ref, vc_ref, False)

    @pl.when(j == n_pages)
    def _():
        attend(kn_ref, vn_ref, True)
        for h in range(heads):
            sl = slice(h * rows, (h + 1) * rows)
            o = _sub_norm(acc_scr[sl], l_scr[sl], lam_ref[...], sw_ref[...], post_scale)
            o_ref[:, h * LANES:(h + 1) * LANES] = o


def _attn_sample(q, cache_k, cache_v, k_new, v_new, page_table, lam, subw, *, layer, post_scale):
    b, n_pages = page_table.shape
    heads = cache_k.shape[3]
    tn = q.shape[0] // b
    rows = 2 * tn
    pt = page_table.reshape(-1)

    def page_map(i, j, pt_ref):
        return (layer, pt_ref[i * n_pages + jnp.minimum(j, n_pages - 1)], 0, 0, 0)

    page_spec = pl.BlockSpec((None, None, PAGE, heads, LANES), page_map)
    new_spec = pl.BlockSpec((None, PAGE, heads, LANES), lambda i, j, pt_ref: (i, 0, 0, 0))
    vec = pl.BlockSpec((1, LANES), lambda i, j, pt_ref: (0, 0))
    grid_spec = pltpu.PrefetchScalarGridSpec(
        num_scalar_prefetch=1,
        grid=(b, n_pages + 1),
        in_specs=[vec,
                  pl.BlockSpec((tn, heads * LANES), lambda i, j, pt_ref: (i, 0)),
                  page_spec, page_spec, new_spec, new_spec, vec],
        out_specs=pl.BlockSpec((tn, heads * LANES), lambda i, j, pt_ref: (i, 0)),
        scratch_shapes=[pltpu.VMEM((heads, rows, LANES), BF16),
                        pltpu.VMEM((heads * rows, 1), F32),
                        pltpu.VMEM((heads * rows, 1), F32),
                        pltpu.VMEM((heads * rows, LANES), F32)],
    )
    return pl.pallas_call(
        functools.partial(_attn_sample_body, heads=heads, tn=tn, post_scale=post_scale),
        grid_spec=grid_spec,
        out_shape=jax.ShapeDtypeStruct((b * tn, heads * LANES), F32),
        compiler_params=_cparams(("parallel", "arbitrary")),
        name="diff_attn_sample",
    )(pt, lam, q, cache_k, cache_v, k_new, v_new, subw)


def _pick_tile(n, candidates):
    for c in candidates:
        if n % c == 0:
            return c
    raise ValueError(f"no tile for {n}")


def _pair_states(s):
    b, h, _, _ = s.shape
    st = jnp.swapaxes(s, -1, -2).reshape(b, h // 2, 2, HS_R, HS_R)
    z = jnp.zeros_like(st[:, :, 0])
    top = jnp.concatenate([st[:, :, 0], z], axis=-1)
    bot = jnp.concatenate([z, st[:, :, 1]], axis=-1)
    return jnp.concatenate([top, bot], axis=-2)


def _head_states(sp):
    b, hp, _, _ = sp.shape
    s0 = sp[:, :, :HS_R, :HS_R]
    s1 = sp[:, :, HS_R:, HS_R:]
    st = jnp.stack([s0, s1], axis=2).reshape(b, 2 * hp, HS_R, HS_R)
    return jnp.swapaxes(st, -1, -2)


def _layer(l, depth, xp, xs, cache_k, cache_v, state_wkv, state_shift, page_table,
           w_in, mu_shift, w_decay0, w_decay2, a0, a2, g2, k_k, k_a, r_k, lnx_w, lnx_b,
           lambda_q1, lambda_k1, lambda_q2, lambda_k2, subln_w,
           w_pa, w_pb, w_gate, b_gate, w_o, ln1_g, ln1_b,
           w_ffn_gate, w_ffn_up, w_ffn_down, ln2_g, ln2_b):
    bp, sp, d = xp.shape
    bs, ts, _ = xs.shape
    assert bp == 1
    heads = cache_k.shape[3]
    att_w = heads * LANES
    rw_w = w_decay0.shape[1]
    h_r = rw_w // HS_R
    rw_in = 3 * rw_w + W_LORA + A_LORA + G_LORA
    assert w_in.shape[2] == 3 * att_w + rw_in and att_w == rw_w
    alpha = (2 * depth) ** 0.25
    lam_init = 0.8 - 0.6 * math.exp(-0.3 * l)
    post_scale = 1.0 - lam_init

    lam = (jnp.exp(jnp.sum(lambda_q1[l] * lambda_k1[l])) - jnp.exp(jnp.sum(lambda_q2[l] * lambda_k2[l]))
           + lam_init)
    lam = jnp.broadcast_to(lam.astype(F32), (1, LANES))

    n_p = bp * sp
    n_s = bs * ts
    m = n_p + n_s
    x = jnp.concatenate([xp.reshape(n_p, d), xs.reshape(n_s, d)], axis=0)
    xb = x.astype(BF16)

    o_rw = 3 * att_w
    c_wl = o_rw + rw_w
    c_k = c_wl + W_LORA
    c_v = c_k + rw_w
    c_al = c_v + rw_w
    c_gl = c_al + A_LORA

    def permute_rw(t, axis):
        take = lambda a, b: lax.slice_in_dim(t, a, b, axis=axis)
        shape = list(t.shape)
        shape[axis] = LORA_PAD - W_LORA
        z = jnp.zeros(shape, t.dtype)
        o = 0 if t.shape[axis] == w_in.shape[2] else o_rw
        return jnp.concatenate([take(o_rw - o, c_wl - o), take(c_k - o, c_v - o), take(c_v - o, c_al - o),
                                take(c_wl - o, c_k - o), z, take(c_al - o, c_gl - o), z,
                                take(c_gl - o, c_gl - o + G_LORA)], axis=axis)

    w_in_l = w_in[l]
    w_all = jnp.concatenate([w_in_l[:, :o_rw], permute_rw(w_in_l, 1)], axis=1).astype(BF16)
    n_all = w_all.shape[1]
    rw_cols = n_all - o_rw

    tm = _pick_tile(m, (768, 512, 384, 256, 128, 64, 32, 16, 8))
    u = _matmul(xb, w_all, tm=tm, tn=_pick_tile(n_all, (512, 256, 128)), out_dtype=F32, name="in_proj")

    mu = permute_rw(mu_shift[l][None, :], 1)
    pad_rows = lambda t: jnp.concatenate([t, jnp.zeros((LORA_PAD - t.shape[0], t.shape[1]), t.dtype)], axis=0)
    w2 = pad_rows(w_decay2[l]).astype(BF16)
    a2b = pad_rows(a2[l]).astype(BF16)
    g2b = g2[l].astype(BF16)
    row = lambda t: t[l][None, :]
    prep_args = (mu, row(w_decay0), w2, row(a0), a2b, g2b, row(k_k), row(k_a))

    tp = _pick_tile(sp, (128, 64, 32, 16, 8))
    nblk_p = sp // tp
    u_rw = lax.slice(u, (0, o_rw), (m, n_all))
    prev_p = jnp.concatenate([jnp.zeros((1, rw_cols), F32), u_rw[tp - 1:sp - 1:tp]], axis=0)
    prev_p = prev_p.reshape(nblk_p, 1, rw_cols)
    prep_p = _rwkv_prep(u, 0, sp, prev_p, *prep_args, tm=tp, width=rw_w, col0=o_rw)

    prev_s = permute_rw(state_shift[l], 1).reshape(bs, 1, rw_cols)
    prep_s = _rwkv_prep(u, n_p, n_s, prev_s, *prep_args, tm=ts, width=rw_w, col0=o_rw)

    wkv_params = (row(lnx_w), row(lnx_b), r_k[l].reshape(1, rw_w))
    s0_p = jnp.zeros((bp, h_r // 2, LANES, LANES), F32)
    o_rw_p, st_p = _wkv(*prep_p, s0_p, *wkv_params, seqs=bp, chunk=WKV_CHUNK)

    t_pad = -(-ts // WKV_CHUNK) * WKV_CHUNK
    pad_t = lambda t: jnp.pad(t.reshape(bs, ts, rw_w), ((0, 0), (0, t_pad - ts), (0, 0))).reshape(bs * t_pad, rw_w)
    s0_s = _pair_states(state_wkv[l].astype(F32))
    o_rw_s, st_s = _wkv(*[pad_t(t) for t in prep_s], s0_s, *wkv_params, seqs=bs, chunk=WKV_CHUNK)
    o_rw_s = o_rw_s.reshape(bs, t_pad, rw_w)[:, :ts].reshape(n_s, rw_w)
    o_rwkv = jnp.concatenate([o_rw_p, o_rw_s], axis=0)

    new_wkv_p = _head_states(st_p).astype(state_wkv.dtype)
    new_wkv_s = _head_states(st_s).astype(state_wkv.dtype)

    def shift_row(rows):
        t = u_rw[rows]
        w_ = rw_w
        return jnp.concatenate([t[:, :w_], t[:, 3 * w_:3 * w_ + W_LORA], t[:, w_:3 * w_],
                                t[:, 3 * w_ + LORA_PAD:3 * w_ + LORA_PAD + A_LORA],
                                t[:, 3 * w_ + 2 * LORA_PAD:]], axis=1)

    new_shift_p = shift_row(jnp.array([sp - 1]))
    new_shift_s = shift_row(n_p + ts - 1 + ts * jnp.arange(bs))

    subw = subln_w[l][None, :]
    tq = _pick_tile(sp, (512, 256, 128))
    o_att_p = _attn_prompt(u, lam, subw, t=sp, heads=heads, tq=tq, post_scale=post_scale)

    q_s = lax.slice(u, (n_p, 0), (m, att_w))
    k_s = lax.slice(u, (n_p, att_w), (m, 2 * att_w)).reshape(bs, ts, heads, DV)
    v_s = lax.slice(u, (n_p, 2 * att_w), (m, 3 * att_w)).reshape(bs, ts, heads, DV)
    pad_new = lambda t: jnp.pad(t, ((0, 0), (0, PAGE - ts), (0, 0), (0, 0)))
    o_att_s = _attn_sample(q_s, cache_k, cache_v, pad_new(k_s), pad_new(v_s), page_table, lam, subw,
                           layer=l, post_scale=post_scale)
    o_att = jnp.concatenate([o_att_p, o_att_s.astype(BF16)], axis=0)

    wg = w_gate[l].astype(BF16)
    bg = b_gate[l][None, :]
    mixin = _gated_merge(xb, o_att, o_rwkv, wg[:, :d], wg[:, d:], bg[:, :d], bg[:, d:],
                         w_pa[l].astype(BF16), w_pb[l].astype(BF16), tm=tm, tn=_pick_tile(d, (256, 128)))
    pre1 = _matmul_resid(mixin, w_o[l].astype(BF16), x, alpha=alpha, tm=tm, tn=_pick_tile(d, (512, 256, 128)),
                         name="out_proj")
    tl = _pick_tile(m, (256, 128, 64, 32, 16, 8))
    x1, x1b = _layer_norm(pre1, row(ln1_g), row(ln1_b), tm=tl)

    d_ff = w_ffn_gate.shape[2]
    hmid = _swiglu(x1b, w_ffn_gate[l].astype(BF16), w_ffn_up[l].astype(BF16), tm=tm,
                   tn=_pick_tile(d_ff, (256, 128)))
    tmd = _pick_tile(m, (384, 256, 128, 64, 32, 16, 8))
    pre2 = _matmul_resid(hmid, w_ffn_down[l].astype(BF16), x1, alpha=alpha, tm=tmd,
                         tn=_pick_tile(d, (256, 128)), name="ffn_down")
    y, _ = _layer_norm(pre2, row(ln2_g), row(ln2_b), tm=tl)

    k_all = lax.slice(u, (0, att_w), (m, 2 * att_w))
    v_all = lax.slice(u, (0, 2 * att_w), (m, 3 * att_w))
    outs = dict(
        xp=y[:n_p].reshape(bp, sp, d), xs=y[n_p:].reshape(bs, ts, d),
        kp=k_all[:n_p].reshape(bp, sp, heads, DV), vp=v_all[:n_p].reshape(bp, sp, heads, DV),
        wp=new_wkv_p, sp=new_shift_p,
        ks=k_s, vs=v_s, ws=new_wkv_s, ss=new_shift_s)
    return outs


def kernel(x_prompt, x_sample, cache_k, cache_v, state_wkv, state_shift, page_table, w_in, mu_shift, w_decay0, w_decay2, a0, a2, g2, k_k, k_a, r_k, lnx_w, lnx_b, lambda_q1, lambda_k1, lambda_q2, lambda_k2, subln_w, w_pa, w_pb, w_gate, b_gate, w_o, ln1_g, ln1_b, w_ffn_gate, w_ffn_up, w_ffn_down, ln2_g, ln2_b):
    depth = w_in.shape[0]
    xp, xs = x_prompt, x_sample
    acc = {n: [] for n in ("kp", "vp", "wp", "sp", "ks", "vs", "ws", "ss")}
    for l in range(depth):
        o = _layer(l, depth, xp, xs, cache_k, cache_v, state_wkv, state_shift, page_table,
                   w_in, mu_shift, w_decay0, w_decay2, a0, a2, g2, k_k, k_a, r_k, lnx_w, lnx_b,
                   lambda_q1, lambda_k1, lambda_q2, lambda_k2, subln_w,
                   w_pa, w_pb, w_gate, b_gate, w_o, ln1_g, ln1_b,
                   w_ffn_gate, w_ffn_up, w_ffn_down, ln2_g, ln2_b)
        xp, xs = o["xp"], o["xs"]
        for n in acc:
            acc[n].append(o[n])
    st = lambda n: jnp.stack(acc[n])
    return (xp, xs, st("kp"), st("vp"), st("wp"), st("sp"), st("ks"), st("vs"), st("ws"), st("ss"))
```

```python
import functools
import math

import jax
import jax.numpy as jnp
from jax import lax
from jax.experimental import pallas as pl
from jax.experimental.pallas import tpu as pltpu

F32 = jnp.float32
BF16 = jnp.bfloat16

LANES = 128
VMEM_LIMIT = 56 * 1024 * 1024

DK = 64
DV = 128
HS_R = 64
W_LORA = 96
A_LORA = 96
G_LORA = 256
LORA_PAD = 128
ATTN_SCALE = DK ** -0.5
SUBLN_EPS = 1e-5
NEG_INF = -1e30
GN_EPS = 64e-5
LN_EPS = 1e-5
PAGE = 128
WKV_CHUNK = 64
WKV_PAIRS = 8


def _cparams(sem):
    return pltpu.CompilerParams(dimension_semantics=sem, vmem_limit_bytes=VMEM_LIMIT)


def _dot(a, b):
    return jnp.dot(a, b, preferred_element_type=F32)


def _dot_nt(a, b):
    return lax.dot_general(a, b, (((1,), (1,)), ((), ())), preferred_element_type=F32)


def _dot_tn(a, b):
    return lax.dot_general(a, b, (((0,), (0,)), ((), ())), preferred_element_type=F32)


def _sigmoid(x):
    return 1.0 / (1.0 + jnp.exp(-x))


def _mm_body(x_ref, w_ref, o_ref):
    o_ref[...] = _dot(x_ref[...], w_ref[...]).astype(o_ref.dtype)


def _matmul(x, w, *, tm, tn, out_dtype, name):
    m, k = x.shape
    n = w.shape[1]
    return pl.pallas_call(
        _mm_body,
        grid=(m // tm, n // tn),
        in_specs=[pl.BlockSpec((tm, k), lambda i, j: (i, 0)),
                  pl.BlockSpec((k, tn), lambda i, j: (0, j))],
        out_specs=pl.BlockSpec((tm, tn), lambda i, j: (i, j)),
        out_shape=jax.ShapeDtypeStruct((m, n), out_dtype),
        compiler_params=_cparams(("parallel", "parallel")),
        name=name,
    )(x, w)


def _mm_resid_body(x_ref, w_ref, r_ref, o_ref, *, alpha):
    o_ref[...] = alpha * r_ref[...] + _dot(x_ref[...], w_ref[...])


def _matmul_resid(x, w, resid, *, alpha, tm, tn, name):
    m, k = x.shape
    n = w.shape[1]
    return pl.pallas_call(
        functools.partial(_mm_resid_body, alpha=alpha),
        grid=(m // tm, n // tn),
        in_specs=[pl.BlockSpec((tm, k), lambda i, j: (i, 0)),
                  pl.BlockSpec((k, tn), lambda i, j: (0, j)),
                  pl.BlockSpec((tm, tn), lambda i, j: (i, j))],
        out_specs=pl.BlockSpec((tm, tn), lambda i, j: (i, j)),
        out_shape=jax.ShapeDtypeStruct((m, n), F32),
        compiler_params=_cparams(("parallel", "parallel")),
        name=name,
    )(x, w, resid)


def _gated_merge_body(x_ref, oa_ref, ob_ref, wga_ref, wgb_ref, ba_ref, bb_ref, wpa_ref, wpb_ref, o_ref):
    x = x_ref[...]
    ga = _sigmoid(_dot(x, wga_ref[...]) + ba_ref[...])
    gb = _sigmoid(_dot(x, wgb_ref[...]) + bb_ref[...])
    o = ga * _dot(oa_ref[...], wpa_ref[...]) + gb * _dot(ob_ref[...], wpb_ref[...])
    o_ref[...] = o.astype(o_ref.dtype)


def _gated_merge(x, oa, ob, wga, wgb, ba, bb, wpa, wpb, *, tm, tn):
    m, d = x.shape
    ka = oa.shape[1]
    kb = ob.shape[1]
    n = wga.shape[1]
    row = lambda w: pl.BlockSpec((tm, w), lambda i, j: (i, 0))
    col = lambda k: pl.BlockSpec((k, tn), lambda i, j: (0, j))
    return pl.pallas_call(
        _gated_merge_body,
        grid=(m // tm, n // tn),
        in_specs=[row(d), row(ka), row(kb), col(d), col(d), col(1), col(1), col(ka), col(kb)],
        out_specs=pl.BlockSpec((tm, tn), lambda i, j: (i, j)),
        out_shape=jax.ShapeDtypeStruct((m, n), BF16),
        compiler_params=_cparams(("parallel", "parallel")),
        name="gated_merge",
    )(x, oa, ob, wga, wgb, ba, bb, wpa, wpb)


def _swiglu_body(x_ref, wg_ref, wu_ref, o_ref):
    x = x_ref[...]
    g = _dot(x, wg_ref[...])
    u = _dot(x, wu_ref[...])
    o_ref[...] = (g * _sigmoid(g) * u).astype(o_ref.dtype)


def _swiglu(x, wg, wu, *, tm, tn):
    m, k = x.shape
    n = wg.shape[1]
    return pl.pallas_call(
        _swiglu_body,
        grid=(m // tm, n // tn),
        in_specs=[pl.BlockSpec((tm, k), lambda i, j: (i, 0)),
                  pl.BlockSpec((k, tn), lambda i, j: (0, j)),
                  pl.BlockSpec((k, tn), lambda i, j: (0, j))],
        out_specs=pl.BlockSpec((tm, tn), lambda i, j: (i, j)),
        out_shape=jax.ShapeDtypeStruct((m, n), BF16),
        compiler_params=_cparams(("parallel", "parallel")),
        name="swiglu",
    )(x, wg, wu)


def _layer_norm_body(x_ref, g_ref, b_ref, o_ref, ob_ref):
    x = x_ref[...]
    mu = jnp.mean(x, axis=-1, keepdims=True)
    d = x - mu
    var = jnp.mean(d * d, axis=-1, keepdims=True)
    y = d * lax.rsqrt(var + LN_EPS) * g_ref[...] + b_ref[...]
    o_ref[...] = y
    ob_ref[...] = y.astype(BF16)


def _layer_norm(x, g, b, *, tm):
    m, d = x.shape
    return pl.pallas_call(
        _layer_norm_body,
        grid=(m // tm,),
        in_specs=[pl.BlockSpec((tm, d), lambda i: (i, 0)),
                  pl.BlockSpec((1, d), lambda i: (0, 0)),
                  pl.BlockSpec((1, d), lambda i: (0, 0))],
        out_specs=[pl.BlockSpec((tm, d), lambda i: (i, 0)),
                   pl.BlockSpec((tm, d), lambda i: (i, 0))],
        out_shape=[jax.ShapeDtypeStruct((m, d), F32), jax.ShapeDtypeStruct((m, d), BF16)],
        compiler_params=_cparams(("parallel",)),
        name="layer_norm",
    )(x, g, b)


def _rwkv_prep_body(ur_ref, uk_ref, uv_ref, ul_ref, prev_ref, mu_ref, w0_ref, w2_ref, a0_ref, a2_ref,
                    g2_ref, kk_ref, ka_ref,
                    r_out, lw_out, k_out, v_out, kk_out, a_out, g_out, *, tm, width):
    first = lax.broadcasted_iota(jnp.int32, (tm, 1), 0) == 0

    def mixed(u_ref, lo, hi):
        u = u_ref[...]
        prev = jnp.broadcast_to(prev_ref[:, lo:hi], (tm, hi - lo))
        shifted = jnp.where(first, prev, pltpu.roll(u, shift=1, axis=0))
        return u + (shifted - u) * mu_ref[:, lo:hi]

    w = width
    r = mixed(ur_ref, 0, w)
    k = mixed(uk_ref, w, 2 * w)
    v = mixed(uv_ref, 2 * w, 3 * w)
    lora = mixed(ul_ref, 3 * w, 3 * w + 2 * LORA_PAD + G_LORA)
    w_l = lora[:, 0:LORA_PAD]
    a_l = lora[:, LORA_PAD:2 * LORA_PAD]
    g_l = lora[:, 2 * LORA_PAD:]

    z = w0_ref[...] + _dot(jnp.tanh(w_l).astype(BF16), w2_ref[...])
    nz = -z
    softplus = jnp.maximum(nz, 0.0) + jnp.log(1.0 + jnp.exp(-jnp.abs(nz)))
    wdec = -softplus - 0.5
    a = _sigmoid(a0_ref[...] + _dot(a_l.astype(BF16), a2_ref[...]))
    g = _dot(_sigmoid(g_l).astype(BF16), g2_ref[...])

    r_out[...] = r
    lw_out[...] = -jnp.exp(wdec)
    k_out[...] = k * (1.0 + (a - 1.0) * ka_ref[...])
    v_out[...] = v
    kk_out[...] = k * kk_ref[...]
    a_out[...] = a
    g_out[...] = g


def _rwkv_prep(u, row0, rows, prev, mu, w0, w2, a0, a2, g2, k_k, k_a, *, tm, width, col0):
    nblk = rows // tm
    rb0 = row0 // tm
    cb = col0 // width
    lw = 2 * LORA_PAD + G_LORA
    lb = (col0 + 3 * width) // lw
    n_prev = prev.shape[1]
    rw_w = prev.shape[2]
    vec = lambda n: pl.BlockSpec((1, n), lambda i: (0, 0))
    out = jax.ShapeDtypeStruct((rows, width), F32)
    body = functools.partial(_rwkv_prep_body, tm=tm, width=width)
    return pl.pallas_call(
        body,
        grid=(nblk,),
        in_specs=[pl.BlockSpec((tm, width), lambda i: (rb0 + i, cb)),
                  pl.BlockSpec((tm, width), lambda i: (rb0 + i, cb + 1)),
                  pl.BlockSpec((tm, width), lambda i: (rb0 + i, cb + 2)),
                  pl.BlockSpec((tm, lw), lambda i: (rb0 + i, lb)),
                  pl.BlockSpec((None, n_prev, rw_w), lambda i: (i, 0, 0)),
                  vec(rw_w), vec(width),
                  pl.BlockSpec((LORA_PAD, width), lambda i: (0, 0)),
                  vec(width),
                  pl.BlockSpec((LORA_PAD, width), lambda i: (0, 0)),
                  pl.BlockSpec((G_LORA, width), lambda i: (0, 0)),
                  vec(width), vec(width)],
        out_specs=[pl.BlockSpec((tm, width), lambda i: (i, 0))] * 7,
        out_shape=[out] * 7,
        compiler_params=_cparams(("parallel",)),
        name="rwkv_prep",
    )(u, u, u, u, prev, mu, w0, w2, a0, a2, g2, k_k, k_a)


def _wkv_pair(r, lw, k, v, kk, a, g, st, lnw, lnb, rk, *, chunk):
    C = chunk
    C2 = 2 * C

    lane = lax.broadcasted_iota(jnp.int32, (C2, LANES), 1)
    rowi = lax.broadcasted_iota(jnp.int32, (C2, LANES), 0)
    own = (lane >= HS_R) == (rowi >= C)

    def twice(x):
        return jnp.concatenate([x, x], axis=0)

    def stack(x):
        return jnp.where(own, twice(x), 0.0)

    tr = lax.broadcasted_iota(jnp.int32, (C, C), 0)
    tc = lax.broadcasted_iota(jnp.int32, (C, C), 1)
    tri = (tr >= tc).astype(BF16)
    lw_hi = lw.astype(BF16)
    rem = lw - lw_hi.astype(F32)
    lw_mid = rem.astype(BF16)
    lw_lo = (rem - lw_mid.astype(F32)).astype(BF16)
    G = _dot(tri, lw_hi) + _dot(tri, lw_mid) + _dot(tri, lw_lo)
    yield None
    GC = G[C - 1:C, :]
    e_prev = jnp.exp(G - lw)
    e_neg = jnp.exp(-G)
    e_pos = jnp.exp(G)
    e_rest = jnp.exp(GC - G)
    e_all = jnp.exp(GC)

    kk2 = stack(kk)
    nrm = jnp.sqrt(jnp.sum(kk2 * kk2, axis=1, keepdims=True))
    kk2 = kk2 / jnp.maximum(nrm, 1e-12)

    A2 = (kk2 * twice(-e_prev)).astype(BF16)
    B2 = (kk2 * twice(a * e_neg)).astype(BF16)
    K2 = stack(k * e_neg).astype(BF16)
    R2 = stack(r * e_pos).astype(BF16)
    Bt2 = (kk2 * twice(a * e_rest)).astype(BF16)
    Kt2 = stack(k * e_rest).astype(BF16)
    V2f = stack(v)
    V2 = V2f.astype(BF16)

    mr = lax.broadcasted_iota(jnp.int32, (C2, C2), 0)
    mc = lax.broadcasted_iota(jnp.int32, (C2, C2), 1)
    strict = mr > mc
    incl = mr >= mc
    eye = (mr == mc).astype(F32)

    m_ab = jnp.where(strict, _dot_nt(A2, B2), 0.0)
    m_ak = jnp.where(strict, _dot_nt(A2, K2), 0.0).astype(BF16)
    m_rb = jnp.where(incl, _dot_nt(R2, B2), 0.0).astype(BF16)
    m_rk = jnp.where(incl, _dot_nt(R2, K2), 0.0).astype(BF16)
    yield None

    t_inv = eye + m_ab
    p = m_ab
    n = 1
    while 2 * n < C:
        pb = p.astype(BF16)
        p = _dot(pb, pb)
        t_inv = _dot((eye + p).astype(BF16), t_inv.astype(BF16))
        n *= 2
        yield None

    stb = st.astype(BF16)
    x0 = _dot(A2, stb) + _dot(m_ak, V2)
    yield None
    u2 = _dot(t_inv.astype(BF16), x0.astype(BF16))
    yield None
    u2b = u2.astype(BF16)
    y2 = _dot(R2, stb) + _dot(m_rb, u2b) + _dot(m_rk, V2)

    decay_col = jnp.broadcast_to(e_all, (LANES, LANES)).T
    st_new = decay_col * st + _dot_tn(Bt2, u2b) + _dot_tn(Kt2, V2)
    yield None

    inv_hs = 1.0 / HS_R
    mean = jnp.sum(y2, axis=1, keepdims=True) * inv_hs
    d = jnp.where(own, y2 - mean, 0.0)
    var = jnp.sum(d * d, axis=1, keepdims=True) * inv_hs
    yn2 = d * lax.rsqrt(var + GN_EPS)
    bonus2 = jnp.sum(stack(r * k * rk), axis=1, keepdims=True) * V2f
    yn = yn2[:C] + yn2[C:]
    bonus = bonus2[:C] + bonus2[C:]
    yield (yn * lnw + lnb + bonus) * g, st_new


def _wkv_body(r_ref, lw_ref, k_ref, v_ref, kk_ref, a_ref, g_ref, s0_ref, lnw_ref, lnb_ref, rk_ref,
              o_ref, st_ref, st_scr, *, chunk, pairs):
    c = pl.program_id(2)

    @pl.when(c == 0)
    def _():
        st_scr[...] = s0_ref[...]

    lanes = [slice(pp * LANES, (pp + 1) * LANES) for pp in range(pairs)]
    gens = [_wkv_pair(r_ref[:, sl], lw_ref[:, sl], k_ref[:, sl], v_ref[:, sl], kk_ref[:, sl],
                      a_ref[:, sl], g_ref[:, sl], st_scr[pp], lnw_ref[:, sl], lnb_ref[:, sl],
                      rk_ref[:, sl], chunk=chunk) for pp, sl in enumerate(lanes)]
    results = [None] * pairs
    while any(res is None for res in results):
        results = [next(gen) for gen in gens]
    for pp, (o, st_new) in enumerate(results):
        st_scr[pp] = st_new
        o_ref[:, lanes[pp]] = o.astype(o_ref.dtype)

    @pl.when(c == pl.num_programs(2) - 1)
    def _():
        st_ref[...] = st_scr[...]


def _wkv(r, lw, k, v, kk, a, g, s0, lnw, lnb, rk, *, seqs, chunk, pairs):
    rows, width = r.shape
    t = rows // seqs
    nchunk = t // chunk
    npair = width // LANES
    wblk = pairs * LANES
    blk = pl.BlockSpec((chunk, wblk), lambda b, p, c: (b * nchunk + c, p))
    vec = pl.BlockSpec((1, wblk), lambda b, p, c: (0, p))
    stspec = pl.BlockSpec((None, pairs, LANES, LANES), lambda b, p, c: (b, p, 0, 0))
    return pl.pallas_call(
        functools.partial(_wkv_body, chunk=chunk, pairs=pairs),
        grid=(seqs, npair // pairs, nchunk),
        in_specs=[blk] * 7 + [stspec, vec, vec, vec],
        out_specs=[blk, stspec],
        out_shape=[jax.ShapeDtypeStruct((rows, width), BF16),
                   jax.ShapeDtypeStruct((seqs, npair, LANES, LANES), F32)],
        scratch_shapes=[pltpu.VMEM((pairs, LANES, LANES), F32)],
        compiler_params=_cparams(("parallel", "parallel", "arbitrary")),
        name="wkv7_chunked",
    )(r, lw, k, v, kk, a, g, s0, lnw, lnb, rk)


def _two_maps(q):
    lane = lax.broadcasted_iota(jnp.int32, q.shape, 1)
    q = q * ATTN_SCALE
    return jnp.where(lane < DK, q, 0.0), jnp.where(lane >= DK, q, 0.0)


def _attn_prompt_body(qi_ref, ki_ref, lam_ref, q_ref, k_ref, vt_ref, sw_ref, o_ref,
                      q2_scr, m_scr, l_scr, acc_scr, *, tq, post_scale):
    step = pl.program_id(1)
    qi = qi_ref[step]
    ki = ki_ref[step]

    @pl.when(ki == 0)
    def _():
        q1, q2 = _two_maps(q_ref[...])
        q2_scr[...] = jnp.concatenate([q1, q2], axis=0).astype(BF16)
        m_scr[...] = jnp.full_like(m_scr, NEG_INF)
        l_scr[...] = jnp.zeros_like(l_scr)
        acc_scr[...] = jnp.zeros_like(acc_scr)

    def update(diagonal):
        st = _dot_nt(k_ref[...].astype(BF16), q2_scr[...])
        if diagonal:
            krow = lax.broadcasted_iota(jnp.int32, st.shape, 0)
            qcol = lax.broadcasted_iota(jnp.int32, st.shape, 1) % tq
            st = jnp.where(krow <= qcol, st, NEG_INF)
        m_old = m_scr[...]
        m_new = jnp.maximum(m_old, jnp.max(st, axis=0, keepdims=True))
        alpha = jnp.exp(m_old - m_new)
        p = jnp.exp(st - m_new)
        l_scr[...] = alpha * l_scr[...] + jnp.sum(p, axis=0, keepdims=True)
        acc_scr[...] = alpha * acc_scr[...] + _dot(vt_ref[...], p.astype(BF16))
        m_scr[...] = m_new

    @pl.when(ki < qi)
    def _():
        update(False)

    @pl.when(ki == qi)
    def _():
        update(True)
        acc = acc_scr[...]
        l = l_scr[...]
        ot = acc[:, :tq] / l[:, :tq] - lam_ref[:, 0:1] * (acc[:, tq:] / l[:, tq:])
        ms = jnp.mean(ot * ot, axis=0, keepdims=True)
        on = ot * lax.rsqrt(ms + SUBLN_EPS)
        o_ref[...] = (on.T * sw_ref[...] * post_scale).astype(o_ref.dtype)


def _attn_prompt(u, vt, lam, subw, *, t, heads, tq, post_scale):
    nq = t // tq
    pairs = [(i, j) for i in range(nq) for j in range(i + 1)]
    qi_tab = jnp.array([p[0] for p in pairs], jnp.int32)
    ki_tab = jnp.array([p[1] for p in pairs], jnp.int32)
    vec = pl.BlockSpec((1, LANES), lambda h, s, qi, ki: (0, 0))
    grid_spec = pltpu.PrefetchScalarGridSpec(
        num_scalar_prefetch=2,
        grid=(heads, len(pairs)),
        in_specs=[vec,
                  pl.BlockSpec((tq, LANES), lambda h, s, qi, ki: (qi[s], h)),
                  pl.BlockSpec((tq, LANES), lambda h, s, qi, ki: (ki[s], heads + h)),
                  pl.BlockSpec((LANES, tq), lambda h, s, qi, ki: (h, ki[s])),
                  vec],
        out_specs=pl.BlockSpec((tq, LANES), lambda h, s, qi, ki: (qi[s], h)),
        scratch_shapes=[pltpu.VMEM((2 * tq, LANES), BF16),
                        pltpu.VMEM((1, 2 * tq), F32),
                        pltpu.VMEM((1, 2 * tq), F32),
                        pltpu.VMEM((LANES, 2 * tq), F32)],
    )
    return pl.pallas_call(
        functools.partial(_attn_prompt_body, tq=tq, post_scale=post_scale),
        grid_spec=grid_spec,
        out_shape=jax.ShapeDtypeStruct((t, heads * LANES), BF16),
        compiler_params=_cparams(("parallel", "arbitrary")),
        name="diff_attn_prompt",
    )(qi_tab, ki_tab, lam, u, u, vt, subw)


def _attn_sample_body(pt_ref, lam_ref, q_ref, kc_ref, vc_ref, kn_ref, vn_ref, bias_ref, biasn_ref, sw_ref,
                      o_ref, q2_scr, m_scr, l_scr, acc_scr, *, heads, tn, post_scale):
    j = pl.program_id(1)
    n_pages = pl.num_programs(1) - 1
    half = heads * tn

    @pl.when(j == 0)
    def _():
        for h in range(heads):
            q1, q2 = _two_maps(q_ref[:, h * LANES:(h + 1) * LANES])
            q2_scr[h * tn:(h + 1) * tn] = q1
            q2_scr[half + h * tn:half + (h + 1) * tn] = q2
        m_scr[...] = jnp.full_like(m_scr, NEG_INF)
        l_scr[...] = jnp.zeros_like(l_scr)
        acc_scr[...] = jnp.zeros_like(acc_scr)

    def attend(k_ref, v_ref, b_ref):
        k2 = k_ref[...].reshape(PAGE * heads, LANES).astype(BF16)
        v2 = v_ref[...].reshape(PAGE * heads, LANES).astype(BF16)
        s = _dot_nt(q2_scr[...].astype(BF16), k2) + b_ref[...]
        m_old = m_scr[...]
        m_new = jnp.maximum(m_old, jnp.max(s, axis=1, keepdims=True))
        alpha = jnp.exp(m_old - m_new)
        p = jnp.exp(s - m_new)
        l_scr[...] = alpha * l_scr[...] + jnp.sum(p, axis=1, keepdims=True)
        acc_scr[...] = alpha * acc_scr[...] + _dot(p.astype(BF16), v2)
        m_scr[...] = m_new

    @pl.when(j < n_pages)
    def _():
        attend(kc_ref, vc_ref, bias_ref)

    @pl.when(j == n_pages)
    def _():
        attend(kn_ref, vn_ref, biasn_ref)
        acc = acc_scr[...]
        l = l_scr[...]
        o = acc[:half] / l[:half] - lam_ref[...] * (acc[half:] / l[half:])
        ms = jnp.mean(o * o, axis=-1, keepdims=True)
        o = o * lax.rsqrt(ms + SUBLN_EPS) * sw_ref[...] * post_scale
        for h in range(heads):
            o_ref[:, h * LANES:(h + 1) * LANES] = o[h * tn:(h + 1) * tn]


def _sample_bias(heads, tn, causal):
    r = jnp.arange(2 * heads * tn)[:, None]
    c = jnp.arange(PAGE * heads)[None, :]
    ok = (c % heads) == ((r % (heads * tn)) // tn)
    if causal:
        ok = ok & ((c // heads) <= (r % tn))
    return jnp.where(ok, 0.0, NEG_INF).astype(F32)


def _attn_sample(q, cache_k, cache_v, k_new, v_new, page_table, lam, subw, *, layer, post_scale):
    b, n_pages = page_table.shape
    heads = cache_k.shape[3]
    tn = q.shape[0] // b
    rows = 2 * heads * tn
    pt = page_table.reshape(-1)

    def page_map(i, j, pt_ref):
        return (layer, pt_ref[i * n_pages + jnp.minimum(j, n_pages - 1)], 0, 0, 0)

    page_spec = pl.BlockSpec((None, None, PAGE, heads, LANES), page_map)
    new_spec = pl.BlockSpec((None, PAGE, heads, LANES), lambda i, j, pt_ref: (i, 0, 0, 0))
    vec = pl.BlockSpec((1, LANES), lambda i, j, pt_ref: (0, 0))
    bias_spec = pl.BlockSpec((rows, PAGE * heads), lambda i, j, pt_ref: (0, 0))
    grid_spec = pltpu.PrefetchScalarGridSpec(
        num_scalar_prefetch=1,
        grid=(b, n_pages + 1),
        in_specs=[vec,
                  pl.BlockSpec((tn, heads * LANES), lambda i, j, pt_ref: (i, 0)),
                  page_spec, page_spec, new_spec, new_spec, bias_spec, bias_spec, vec],
        out_specs=pl.BlockSpec((tn, heads * LANES), lambda i, j, pt_ref: (i, 0)),
        scratch_shapes=[pltpu.VMEM((rows, LANES), F32),
                        pltpu.VMEM((rows, 1), F32),
                        pltpu.VMEM((rows, 1), F32),
                        pltpu.VMEM((rows, LANES), F32)],
    )
    return pl.pallas_call(
        functools.partial(_attn_sample_body, heads=heads, tn=tn, post_scale=post_scale),
        grid_spec=grid_spec,
        out_shape=jax.ShapeDtypeStruct((b * tn, heads * LANES), F32),
        compiler_params=_cparams(("parallel", "arbitrary")),
        name="diff_attn_sample",
    )(pt, lam, q, cache_k, cache_v, k_new, v_new,
      _sample_bias(heads, tn, False), _sample_bias(heads, tn, True), subw)


def _pick_tile(n, candidates):
    for c in candidates:
        if n % c == 0:
            return c
    raise ValueError(f"no tile for {n}")


def _pair_states(s):
    b, h, _, _ = s.shape
    st = jnp.swapaxes(s, -1, -2).reshape(b, h // 2, 2, HS_R, HS_R)
    z = jnp.zeros_like(st[:, :, 0])
    top = jnp.concatenate([st[:, :, 0], z], axis=-1)
    bot = jnp.concatenate([z, st[:, :, 1]], axis=-1)
    return jnp.concatenate([top, bot], axis=-2)


def _head_states(sp):
    b, hp, _, _ = sp.shape
    s0 = sp[:, :, :HS_R, :HS_R]
    s1 = sp[:, :, HS_R:, HS_R:]
    st = jnp.stack([s0, s1], axis=2).reshape(b, 2 * hp, HS_R, HS_R)
    return jnp.swapaxes(st, -1, -2)


def _layer(l, depth, xp, xs, cache_k, cache_v, state_wkv, state_shift, page_table,
           w_in, mu_shift, w_decay0, w_decay2, a0, a2, g2, k_k, k_a, r_k, lnx_w, lnx_b,
           lambda_q1, lambda_k1, lambda_q2, lambda_k2, subln_w,
           w_pa, w_pb, w_gate, b_gate, w_o, ln1_g, ln1_b,
           w_ffn_gate, w_ffn_up, w_ffn_down, ln2_g, ln2_b):
    bp, sp, d = xp.shape
    bs, ts, _ = xs.shape
    assert bp == 1
    heads = cache_k.shape[3]
    att_w = heads * LANES
    rw_w = w_decay0.shape[1]
    h_r = rw_w // HS_R
    rw_in = 3 * rw_w + W_LORA + A_LORA + G_LORA
    assert w_in.shape[2] == 3 * att_w + rw_in and att_w == rw_w
    alpha = (2 * depth) ** 0.25
    lam_init = 0.8 - 0.6 * math.exp(-0.3 * l)
    post_scale = 1.0 - lam_init

    lam = (jnp.exp(jnp.sum(lambda_q1[l] * lambda_k1[l])) - jnp.exp(jnp.sum(lambda_q2[l] * lambda_k2[l]))
           + lam_init)
    lam = jnp.broadcast_to(lam.astype(F32), (1, LANES))

    n_p = bp * sp
    n_s = bs * ts
    m = n_p + n_s
    x = jnp.concatenate([xp.reshape(n_p, d), xs.reshape(n_s, d)], axis=0)
    xb = x.astype(BF16)

    o_rw = 3 * att_w
    c_wl = o_rw + rw_w
    c_k = c_wl + W_LORA
    c_v = c_k + rw_w
    c_al = c_v + rw_w
    c_gl = c_al + A_LORA

    def permute_rw(t, axis):
        take = lambda a, b: lax.slice_in_dim(t, a, b, axis=axis)
        shape = list(t.shape)
        shape[axis] = LORA_PAD - W_LORA
        z = jnp.zeros(shape, t.dtype)
        o = 0 if t.shape[axis] == w_in.shape[2] else o_rw
        return jnp.concatenate([take(o_rw - o, c_wl - o), take(c_k - o, c_v - o), take(c_v - o, c_al - o),
                                take(c_wl - o, c_k - o), z, take(c_al - o, c_gl - o), z,
                                take(c_gl - o, c_gl - o + G_LORA)], axis=axis)

    w_in_l = w_in[l]
    w_all = jnp.concatenate([w_in_l[:, :o_rw], permute_rw(w_in_l, 1)], axis=1).astype(BF16)
    n_all = w_all.shape[1]
    rw_cols = n_all - o_rw

    tm = _pick_tile(m, (768, 512, 384, 256, 128, 64, 32, 16, 8))
    u = _matmul(xb, w_all, tm=tm, tn=_pick_tile(n_all, (512, 256, 128)), out_dtype=F32, name="in_proj")

    mu = permute_rw(mu_shift[l][None, :], 1)
    pad_rows = lambda t: jnp.concatenate([t, jnp.zeros((LORA_PAD - t.shape[0], t.shape[1]), t.dtype)], axis=0)
    w2 = pad_rows(w_decay2[l]).astype(BF16)
    a2b = pad_rows(a2[l]).astype(BF16)
    g2b = g2[l].astype(BF16)
    row = lambda t: t[l][None, :]
    prep_args = (mu, row(w_decay0), w2, row(a0), a2b, g2b, row(k_k), row(k_a))

    tp = _pick_tile(sp, (128, 64, 32, 16, 8))
    nblk_p = sp // tp
    prev_p = jnp.concatenate([jnp.zeros((1, rw_cols), F32),
                              lax.slice(u, (tp - 1, o_rw), (sp - 1, n_all), (tp, 1))], axis=0)
    prev_p = prev_p.reshape(nblk_p, 1, rw_cols)
    prep_p = _rwkv_prep(u, 0, sp, prev_p, *prep_args, tm=tp, width=rw_w, col0=o_rw)

    prev_s = permute_rw(state_shift[l], 1).reshape(bs, 1, rw_cols)
    prep_s = _rwkv_prep(u, n_p, n_s, prev_s, *prep_args, tm=ts, width=rw_w, col0=o_rw)

    wkv_params = (row(lnx_w), row(lnx_b), r_k[l].reshape(1, rw_w))
    s0_p = jnp.zeros((bp, h_r // 2, LANES, LANES), F32)
    pairs = _pick_tile(h_r // 2, (WKV_PAIRS, 2, 1))
    o_rw_p, st_p = _wkv(*prep_p, s0_p, *wkv_params, seqs=bp, chunk=WKV_CHUNK, pairs=pairs)

    t_pad = -(-ts // WKV_CHUNK) * WKV_CHUNK
    pad_t = lambda t: jnp.pad(t.reshape(bs, ts, rw_w), ((0, 0), (0, t_pad - ts), (0, 0))).reshape(bs * t_pad, rw_w)
    s0_s = _pair_states(state_wkv[l].astype(F32))
    o_rw_s, st_s = _wkv(*[pad_t(t) for t in prep_s], s0_s, *wkv_params, seqs=bs, chunk=WKV_CHUNK, pairs=pairs)
    o_rw_s = o_rw_s.reshape(bs, t_pad, rw_w)[:, :ts].reshape(n_s, rw_w)
    o_rwkv = jnp.concatenate([o_rw_p, o_rw_s], axis=0)

    new_wkv_p = _head_states(st_p).astype(state_wkv.dtype)
    new_wkv_s = _head_states(st_s).astype(state_wkv.dtype)

    def shift_row(rows):
        t = u[rows][:, o_rw:]
        w_ = rw_w
        return jnp.concatenate([t[:, :w_], t[:, 3 * w_:3 * w_ + W_LORA], t[:, w_:3 * w_],
                                t[:, 3 * w_ + LORA_PAD:3 * w_ + LORA_PAD + A_LORA],
                                t[:, 3 * w_ + 2 * LORA_PAD:]], axis=1)

    new_shift_p = shift_row(jnp.array([sp - 1]))
    new_shift_s = shift_row(n_p + ts - 1 + ts * jnp.arange(bs))

    subw = subln_w[l][None, :]
    tq = _pick_tile(sp, (512, 256, 128))
    vt = lax.slice(u, (0, 2 * att_w), (n_p, 3 * att_w)).T.astype(BF16)
    o_att_p = _attn_prompt(u, vt, lam, subw, t=sp, heads=heads, tq=tq, post_scale=post_scale)

    q_s = lax.slice(u, (n_p, 0), (m, att_w))
    k_s = lax.slice(u, (n_p, att_w), (m, 2 * att_w)).reshape(bs, ts, heads, DV)
    v_s = lax.slice(u, (n_p, 2 * att_w), (m, 3 * att_w)).reshape(bs, ts, heads, DV)
    pad_new = lambda t: jnp.pad(t, ((0, 0), (0, PAGE - ts), (0, 0), (0, 0)))
    o_att_s = _attn_sample(q_s, cache_k, cache_v, pad_new(k_s), pad_new(v_s), page_table, lam, subw,
                           layer=l, post_scale=post_scale)
    o_att = jnp.concatenate([o_att_p, o_att_s.astype(BF16)], axis=0)

    wg = w_gate[l].astype(BF16)
    bg = b_gate[l][None, :]
    mixin = _gated_merge(xb, o_att, o_rwkv, wg[:, :d], wg[:, d:], bg[:, :d], bg[:, d:],
                         w_pa[l].astype(BF16), w_pb[l].astype(BF16), tm=tm, tn=_pick_tile(d, (256, 128)))
    pre1 = _matmul_resid(mixin, w_o[l].astype(BF16), x, alpha=alpha, tm=tm, tn=_pick_tile(d, (512, 256, 128)),
                         name="out_proj")
    tl = _pick_tile(m, (256, 128, 64, 32, 16, 8))
    x1, x1b = _layer_norm(pre1, row(ln1_g), row(ln1_b), tm=tl)

    d_ff = w_ffn_gate.shape[2]
    hmid = _swiglu(x1b, w_ffn_gate[l].astype(BF16), w_ffn_up[l].astype(BF16), tm=tm,
                   tn=_pick_tile(d_ff, (256, 128)))
    tmd = _pick_tile(m, (384, 256, 128, 64, 32, 16, 8))
    pre2 = _matmul_resid(hmid, w_ffn_down[l].astype(BF16), x1, alpha=alpha, tm=tmd,
                         tn=_pick_tile(d, (256, 128)), name="ffn_down")
    y, _ = _layer_norm(pre2, row(ln2_g), row(ln2_b), tm=tl)

    k_all = lax.slice(u, (0, att_w), (m, 2 * att_w))
    v_all = lax.slice(u, (0, 2 * att_w), (m, 3 * att_w))
    outs = dict(
        xp=y[:n_p].reshape(bp, sp, d), xs=y[n_p:].reshape(bs, ts, d),
        kp=k_all[:n_p].reshape(bp, sp, heads, DV), vp=v_all[:n_p].reshape(bp, sp, heads, DV),
        wp=new_wkv_p, sp=new_shift_p,
        ks=k_s, vs=v_s, ws=new_wkv_s, ss=new_shift_s)
    return outs


def kernel(x_prompt, x_sample, cache_k, cache_v, state_wkv, state_shift, page_table, w_in, mu_shift, w_decay0, w_decay2, a0, a2, g2, k_k, k_a, r_k, lnx_w, lnx_b, lambda_q1, lambda_k1, lambda_q2, lambda_k2, subln_w, w_pa, w_pb, w_gate, b_gate, w_o, ln1_g, ln1_b, w_ffn_gate, w_ffn_up, w_ffn_down, ln2_g, ln2_b):
    depth = w_in.shape[0]
    xp, xs = x_prompt, x_sample
    acc = {n: [] for n in ("kp", "vp", "wp", "sp", "ks", "vs", "ws", "ss")}
    for l in range(depth):
        o = _layer(l, depth, xp, xs, cache_k, cache_v, state_wkv, state_shift, page_table,
                   w_in, mu_shift, w_decay0, w_decay2, a0, a2, g2, k_k, k_a, r_k, lnx_w, lnx_b,
                   lambda_q1, lambda_k1, lambda_q2, lambda_k2, subln_w,
                   w_pa, w_pb, w_gate, b_gate, w_o, ln1_g, ln1_b,
                   w_ffn_gate, w_ffn_up, w_ffn_down, ln2_g, ln2_b)
        xp, xs = o["xp"], o["xs"]
        for n in acc:
            acc[n].append(o[n])
    st = lambda n: jnp.stack(acc[n])
    return (xp, xs, st("kp"), st("vp"), st("wp"), st("sp"), st("ks"), st("vs"), st("ws"), st("ss"))
```

```python
import functools
import math

import jax
import jax.numpy as jnp
from jax import lax
from jax.experimental import pallas as pl
from jax.experimental.pallas import tpu as pltpu

F32 = jnp.float32
BF16 = jnp.bfloat16

LANES = 128
VMEM_LIMIT = 56 * 1024 * 1024

DK = 64
DV = 128
HS_R = 64
W_LORA = 96
A_LORA = 96
G_LORA = 256
LORA_PAD = 128
ATTN_SCALE = DK ** -0.5
LOG2E = math.log2(math.e)
ATTN_KEY_SUB = 256
ATTN_Q_COLS = 128
SAMPLE_PAGES = 4
SUBLN_EPS = 1e-5
NEG_INF = -1e30
GN_EPS = 64e-5
LN_EPS = 1e-5
PAGE = 128
WKV_CHUNK = 64
WKV_PAIRS = 8


def _cparams(sem):
    return pltpu.CompilerParams(dimension_semantics=sem, vmem_limit_bytes=VMEM_LIMIT)


def _dot(a, b):
    return jnp.dot(a, b, preferred_element_type=F32)


def _dot_nt(a, b):
    return lax.dot_general(a, b, (((1,), (1,)), ((), ())), preferred_element_type=F32)


def _dot_tn(a, b):
    return lax.dot_general(a, b, (((0,), (0,)), ((), ())), preferred_element_type=F32)


def _sigmoid(x):
    return 1.0 / (1.0 + jnp.exp(-x))


def _mm_body(x_ref, w_ref, o_ref):
    o_ref[...] = _dot(x_ref[...], w_ref[...]).astype(o_ref.dtype)


def _matmul(x, w, *, tm, tn, out_dtype, name):
    m, k = x.shape
    n = w.shape[1]
    return pl.pallas_call(
        _mm_body,
        grid=(m // tm, n // tn),
        in_specs=[pl.BlockSpec((tm, k), lambda i, j: (i, 0)),
                  pl.BlockSpec((k, tn), lambda i, j: (0, j))],
        out_specs=pl.BlockSpec((tm, tn), lambda i, j: (i, j)),
        out_shape=jax.ShapeDtypeStruct((m, n), out_dtype),
        compiler_params=_cparams(("parallel", "parallel")),
        name=name,
    )(x, w)


def _mm_resid_body(x_ref, w_ref, r_ref, o_ref, *, alpha):
    o_ref[...] = alpha * r_ref[...] + _dot(x_ref[...], w_ref[...])


def _matmul_resid(x, w, resid, *, alpha, tm, tn, name):
    m, k = x.shape
    n = w.shape[1]
    return pl.pallas_call(
        functools.partial(_mm_resid_body, alpha=alpha),
        grid=(m // tm, n // tn),
        in_specs=[pl.BlockSpec((tm, k), lambda i, j: (i, 0)),
                  pl.BlockSpec((k, tn), lambda i, j: (0, j)),
                  pl.BlockSpec((tm, tn), lambda i, j: (i, j))],
        out_specs=pl.BlockSpec((tm, tn), lambda i, j: (i, j)),
        out_shape=jax.ShapeDtypeStruct((m, n), F32),
        compiler_params=_cparams(("parallel", "parallel")),
        name=name,
    )(x, w, resid)


def _gated_merge_body(x_ref, oa_ref, ob_ref, wga_ref, wgb_ref, ba_ref, bb_ref, wpa_ref, wpb_ref, o_ref):
    x = x_ref[...]
    ga = _sigmoid(_dot(x, wga_ref[...]) + ba_ref[...])
    gb = _sigmoid(_dot(x, wgb_ref[...]) + bb_ref[...])
    o = ga * _dot(oa_ref[...], wpa_ref[...]) + gb * _dot(ob_ref[...], wpb_ref[...])
    o_ref[...] = o.astype(o_ref.dtype)


def _gated_merge(x, oa, ob, wga, wgb, ba, bb, wpa, wpb, *, tm, tn):
    m, d = x.shape
    ka = oa.shape[1]
    kb = ob.shape[1]
    n = wga.shape[1]
    row = lambda w: pl.BlockSpec((tm, w), lambda i, j: (i, 0))
    col = lambda k: pl.BlockSpec((k, tn), lambda i, j: (0, j))
    return pl.pallas_call(
        _gated_merge_body,
        grid=(m // tm, n // tn),
        in_specs=[row(d), row(ka), row(kb), col(d), col(d), col(1), col(1), col(ka), col(kb)],
        out_specs=pl.BlockSpec((tm, tn), lambda i, j: (i, j)),
        out_shape=jax.ShapeDtypeStruct((m, n), BF16),
        compiler_params=_cparams(("parallel", "parallel")),
        name="gated_merge",
    )(x, oa, ob, wga, wgb, ba, bb, wpa, wpb)


def _swiglu_body(x_ref, wg_ref, wu_ref, o_ref):
    x = x_ref[...]
    g = _dot(x, wg_ref[...])
    u = _dot(x, wu_ref[...])
    o_ref[...] = (g * _sigmoid(g) * u).astype(o_ref.dtype)


def _swiglu(x, wg, wu, *, tm, tn):
    m, k = x.shape
    n = wg.shape[1]
    return pl.pallas_call(
        _swiglu_body,
        grid=(m // tm, n // tn),
        in_specs=[pl.BlockSpec((tm, k), lambda i, j: (i, 0)),
                  pl.BlockSpec((k, tn), lambda i, j: (0, j)),
                  pl.BlockSpec((k, tn), lambda i, j: (0, j))],
        out_specs=pl.BlockSpec((tm, tn), lambda i, j: (i, j)),
        out_shape=jax.ShapeDtypeStruct((m, n), BF16),
        compiler_params=_cparams(("parallel", "parallel")),
        name="swiglu",
    )(x, wg, wu)


def _layer_norm_body(x_ref, g_ref, b_ref, o_ref, ob_ref):
    x = x_ref[...]
    mu = jnp.mean(x, axis=-1, keepdims=True)
    d = x - mu
    var = jnp.mean(d * d, axis=-1, keepdims=True)
    y = d * lax.rsqrt(var + LN_EPS) * g_ref[...] + b_ref[...]
    o_ref[...] = y
    ob_ref[...] = y.astype(BF16)


def _layer_norm(x, g, b, *, tm):
    m, d = x.shape
    return pl.pallas_call(
        _layer_norm_body,
        grid=(m // tm,),
        in_specs=[pl.BlockSpec((tm, d), lambda i: (i, 0)),
                  pl.BlockSpec((1, d), lambda i: (0, 0)),
                  pl.BlockSpec((1, d), lambda i: (0, 0))],
        out_specs=[pl.BlockSpec((tm, d), lambda i: (i, 0)),
                   pl.BlockSpec((tm, d), lambda i: (i, 0))],
        out_shape=[jax.ShapeDtypeStruct((m, d), F32), jax.ShapeDtypeStruct((m, d), BF16)],
        compiler_params=_cparams(("parallel",)),
        name="layer_norm",
    )(x, g, b)


def _rwkv_prep_body(ur_ref, uk_ref, uv_ref, ul_ref, prev_ref, mu_ref, w0_ref, w2_ref, a0_ref, a2_ref,
                    g2_ref, kk_ref, ka_ref,
                    r_out, lw_out, k_out, v_out, kk_out, a_out, g_out, *, tm, width):
    first = lax.broadcasted_iota(jnp.int32, (tm, 1), 0) == 0

    def mixed(u_ref, lo, hi):
        u = u_ref[...]
        prev = jnp.broadcast_to(prev_ref[:, lo:hi], (tm, hi - lo))
        shifted = jnp.where(first, prev, pltpu.roll(u, shift=1, axis=0))
        return u + (shifted - u) * mu_ref[:, lo:hi]

    w = width
    r = mixed(ur_ref, 0, w)
    k = mixed(uk_ref, w, 2 * w)
    v = mixed(uv_ref, 2 * w, 3 * w)
    lora = mixed(ul_ref, 3 * w, 3 * w + 2 * LORA_PAD + G_LORA)
    w_l = lora[:, 0:LORA_PAD]
    a_l = lora[:, LORA_PAD:2 * LORA_PAD]
    g_l = lora[:, 2 * LORA_PAD:]

    z = w0_ref[...] + _dot(jnp.tanh(w_l).astype(BF16), w2_ref[...])
    nz = -z
    softplus = jnp.maximum(nz, 0.0) + jnp.log(1.0 + jnp.exp(-jnp.abs(nz)))
    wdec = -softplus - 0.5
    a = _sigmoid(a0_ref[...] + _dot(a_l.astype(BF16), a2_ref[...]))
    g = _dot(_sigmoid(g_l).astype(BF16), g2_ref[...])

    r_out[...] = r
    lw_out[...] = -jnp.exp(wdec)
    k_out[...] = k * (1.0 + (a - 1.0) * ka_ref[...])
    v_out[...] = v
    kk_out[...] = k * kk_ref[...]
    a_out[...] = a
    g_out[...] = g


def _rwkv_prep(u, row0, rows, prev, mu, w0, w2, a0, a2, g2, k_k, k_a, *, tm, width, col0):
    nblk = rows // tm
    rb0 = row0 // tm
    cb = col0 // width
    lw = 2 * LORA_PAD + G_LORA
    lb = (col0 + 3 * width) // lw
    n_prev = prev.shape[1]
    rw_w = prev.shape[2]
    vec = lambda n: pl.BlockSpec((1, n), lambda i: (0, 0))
    out = jax.ShapeDtypeStruct((rows, width), F32)
    body = functools.partial(_rwkv_prep_body, tm=tm, width=width)
    return pl.pallas_call(
        body,
        grid=(nblk,),
        in_specs=[pl.BlockSpec((tm, width), lambda i: (rb0 + i, cb)),
                  pl.BlockSpec((tm, width), lambda i: (rb0 + i, cb + 1)),
                  pl.BlockSpec((tm, width), lambda i: (rb0 + i, cb + 2)),
                  pl.BlockSpec((tm, lw), lambda i: (rb0 + i, lb)),
                  pl.BlockSpec((None, n_prev, rw_w), lambda i: (i, 0, 0)),
                  vec(rw_w), vec(width),
                  pl.BlockSpec((LORA_PAD, width), lambda i: (0, 0)),
                  vec(width),
                  pl.BlockSpec((LORA_PAD, width), lambda i: (0, 0)),
                  pl.BlockSpec((G_LORA, width), lambda i: (0, 0)),
                  vec(width), vec(width)],
        out_specs=[pl.BlockSpec((tm, width), lambda i: (i, 0))] * 7,
        out_shape=[out] * 7,
        compiler_params=_cparams(("parallel",)),
        name="rwkv_prep",
    )(u, u, u, u, prev, mu, w0, w2, a0, a2, g2, k_k, k_a)


def _wkv_pair(r, lw, k, v, kk, a, g, st, lnw, lnb, rk, *, chunk):
    C = chunk
    C2 = 2 * C

    lane = lax.broadcasted_iota(jnp.int32, (C2, LANES), 1)
    rowi = lax.broadcasted_iota(jnp.int32, (C2, LANES), 0)
    own = (lane >= HS_R) == (rowi >= C)

    def twice(x):
        return jnp.concatenate([x, x], axis=0)

    def stack(x):
        return jnp.where(own, twice(x), 0.0)

    tr = lax.broadcasted_iota(jnp.int32, (C, C), 0)
    tc = lax.broadcasted_iota(jnp.int32, (C, C), 1)
    tri = (tr >= tc).astype(BF16)
    lw_hi = lw.astype(BF16)
    rem = lw - lw_hi.astype(F32)
    lw_mid = rem.astype(BF16)
    lw_lo = (rem - lw_mid.astype(F32)).astype(BF16)
    G = _dot(tri, lw_hi) + _dot(tri, lw_mid) + _dot(tri, lw_lo)
    yield None
    GC = G[C - 1:C, :]
    e_prev = jnp.exp(G - lw)
    e_neg = jnp.exp(-G)
    e_pos = jnp.exp(G)
    e_rest = jnp.exp(GC - G)
    e_all = jnp.exp(GC)

    kk2 = stack(kk)
    nrm = jnp.sqrt(jnp.sum(kk2 * kk2, axis=1, keepdims=True))
    kk2 = kk2 / jnp.maximum(nrm, 1e-12)

    A2 = (kk2 * twice(-e_prev)).astype(BF16)
    B2 = (kk2 * twice(a * e_neg)).astype(BF16)
    K2 = stack(k * e_neg).astype(BF16)
    R2 = stack(r * e_pos).astype(BF16)
    Bt2 = (kk2 * twice(a * e_rest)).astype(BF16)
    Kt2 = stack(k * e_rest).astype(BF16)
    V2f = stack(v)
    V2 = V2f.astype(BF16)

    mr = lax.broadcasted_iota(jnp.int32, (C2, C2), 0)
    mc = lax.broadcasted_iota(jnp.int32, (C2, C2), 1)
    strict = mr > mc
    incl = mr >= mc
    eye = (mr == mc).astype(F32)

    m_ab = jnp.where(strict, _dot_nt(A2, B2), 0.0)
    m_ak = jnp.where(strict, _dot_nt(A2, K2), 0.0).astype(BF16)
    m_rb = jnp.where(incl, _dot_nt(R2, B2), 0.0).astype(BF16)
    m_rk = jnp.where(incl, _dot_nt(R2, K2), 0.0).astype(BF16)
    yield None

    t_inv = eye + m_ab
    p = m_ab
    n = 1
    while 2 * n < C:
        pb = p.astype(BF16)
        p = _dot(pb, pb)
        t_inv = _dot((eye + p).astype(BF16), t_inv.astype(BF16))
        n *= 2
        yield None

    stb = st.astype(BF16)
    x0 = _dot(A2, stb) + _dot(m_ak, V2)
    yield None
    u2 = _dot(t_inv.astype(BF16), x0.astype(BF16))
    yield None
    u2b = u2.astype(BF16)
    y2 = _dot(R2, stb) + _dot(m_rb, u2b) + _dot(m_rk, V2)

    decay_col = jnp.broadcast_to(e_all, (LANES, LANES)).T
    st_new = decay_col * st + _dot_tn(Bt2, u2b) + _dot_tn(Kt2, V2)
    yield None

    inv_hs = 1.0 / HS_R
    mean = jnp.sum(y2, axis=1, keepdims=True) * inv_hs
    d = jnp.where(own, y2 - mean, 0.0)
    var = jnp.sum(d * d, axis=1, keepdims=True) * inv_hs
    yn2 = d * lax.rsqrt(var + GN_EPS)
    bonus2 = jnp.sum(stack(r * k * rk), axis=1, keepdims=True) * V2f
    yn = yn2[:C] + yn2[C:]
    bonus = bonus2[:C] + bonus2[C:]
    yield (yn * lnw + lnb + bonus) * g, st_new


def _wkv_body(r_ref, lw_ref, k_ref, v_ref, kk_ref, a_ref, g_ref, s0_ref, lnw_ref, lnb_ref, rk_ref,
              o_ref, st_ref, st_scr, *, chunk, pairs):
    c = pl.program_id(2)

    @pl.when(c == 0)
    def _():
        st_scr[...] = s0_ref[...]

    lanes = [slice(pp * LANES, (pp + 1) * LANES) for pp in range(pairs)]
    gens = [_wkv_pair(r_ref[:, sl], lw_ref[:, sl], k_ref[:, sl], v_ref[:, sl], kk_ref[:, sl],
                      a_ref[:, sl], g_ref[:, sl], st_scr[pp], lnw_ref[:, sl], lnb_ref[:, sl],
                      rk_ref[:, sl], chunk=chunk) for pp, sl in enumerate(lanes)]
    results = [None] * pairs
    while any(res is None for res in results):
        results = [next(gen) for gen in gens]
    for pp, (o, st_new) in enumerate(results):
        st_scr[pp] = st_new
        o_ref[:, lanes[pp]] = o.astype(o_ref.dtype)

    @pl.when(c == pl.num_programs(2) - 1)
    def _():
        st_ref[...] = st_scr[...]


def _wkv(r, lw, k, v, kk, a, g, s0, lnw, lnb, rk, *, seqs, chunk, pairs):
    rows, width = r.shape
    t = rows // seqs
    nchunk = t // chunk
    npair = width // LANES
    wblk = pairs * LANES
    blk = pl.BlockSpec((chunk, wblk), lambda b, p, c: (b * nchunk + c, p))
    vec = pl.BlockSpec((1, wblk), lambda b, p, c: (0, p))
    stspec = pl.BlockSpec((None, pairs, LANES, LANES), lambda b, p, c: (b, p, 0, 0))
    return pl.pallas_call(
        functools.partial(_wkv_body, chunk=chunk, pairs=pairs),
        grid=(seqs, npair // pairs, nchunk),
        in_specs=[blk] * 7 + [stspec, vec, vec, vec],
        out_specs=[blk, stspec],
        out_shape=[jax.ShapeDtypeStruct((rows, width), BF16),
                   jax.ShapeDtypeStruct((seqs, npair, LANES, LANES), F32)],
        scratch_shapes=[pltpu.VMEM((pairs, LANES, LANES), F32)],
        compiler_params=_cparams(("parallel", "parallel", "arbitrary")),
        name="wkv7_chunked",
    )(r, lw, k, v, kk, a, g, s0, lnw, lnb, rk)


def _two_maps(q):
    lane = lax.broadcasted_iota(jnp.int32, q.shape, 1)
    q = q * (ATTN_SCALE * LOG2E)
    return jnp.where(lane < DK, q, 0.0), jnp.where(lane >= DK, q, 0.0)


def _attn_prompt_body(qi_ref, ki_ref, lam_ref, q_ref, k_ref, vt_ref, sw_ref, o_ref,
                      q2_scr, m_scr, l_scr, acc_scr, *, tq, post_scale):
    step = pl.program_id(1)
    qi = qi_ref[step]
    ki = ki_ref[step]

    @pl.when(ki == 0)
    def _():
        q1, q2 = _two_maps(q_ref[...])
        q2_scr[...] = jnp.concatenate([q1, q2], axis=0).astype(BF16)
        m_scr[...] = jnp.full_like(m_scr, NEG_INF)
        l_scr[...] = jnp.zeros_like(l_scr)
        acc_scr[...] = jnp.zeros_like(acc_scr)

    def update(diagonal):
        for kb in range(tq // ATTN_KEY_SUB):
            k0 = kb * ATTN_KEY_SUB
            kblk = k_ref[k0:k0 + ATTN_KEY_SUB, :].astype(BF16)
            vtblk = vt_ref[:, k0:k0 + ATTN_KEY_SUB]
            for c in range(2 * tq // ATTN_Q_COLS):
                q0 = (c * ATTN_Q_COLS) % tq
                if diagonal and k0 > q0 + ATTN_Q_COLS - 1:
                    continue
                cs = slice(c * ATTN_Q_COLS, (c + 1) * ATTN_Q_COLS)
                st = _dot_nt(kblk, q2_scr[cs, :])
                if diagonal and k0 + ATTN_KEY_SUB - 1 > q0:
                    krow = k0 + lax.broadcasted_iota(jnp.int32, st.shape, 0)
                    qcol = q0 + lax.broadcasted_iota(jnp.int32, st.shape, 1)
                    st = jnp.where(krow <= qcol, st, NEG_INF)
                m_old = m_scr[:, cs]
                m_new = jnp.maximum(m_old, jnp.max(st, axis=0, keepdims=True))
                alpha = jnp.exp2(m_old - m_new)
                p = jnp.exp2(st - m_new)
                l_scr[:, cs] = alpha * l_scr[:, cs] + jnp.sum(p, axis=0, keepdims=True)
                acc_scr[:, cs] = alpha * acc_scr[:, cs] + _dot(vtblk, p.astype(BF16))
                m_scr[:, cs] = m_new

    @pl.when(ki < qi)
    def _():
        update(False)

    @pl.when(ki == qi)
    def _():
        update(True)
        acc = acc_scr[...]
        l = l_scr[...]
        ot = acc[:, :tq] / l[:, :tq] - lam_ref[:, 0:1] * (acc[:, tq:] / l[:, tq:])
        ms = jnp.mean(ot * ot, axis=0, keepdims=True)
        on = ot * lax.rsqrt(ms + SUBLN_EPS)
        o_ref[...] = (on.T * sw_ref[...] * post_scale).astype(o_ref.dtype)


def _attn_prompt(u, vt, lam, subw, *, t, heads, tq, post_scale):
    nq = t // tq
    pairs = [(i, j) for i in range(nq) for j in range(i + 1)]
    qi_tab = jnp.array([p[0] for p in pairs], jnp.int32)
    ki_tab = jnp.array([p[1] for p in pairs], jnp.int32)
    vec = pl.BlockSpec((1, LANES), lambda h, s, qi, ki: (0, 0))
    grid_spec = pltpu.PrefetchScalarGridSpec(
        num_scalar_prefetch=2,
        grid=(heads, len(pairs)),
        in_specs=[vec,
                  pl.BlockSpec((tq, LANES), lambda h, s, qi, ki: (qi[s], h)),
                  pl.BlockSpec((tq, LANES), lambda h, s, qi, ki: (ki[s], heads + h)),
                  pl.BlockSpec((LANES, tq), lambda h, s, qi, ki: (h, ki[s])),
                  vec],
        out_specs=pl.BlockSpec((tq, LANES), lambda h, s, qi, ki: (qi[s], h)),
        scratch_shapes=[pltpu.VMEM((2 * tq, LANES), BF16),
                        pltpu.VMEM((1, 2 * tq), F32),
                        pltpu.VMEM((1, 2 * tq), F32),
                        pltpu.VMEM((LANES, 2 * tq), F32)],
    )
    return pl.pallas_call(
        functools.partial(_attn_prompt_body, tq=tq, post_scale=post_scale),
        grid_spec=grid_spec,
        out_shape=jax.ShapeDtypeStruct((t, heads * LANES), BF16),
        compiler_params=_cparams(("parallel", "arbitrary")),
        name="diff_attn_prompt",
    )(qi_tab, ki_tab, lam, u, u, vt, subw)


def _attn_sample_body(pt_ref, lam_ref, q_ref, *refs, heads, tn, post_scale, pages):
    kc_refs = refs[:pages]
    vc_refs = refs[pages:2 * pages]
    kn_ref, vn_ref, bias_ref, biasn_ref, sw_ref, o_ref, q2_scr, m_scr, l_scr, acc_scr = refs[2 * pages:]
    j = pl.program_id(1)
    n_steps = pl.num_programs(1) - 1
    half = heads * tn

    @pl.when(j == 0)
    def _():
        for h in range(heads):
            q1, q2 = _two_maps(q_ref[:, h * LANES:(h + 1) * LANES])
            q2_scr[h * tn:(h + 1) * tn] = q1
            q2_scr[half + h * tn:half + (h + 1) * tn] = q2
        m_scr[...] = jnp.full_like(m_scr, NEG_INF)
        l_scr[...] = jnp.zeros_like(l_scr)
        acc_scr[...] = jnp.zeros_like(acc_scr)

    def attend(k_refs, v_refs, b_ref):
        q2 = q2_scr[...].astype(BF16)
        bias = b_ref[...]
        flat = lambda ref: ref[...].reshape(PAGE * heads, LANES).astype(BF16)
        ss = [_dot_nt(q2, flat(k_ref)) + bias for k_ref in k_refs]
        m_old = m_scr[...]
        m_new = m_old
        for s in ss:
            m_new = jnp.maximum(m_new, jnp.max(s, axis=1, keepdims=True))
        alpha = jnp.exp2(m_old - m_new)
        l_new = alpha * l_scr[...]
        acc_new = alpha * acc_scr[...]
        for s, v_ref in zip(ss, v_refs):
            p = jnp.exp2(s - m_new)
            l_new = l_new + jnp.sum(p, axis=1, keepdims=True)
            acc_new = acc_new + _dot(p.astype(BF16), flat(v_ref))
        l_scr[...] = l_new
        acc_scr[...] = acc_new
        m_scr[...] = m_new

    @pl.when(j < n_steps)
    def _():
        attend(kc_refs, vc_refs, bias_ref)

    @pl.when(j == n_steps)
    def _():
        attend([kn_ref], [vn_ref], biasn_ref)
        acc = acc_scr[...]
        l = l_scr[...]
        o = acc[:half] / l[:half] - lam_ref[...] * (acc[half:] / l[half:])
        ms = jnp.mean(o * o, axis=-1, keepdims=True)
        o = o * lax.rsqrt(ms + SUBLN_EPS) * sw_ref[...] * post_scale
        for h in range(heads):
            o_ref[:, h * LANES:(h + 1) * LANES] = o[h * tn:(h + 1) * tn]


def _sample_bias(heads, tn, causal):
    r = jnp.arange(2 * heads * tn)[:, None]
    c = jnp.arange(PAGE * heads)[None, :]
    ok = (c % heads) == ((r % (heads * tn)) // tn)
    if causal:
        ok = ok & ((c // heads) <= (r % tn))
    return jnp.where(ok, 0.0, NEG_INF).astype(F32)


def _attn_sample(q, cache_k, cache_v, k_new, v_new, page_table, lam, subw, *, layer, post_scale):
    b, n_pages = page_table.shape
    heads = cache_k.shape[3]
    tn = q.shape[0] // b
    rows = 2 * heads * tn
    pt = page_table.reshape(-1)

    pages = _pick_tile(n_pages, (SAMPLE_PAGES, 1))
    n_steps = n_pages // pages

    def page_spec(k):
        def page_map(i, j, pt_ref):
            return (layer, pt_ref[i * n_pages + jnp.minimum(j, n_steps - 1) * pages + k], 0, 0, 0)
        return pl.BlockSpec((None, None, PAGE, heads, LANES), page_map)

    page_specs = [page_spec(k) for k in range(pages)]
    new_spec = pl.BlockSpec((None, PAGE, heads, LANES), lambda i, j, pt_ref: (i, 0, 0, 0))
    vec = pl.BlockSpec((1, LANES), lambda i, j, pt_ref: (0, 0))
    bias_spec = pl.BlockSpec((rows, PAGE * heads), lambda i, j, pt_ref: (0, 0))
    grid_spec = pltpu.PrefetchScalarGridSpec(
        num_scalar_prefetch=1,
        grid=(b, n_steps + 1),
        in_specs=[vec,
                  pl.BlockSpec((tn, heads * LANES), lambda i, j, pt_ref: (i, 0)),
                  *page_specs, *page_specs, new_spec, new_spec, bias_spec, bias_spec, vec],
        out_specs=pl.BlockSpec((tn, heads * LANES), lambda i, j, pt_ref: (i, 0)),
        scratch_shapes=[pltpu.VMEM((rows, LANES), F32),
                        pltpu.VMEM((rows, 1), F32),
                        pltpu.VMEM((rows, 1), F32),
                        pltpu.VMEM((rows, LANES), F32)],
    )
    return pl.pallas_call(
        functools.partial(_attn_sample_body, heads=heads, tn=tn, post_scale=post_scale, pages=pages),
        grid_spec=grid_spec,
        out_shape=jax.ShapeDtypeStruct((b * tn, heads * LANES), F32),
        compiler_params=_cparams(("parallel", "arbitrary")),
        name="diff_attn_sample",
    )(pt, lam, q, *([cache_k] * pages), *([cache_v] * pages), k_new, v_new,
      _sample_bias(heads, tn, False), _sample_bias(heads, tn, True), subw)


def _pick_tile(n, candidates):
    for c in candidates:
        if n % c == 0:
            return c
    raise ValueError(f"no tile for {n}")


def _pair_states(s):
    b, h, _, _ = s.shape
    st = jnp.swapaxes(s, -1, -2).reshape(b, h // 2, 2, HS_R, HS_R)
    z = jnp.zeros_like(st[:, :, 0])
    top = jnp.concatenate([st[:, :, 0], z], axis=-1)
    bot = jnp.concatenate([z, st[:, :, 1]], axis=-1)
    return jnp.concatenate([top, bot], axis=-2)


def _head_states(sp):
    b, hp, _, _ = sp.shape
    s0 = sp[:, :, :HS_R, :HS_R]
    s1 = sp[:, :, HS_R:, HS_R:]
    st = jnp.stack([s0, s1], axis=2).reshape(b, 2 * hp, HS_R, HS_R)
    return jnp.swapaxes(st, -1, -2)


def _layer(l, depth, xp, xs, cache_k, cache_v, state_wkv, state_shift, page_table,
           w_in, mu_shift, w_decay0, w_decay2, a0, a2, g2, k_k, k_a, r_k, lnx_w, lnx_b,
           lambda_q1, lambda_k1, lambda_q2, lambda_k2, subln_w,
           w_pa, w_pb, w_gate, b_gate, w_o, ln1_g, ln1_b,
           w_ffn_gate, w_ffn_up, w_ffn_down, ln2_g, ln2_b):
    bp, sp, d = xp.shape
    bs, ts, _ = xs.shape
    assert bp == 1
    heads = cache_k.shape[3]
    att_w = heads * LANES
    rw_w = w_decay0.shape[1]
    h_r = rw_w // HS_R
    rw_in = 3 * rw_w + W_LORA + A_LORA + G_LORA
    assert w_in.shape[2] == 3 * att_w + rw_in and att_w == rw_w
    alpha = (2 * depth) ** 0.25
    lam_init = 0.8 - 0.6 * math.exp(-0.3 * l)
    post_scale = 1.0 - lam_init

    lam = (jnp.exp(jnp.sum(lambda_q1[l] * lambda_k1[l])) - jnp.exp(jnp.sum(lambda_q2[l] * lambda_k2[l]))
           + lam_init)
    lam = jnp.broadcast_to(lam.astype(F32), (1, LANES))

    n_p = bp * sp
    n_s = bs * ts
    m = n_p + n_s
    x = jnp.concatenate([xp.reshape(n_p, d), xs.reshape(n_s, d)], axis=0)
    xb = x.astype(BF16)

    o_rw = 3 * att_w
    c_wl = o_rw + rw_w
    c_k = c_wl + W_LORA
    c_v = c_k + rw_w
    c_al = c_v + rw_w
    c_gl = c_al + A_LORA

    def permute_rw(t, axis):
        take = lambda a, b: lax.slice_in_dim(t, a, b, axis=axis)
        shape = list(t.shape)
        shape[axis] = LORA_PAD - W_LORA
        z = jnp.zeros(shape, t.dtype)
        o = 0 if t.shape[axis] == w_in.shape[2] else o_rw
        return jnp.concatenate([take(o_rw - o, c_wl - o), take(c_k - o, c_v - o), take(c_v - o, c_al - o),
                                take(c_wl - o, c_k - o), z, take(c_al - o, c_gl - o), z,
                                take(c_gl - o, c_gl - o + G_LORA)], axis=axis)

    w_in_l = w_in[l]
    w_all = jnp.concatenate([w_in_l[:, :o_rw], permute_rw(w_in_l, 1)], axis=1).astype(BF16)
    n_all = w_all.shape[1]
    rw_cols = n_all - o_rw

    tm = _pick_tile(m, (768, 512, 384, 256, 128, 64, 32, 16, 8))
    u = _matmul(xb, w_all, tm=tm, tn=_pick_tile(n_all, (512, 256, 128)), out_dtype=F32, name="in_proj")

    mu = permute_rw(mu_shift[l][None, :], 1)
    pad_rows = lambda t: jnp.concatenate([t, jnp.zeros((LORA_PAD - t.shape[0], t.shape[1]), t.dtype)], axis=0)
    w2 = pad_rows(w_decay2[l]).astype(BF16)
    a2b = pad_rows(a2[l]).astype(BF16)
    g2b = g2[l].astype(BF16)
    row = lambda t: t[l][None, :]
    prep_args = (mu, row(w_decay0), w2, row(a0), a2b, g2b, row(k_k), row(k_a))

    tp = _pick_tile(sp, (128, 64, 32, 16, 8))
    nblk_p = sp // tp
    prev_p = jnp.concatenate([jnp.zeros((1, rw_cols), F32),
                              lax.slice(u, (tp - 1, o_rw), (sp - 1, n_all), (tp, 1))], axis=0)
    prev_p = prev_p.reshape(nblk_p, 1, rw_cols)
    prep_p = _rwkv_prep(u, 0, sp, prev_p, *prep_args, tm=tp, width=rw_w, col0=o_rw)

    prev_s = permute_rw(state_shift[l], 1).reshape(bs, 1, rw_cols)
    prep_s = _rwkv_prep(u, n_p, n_s, prev_s, *prep_args, tm=ts, width=rw_w, col0=o_rw)

    wkv_params = (row(lnx_w), row(lnx_b), r_k[l].reshape(1, rw_w))
    s0_p = jnp.zeros((bp, h_r // 2, LANES, LANES), F32)
    pairs = _pick_tile(h_r // 2, (WKV_PAIRS, 2, 1))
    o_rw_p, st_p = _wkv(*prep_p, s0_p, *wkv_params, seqs=bp, chunk=WKV_CHUNK, pairs=pairs)

    t_pad = -(-ts // WKV_CHUNK) * WKV_CHUNK
    pad_t = lambda t: jnp.pad(t.reshape(bs, ts, rw_w), ((0, 0), (0, t_pad - ts), (0, 0))).reshape(bs * t_pad, rw_w)
    s0_s = _pair_states(state_wkv[l].astype(F32))
    o_rw_s, st_s = _wkv(*[pad_t(t) for t in prep_s], s0_s, *wkv_params, seqs=bs, chunk=WKV_CHUNK, pairs=pairs)
    o_rw_s = o_rw_s.reshape(bs, t_pad, rw_w)[:, :ts].reshape(n_s, rw_w)
    o_rwkv = jnp.concatenate([o_rw_p, o_rw_s], axis=0)

    new_wkv_p = _head_states(st_p).astype(state_wkv.dtype)
    new_wkv_s = _head_states(st_s).astype(state_wkv.dtype)

    def shift_row(rows):
        t = u[rows][:, o_rw:]
        w_ = rw_w
        return jnp.concatenate([t[:, :w_], t[:, 3 * w_:3 * w_ + W_LORA], t[:, w_:3 * w_],
                                t[:, 3 * w_ + LORA_PAD:3 * w_ + LORA_PAD + A_LORA],
                                t[:, 3 * w_ + 2 * LORA_PAD:]], axis=1)

    new_shift_p = shift_row(jnp.array([sp - 1]))
    new_shift_s = shift_row(n_p + ts - 1 + ts * jnp.arange(bs))

    subw = subln_w[l][None, :]
    tq = _pick_tile(sp, (1024, 512, 256, 128))
    vt = lax.slice(u, (0, 2 * att_w), (n_p, 3 * att_w)).T.astype(BF16)
    o_att_p = _attn_prompt(u, vt, lam, subw, t=sp, heads=heads, tq=tq, post_scale=post_scale)

    q_s = lax.slice(u, (n_p, 0), (m, att_w))
    k_s = lax.slice(u, (n_p, att_w), (m, 2 * att_w)).reshape(bs, ts, heads, DV)
    v_s = lax.slice(u, (n_p, 2 * att_w), (m, 3 * att_w)).reshape(bs, ts, heads, DV)
    pad_new = lambda t: jnp.pad(t, ((0, 0), (0, PAGE - ts), (0, 0), (0, 0)))
    o_att_s = _attn_sample(q_s, cache_k, cache_v, pad_new(k_s), pad_new(v_s), page_table, lam, subw,
                           layer=l, post_scale=post_scale)
    o_att = jnp.concatenate([o_att_p, o_att_s.astype(BF16)], axis=0)

    wg = w_gate[l].astype(BF16)
    bg = b_gate[l][None, :]
    mixin = _gated_merge(xb, o_att, o_rwkv, wg[:, :d], wg[:, d:], bg[:, :d], bg[:, d:],
                         w_pa[l].astype(BF16), w_pb[l].astype(BF16), tm=tm, tn=_pick_tile(d, (256, 128)))
    pre1 = _matmul_resid(mixin, w_o[l].astype(BF16), x, alpha=alpha, tm=tm, tn=_pick_tile(d, (512, 256, 128)),
                         name="out_proj")
    tl = _pick_tile(m, (256, 128, 64, 32, 16, 8))
    x1, x1b = _layer_norm(pre1, row(ln1_g), row(ln1_b), tm=tl)

    d_ff = w_ffn_gate.shape[2]
    hmid = _swiglu(x1b, w_ffn_gate[l].astype(BF16), w_ffn_up[l].astype(BF16), tm=tm,
                   tn=_pick_tile(d_ff, (256, 128)))
    tmd = _pick_tile(m, (384, 256, 128, 64, 32, 16, 8))
    pre2 = _matmul_resid(hmid, w_ffn_down[l].astype(BF16), x1, alpha=alpha, tm=tmd,
                         tn=_pick_tile(d, (256, 128)), name="ffn_down")
    y, _ = _layer_norm(pre2, row(ln2_g), row(ln2_b), tm=tl)

    k_all = lax.slice(u, (0, att_w), (m, 2 * att_w))
    v_all = lax.slice(u, (0, 2 * att_w), (m, 3 * att_w))
    outs = dict(
        xp=y[:n_p].reshape(bp, sp, d), xs=y[n_p:].reshape(bs, ts, d),
        kp=k_all[:n_p].reshape(bp, sp, heads, DV), vp=v_all[:n_p].reshape(bp, sp, heads, DV),
        wp=new_wkv_p, sp=new_shift_p,
        ks=k_s, vs=v_s, ws=new_wkv_s, ss=new_shift_s)
    return outs


def kernel(x_prompt, x_sample, cache_k, cache_v, state_wkv, state_shift, page_table, w_in, mu_shift, w_decay0, w_decay2, a0, a2, g2, k_k, k_a, r_k, lnx_w, lnx_b, lambda_q1, lambda_k1, lambda_q2, lambda_k2, subln_w, w_pa, w_pb, w_gate, b_gate, w_o, ln1_g, ln1_b, w_ffn_gate, w_ffn_up, w_ffn_down, ln2_g, ln2_b):
    depth = w_in.shape[0]
    xp, xs = x_prompt, x_sample
    acc = {n: [] for n in ("kp", "vp", "wp", "sp", "ks", "vs", "ws", "ss")}
    for l in range(depth):
        o = _layer(l, depth, xp, xs, cache_k, cache_v, state_wkv, state_shift, page_table,
                   w_in, mu_shift, w_decay0, w_decay2, a0, a2, g2, k_k, k_a, r_k, lnx_w, lnx_b,
                   lambda_q1, lambda_k1, lambda_q2, lambda_k2, subln_w,
                   w_pa, w_pb, w_gate, b_gate, w_o, ln1_g, ln1_b,
                   w_ffn_gate, w_ffn_up, w_ffn_down, ln2_g, ln2_b)
        xp, xs = o["xp"], o["xs"]
        for n in acc:
            acc[n].append(o[n])
    st = lambda n: jnp.stack(acc[n])
    return (xp, xs, st("kp"), st("vp"), st("wp"), st("sp"), st("ks"), st("vs"), st("ws"), st("ss"))
```

```python
import functools
import math

import jax
import jax.numpy as jnp
from jax import lax
from jax.experimental import pallas as pl
from jax.experimental.pallas import tpu as pltpu

F32 = jnp.float32
BF16 = jnp.bfloat16

LANES = 128
VMEM_LIMIT = 56 * 1024 * 1024

DK = 64
DV = 128
HS_R = 64
W_LORA = 96
A_LORA = 96
G_LORA = 256
LORA_PAD = 128
ATTN_SCALE = DK ** -0.5
LOG2E = math.log2(math.e)
ATTN_KEY_SUB = 256
ATTN_Q_COLS = 128
SAMPLE_PAGES = 4
SUBLN_EPS = 1e-5
NEG_INF = -1e30
GN_EPS = 64e-5
LN_EPS = 1e-5
PAGE = 128
HEAD_TILE = 8
WKV_CHUNK = 64
WKV_PAIRS = 8


def _cparams(sem):
    return pltpu.CompilerParams(dimension_semantics=sem, vmem_limit_bytes=VMEM_LIMIT)


def _dot(a, b):
    return jnp.dot(a, b, preferred_element_type=F32)


def _dot_nt(a, b):
    return lax.dot_general(a, b, (((1,), (1,)), ((), ())), preferred_element_type=F32)


def _dot_tn(a, b):
    return lax.dot_general(a, b, (((0,), (0,)), ((), ())), preferred_element_type=F32)


def _sigmoid(x):
    return 1.0 / (1.0 + jnp.exp(-x))


def _mm_body(x_ref, w_ref, o_ref):
    o_ref[...] = _dot(x_ref[...], w_ref[...]).astype(o_ref.dtype)


def _matmul(x, w, *, tm, tn, out_dtype, name):
    m, k = x.shape
    n = w.shape[1]
    return pl.pallas_call(
        _mm_body,
        grid=(m // tm, n // tn),
        in_specs=[pl.BlockSpec((tm, k), lambda i, j: (i, 0)),
                  pl.BlockSpec((k, tn), lambda i, j: (0, j))],
        out_specs=pl.BlockSpec((tm, tn), lambda i, j: (i, j)),
        out_shape=jax.ShapeDtypeStruct((m, n), out_dtype),
        compiler_params=_cparams(("parallel", "parallel")),
        name=name,
    )(x, w)


def _mm_resid_body(x_ref, w_ref, r_ref, o_ref, *, alpha):
    o_ref[...] = alpha * r_ref[...] + _dot(x_ref[...], w_ref[...])


def _matmul_resid(x, w, resid, *, alpha, tm, tn, name):
    m, k = x.shape
    n = w.shape[1]
    return pl.pallas_call(
        functools.partial(_mm_resid_body, alpha=alpha),
        grid=(m // tm, n // tn),
        in_specs=[pl.BlockSpec((tm, k), lambda i, j: (i, 0)),
                  pl.BlockSpec((k, tn), lambda i, j: (0, j)),
                  pl.BlockSpec((tm, tn), lambda i, j: (i, j))],
        out_specs=pl.BlockSpec((tm, tn), lambda i, j: (i, j)),
        out_shape=jax.ShapeDtypeStruct((m, n), F32),
        compiler_params=_cparams(("parallel", "parallel")),
        name=name,
    )(x, w, resid)


def _gated_merge_body(x_ref, oa_ref, ob_ref, wga_ref, wgb_ref, ba_ref, bb_ref, wpa_ref, wpb_ref, o_ref):
    x = x_ref[...]
    ga = _sigmoid(_dot(x, wga_ref[...]) + ba_ref[...])
    gb = _sigmoid(_dot(x, wgb_ref[...]) + bb_ref[...])
    o = ga * _dot(oa_ref[...], wpa_ref[...]) + gb * _dot(ob_ref[...], wpb_ref[...])
    o_ref[...] = o.astype(o_ref.dtype)


def _gated_merge(x, oa, ob, wga, wgb, ba, bb, wpa, wpb, *, tm, tn):
    m, d = x.shape
    ka = oa.shape[1]
    kb = ob.shape[1]
    n = wga.shape[1]
    row = lambda w: pl.BlockSpec((tm, w), lambda i, j: (i, 0))
    col = lambda k: pl.BlockSpec((k, tn), lambda i, j: (0, j))
    return pl.pallas_call(
        _gated_merge_body,
        grid=(m // tm, n // tn),
        in_specs=[row(d), row(ka), row(kb), col(d), col(d), col(1), col(1), col(ka), col(kb)],
        out_specs=pl.BlockSpec((tm, tn), lambda i, j: (i, j)),
        out_shape=jax.ShapeDtypeStruct((m, n), BF16),
        compiler_params=_cparams(("parallel", "parallel")),
        name="gated_merge",
    )(x, oa, ob, wga, wgb, ba, bb, wpa, wpb)


def _swiglu_body(x_ref, wg_ref, wu_ref, o_ref):
    x = x_ref[...]
    g = _dot(x, wg_ref[...])
    u = _dot(x, wu_ref[...])
    o_ref[...] = (g * _sigmoid(g) * u).astype(o_ref.dtype)


def _swiglu(x, wg, wu, *, tm, tn):
    m, k = x.shape
    n = wg.shape[1]
    return pl.pallas_call(
        _swiglu_body,
        grid=(m // tm, n // tn),
        in_specs=[pl.BlockSpec((tm, k), lambda i, j: (i, 0)),
                  pl.BlockSpec((k, tn), lambda i, j: (0, j)),
                  pl.BlockSpec((k, tn), lambda i, j: (0, j))],
        out_specs=pl.BlockSpec((tm, tn), lambda i, j: (i, j)),
        out_shape=jax.ShapeDtypeStruct((m, n), BF16),
        compiler_params=_cparams(("parallel", "parallel")),
        name="swiglu",
    )(x, wg, wu)


def _layer_norm_body(x_ref, g_ref, b_ref, o_ref, ob_ref):
    x = x_ref[...]
    mu = jnp.mean(x, axis=-1, keepdims=True)
    d = x - mu
    var = jnp.mean(d * d, axis=-1, keepdims=True)
    y = d * lax.rsqrt(var + LN_EPS) * g_ref[...] + b_ref[...]
    o_ref[...] = y
    ob_ref[...] = y.astype(BF16)


def _layer_norm(x, g, b, *, tm):
    m, d = x.shape
    return pl.pallas_call(
        _layer_norm_body,
        grid=(m // tm,),
        in_specs=[pl.BlockSpec((tm, d), lambda i: (i, 0)),
                  pl.BlockSpec((1, d), lambda i: (0, 0)),
                  pl.BlockSpec((1, d), lambda i: (0, 0))],
        out_specs=[pl.BlockSpec((tm, d), lambda i: (i, 0)),
                   pl.BlockSpec((tm, d), lambda i: (i, 0))],
        out_shape=[jax.ShapeDtypeStruct((m, d), F32), jax.ShapeDtypeStruct((m, d), BF16)],
        compiler_params=_cparams(("parallel",)),
        name="layer_norm",
    )(x, g, b)


def _rwkv_prep_body(ur_ref, uk_ref, uv_ref, ul_ref, prev_ref, mu_ref, w0_ref, w2_ref, a0_ref, a2_ref,
                    g2_ref, kk_ref, ka_ref,
                    r_out, lw_out, k_out, v_out, kk_out, a_out, g_out, *, tm, width):
    first = lax.broadcasted_iota(jnp.int32, (tm, 1), 0) == 0

    def mixed(u_ref, lo, hi):
        u = u_ref[...]
        prev = jnp.broadcast_to(prev_ref[:, lo:hi], (tm, hi - lo))
        shifted = jnp.where(first, prev, pltpu.roll(u, shift=1, axis=0))
        return u + (shifted - u) * mu_ref[:, lo:hi]

    w = width
    r = mixed(ur_ref, 0, w)
    k = mixed(uk_ref, w, 2 * w)
    v = mixed(uv_ref, 2 * w, 3 * w)
    lora = mixed(ul_ref, 3 * w, 3 * w + 2 * LORA_PAD + G_LORA)
    w_l = lora[:, 0:LORA_PAD]
    a_l = lora[:, LORA_PAD:2 * LORA_PAD]
    g_l = lora[:, 2 * LORA_PAD:]

    z = w0_ref[...] + _dot(jnp.tanh(w_l).astype(BF16), w2_ref[...])
    nz = -z
    softplus = jnp.maximum(nz, 0.0) + jnp.log(1.0 + jnp.exp(-jnp.abs(nz)))
    wdec = -softplus - 0.5
    a = _sigmoid(a0_ref[...] + _dot(a_l.astype(BF16), a2_ref[...]))
    g = _dot(_sigmoid(g_l).astype(BF16), g2_ref[...])

    r_out[...] = r
    lw_out[...] = -jnp.exp(wdec)
    k_out[...] = k * (1.0 + (a - 1.0) * ka_ref[...])
    v_out[...] = v
    kk_out[...] = k * kk_ref[...]
    a_out[...] = a
    g_out[...] = g


def _rwkv_prep(u, row0, rows, prev, mu, w0, w2, a0, a2, g2, k_k, k_a, *, tm, width, col0):
    nblk = rows // tm
    rb0 = row0 // tm
    cb = col0 // width
    lw = 2 * LORA_PAD + G_LORA
    lb = (col0 + 3 * width) // lw
    n_prev = prev.shape[1]
    rw_w = prev.shape[2]
    vec = lambda n: pl.BlockSpec((1, n), lambda i: (0, 0))
    out = jax.ShapeDtypeStruct((rows, width), F32)
    body = functools.partial(_rwkv_prep_body, tm=tm, width=width)
    return pl.pallas_call(
        body,
        grid=(nblk,),
        in_specs=[pl.BlockSpec((tm, width), lambda i: (rb0 + i, cb)),
                  pl.BlockSpec((tm, width), lambda i: (rb0 + i, cb + 1)),
                  pl.BlockSpec((tm, width), lambda i: (rb0 + i, cb + 2)),
                  pl.BlockSpec((tm, lw), lambda i: (rb0 + i, lb)),
                  pl.BlockSpec((None, n_prev, rw_w), lambda i: (i, 0, 0)),
                  vec(rw_w), vec(width),
                  pl.BlockSpec((LORA_PAD, width), lambda i: (0, 0)),
                  vec(width),
                  pl.BlockSpec((LORA_PAD, width), lambda i: (0, 0)),
                  pl.BlockSpec((G_LORA, width), lambda i: (0, 0)),
                  vec(width), vec(width)],
        out_specs=[pl.BlockSpec((tm, width), lambda i: (i, 0))] * 7,
        out_shape=[out] * 7,
        compiler_params=_cparams(("parallel",)),
        name="rwkv_prep",
    )(u, u, u, u, prev, mu, w0, w2, a0, a2, g2, k_k, k_a)


def _wkv_pair(r, lw, k, v, kk, a, g, st, lnw, lnb, rk, *, chunk):
    C = chunk
    C2 = 2 * C

    lane = lax.broadcasted_iota(jnp.int32, (C2, LANES), 1)
    rowi = lax.broadcasted_iota(jnp.int32, (C2, LANES), 0)
    own = (lane >= HS_R) == (rowi >= C)

    def twice(x):
        return jnp.concatenate([x, x], axis=0)

    def stack(x):
        return jnp.where(own, twice(x), 0.0)

    tr = lax.broadcasted_iota(jnp.int32, (C, C), 0)
    tc = lax.broadcasted_iota(jnp.int32, (C, C), 1)
    tri = (tr >= tc).astype(BF16)
    lw_hi = lw.astype(BF16)
    rem = lw - lw_hi.astype(F32)
    lw_mid = rem.astype(BF16)
    lw_lo = (rem - lw_mid.astype(F32)).astype(BF16)
    G = _dot(tri, lw_hi) + _dot(tri, lw_mid) + _dot(tri, lw_lo)
    yield None
    GC = G[C - 1:C, :]
    e_prev = jnp.exp(G - lw)
    e_neg = jnp.exp(-G)
    e_pos = jnp.exp(G)
    e_rest = jnp.exp(GC - G)
    e_all = jnp.exp(GC)

    kk2 = stack(kk)
    nrm = jnp.sqrt(jnp.sum(kk2 * kk2, axis=1, keepdims=True))
    kk2 = kk2 / jnp.maximum(nrm, 1e-12)

    A2 = (kk2 * twice(-e_prev)).astype(BF16)
    B2 = (kk2 * twice(a * e_neg)).astype(BF16)
    K2 = stack(k * e_neg).astype(BF16)
    R2 = stack(r * e_pos).astype(BF16)
    Bt2 = (kk2 * twice(a * e_rest)).astype(BF16)
    Kt2 = stack(k * e_rest).astype(BF16)
    V2f = stack(v)
    V2 = V2f.astype(BF16)

    AR = jnp.concatenate([A2, R2], axis=0)
    mr = lax.broadcasted_iota(jnp.int32, (2 * C2, 2 * C2), 0)
    mc = lax.broadcasted_iota(jnp.int32, (2 * C2, 2 * C2), 1)
    tr_r = mr % C2
    tr_c = mc % C2
    keep = tr_r + jnp.where(mr < C2, 0, 1) > tr_c
    m_all = jnp.where(keep, _dot_nt(AR, jnp.concatenate([B2, K2], axis=0)), 0.0)
    m_ab = m_all[:C2, :C2]
    m_rb = m_all[C2:, :C2].astype(BF16)
    m_xk = m_all[:, C2:].astype(BF16)
    eye = (lax.broadcasted_iota(jnp.int32, (C2, C2), 0) == lax.broadcasted_iota(jnp.int32, (C2, C2), 1)).astype(F32)
    yield None

    t_inv = eye + m_ab
    pb = m_ab.astype(BF16)
    p = _dot(pb, pb)
    yield None
    n = 2
    while 2 * n < C:
        pb = p.astype(BF16)
        both = _dot(jnp.concatenate([pb, t_inv.astype(BF16)], axis=0), pb)
        p = both[:C2]
        t_inv = t_inv + both[C2:]
        n *= 2
        yield None
    t_inv = t_inv + _dot(t_inv.astype(BF16), p.astype(BF16))
    yield None

    stb = st.astype(BF16)
    xy = _dot(AR, stb) + _dot(m_xk, V2)
    yield None
    u2 = _dot(t_inv.astype(BF16), xy[:C2].astype(BF16))
    yield None
    u2b = u2.astype(BF16)
    y2 = xy[C2:] + _dot(m_rb, u2b)

    decay_col = jnp.broadcast_to(e_all, (LANES, LANES)).T
    st_new = decay_col * st + _dot_tn(jnp.concatenate([Bt2, Kt2], axis=0), jnp.concatenate([u2b, V2], axis=0))
    yield None

    inv_hs = 1.0 / HS_R
    mean = jnp.sum(y2, axis=1, keepdims=True) * inv_hs
    d = jnp.where(own, y2 - mean, 0.0)
    var = jnp.sum(d * d, axis=1, keepdims=True) * inv_hs
    yn2 = d * lax.rsqrt(var + GN_EPS)
    bonus2 = jnp.sum(stack(r * k * rk), axis=1, keepdims=True) * V2f
    yn = yn2[:C] + yn2[C:]
    bonus = bonus2[:C] + bonus2[C:]
    yield (yn * lnw + lnb + bonus) * g, st_new


def _wkv_body(r_ref, lw_ref, k_ref, v_ref, kk_ref, a_ref, g_ref, s0_ref, lnw_ref, lnb_ref, rk_ref,
              o_ref, st_ref, st_scr, *, chunk, pairs):
    c = pl.program_id(2)

    @pl.when(c == 0)
    def _():
        st_scr[...] = s0_ref[...]

    lanes = [slice(pp * LANES, (pp + 1) * LANES) for pp in range(pairs)]
    gens = [_wkv_pair(r_ref[:, sl], lw_ref[:, sl], k_ref[:, sl], v_ref[:, sl], kk_ref[:, sl],
                      a_ref[:, sl], g_ref[:, sl], st_scr[pp], lnw_ref[:, sl], lnb_ref[:, sl],
                      rk_ref[:, sl], chunk=chunk) for pp, sl in enumerate(lanes)]
    results = [None] * pairs
    while any(res is None for res in results):
        results = [next(gen) for gen in gens]
    for pp, (o, st_new) in enumerate(results):
        st_scr[pp] = st_new
        o_ref[:, lanes[pp]] = o.astype(o_ref.dtype)

    @pl.when(c == pl.num_programs(2) - 1)
    def _():
        st_ref[...] = st_scr[...]


def _wkv(r, lw, k, v, kk, a, g, s0, lnw, lnb, rk, *, seqs, chunk, pairs):
    rows, width = r.shape
    t = rows // seqs
    nchunk = t // chunk
    npair = width // LANES
    wblk = pairs * LANES
    blk = pl.BlockSpec((chunk, wblk), lambda b, p, c: (b * nchunk + c, p))
    vec = pl.BlockSpec((1, wblk), lambda b, p, c: (0, p))
    stspec = pl.BlockSpec((None, pairs, LANES, LANES), lambda b, p, c: (b, p, 0, 0))
    return pl.pallas_call(
        functools.partial(_wkv_body, chunk=chunk, pairs=pairs),
        grid=(seqs, npair // pairs, nchunk),
        in_specs=[blk] * 7 + [stspec, vec, vec, vec],
        out_specs=[blk, stspec],
        out_shape=[jax.ShapeDtypeStruct((rows, width), BF16),
                   jax.ShapeDtypeStruct((seqs, npair, LANES, LANES), F32)],
        scratch_shapes=[pltpu.VMEM((pairs, LANES, LANES), F32)],
        compiler_params=_cparams(("parallel", "parallel", "arbitrary")),
        name="wkv7_chunked",
    )(r, lw, k, v, kk, a, g, s0, lnw, lnb, rk)


def _two_maps(q):
    lane = lax.broadcasted_iota(jnp.int32, q.shape, 1)
    q = q * (ATTN_SCALE * LOG2E)
    return jnp.where(lane < DK, q, 0.0), jnp.where(lane >= DK, q, 0.0)


def _attn_prompt_body(qi_ref, ki_ref, lam_ref, q_ref, k_ref, v_ref, sw_ref, o_ref,
                      q2_scr, m_scr, l_scr, acc_scr, *, tq, post_scale):
    step = pl.program_id(1)
    qi = qi_ref[step]
    ki = ki_ref[step]

    @pl.when(ki == 0)
    def _():
        q1, q2 = _two_maps(q_ref[...])
        q2_scr[...] = jnp.concatenate([q1, q2], axis=0).astype(BF16)
        m_scr[...] = jnp.full_like(m_scr, NEG_INF)
        l_scr[...] = jnp.zeros_like(l_scr)
        acc_scr[...] = jnp.zeros_like(acc_scr)

    def update(diagonal):
        for kb in range(tq // ATTN_KEY_SUB):
            k0 = kb * ATTN_KEY_SUB
            kblk = k_ref[k0:k0 + ATTN_KEY_SUB, :].astype(BF16)
            vtblk = v_ref[k0:k0 + ATTN_KEY_SUB, :].T.astype(BF16)
            for c in range(2 * tq // ATTN_Q_COLS):
                q0 = (c * ATTN_Q_COLS) % tq
                if diagonal and k0 > q0 + ATTN_Q_COLS - 1:
                    continue
                cs = slice(c * ATTN_Q_COLS, (c + 1) * ATTN_Q_COLS)
                st = _dot_nt(kblk, q2_scr[cs, :])
                if diagonal and k0 + ATTN_KEY_SUB - 1 > q0:
                    krow = k0 + lax.broadcasted_iota(jnp.int32, st.shape, 0)
                    qcol = q0 + lax.broadcasted_iota(jnp.int32, st.shape, 1)
                    st = jnp.where(krow <= qcol, st, NEG_INF)
                m_old = m_scr[:, cs]
                m_new = jnp.maximum(m_old, jnp.max(st, axis=0, keepdims=True))
                alpha = jnp.exp2(m_old - m_new)
                p = jnp.exp2(st - m_new)
                l_scr[:, cs] = alpha * l_scr[:, cs] + jnp.sum(p, axis=0, keepdims=True)
                acc_scr[:, cs] = alpha * acc_scr[:, cs] + _dot(vtblk, p.astype(BF16))
                m_scr[:, cs] = m_new

    @pl.when(ki < qi)
    def _():
        update(False)

    @pl.when(ki == qi)
    def _():
        update(True)
        acc = acc_scr[...]
        l = l_scr[...]
        ot = acc[:, :tq] / l[:, :tq] - lam_ref[:, 0:1] * (acc[:, tq:] / l[:, tq:])
        ms = jnp.mean(ot * ot, axis=0, keepdims=True)
        on = ot * lax.rsqrt(ms + SUBLN_EPS)
        o_ref[...] = (on.T * sw_ref[...] * post_scale).astype(o_ref.dtype)


def _attn_prompt(u, lam, subw, *, t, heads, tq, post_scale):
    nq = t // tq
    pairs = [(i, j) for i in range(nq) for j in range(i + 1)]
    qi_tab = jnp.array([p[0] for p in pairs], jnp.int32)
    ki_tab = jnp.array([p[1] for p in pairs], jnp.int32)
    vec = pl.BlockSpec((1, LANES), lambda h, s, qi, ki: (0, 0))
    grid_spec = pltpu.PrefetchScalarGridSpec(
        num_scalar_prefetch=2,
        grid=(heads, len(pairs)),
        in_specs=[vec,
                  pl.BlockSpec((tq, LANES), lambda h, s, qi, ki: (qi[s], h)),
                  pl.BlockSpec((tq, LANES), lambda h, s, qi, ki: (ki[s], heads + h)),
                  pl.BlockSpec((tq, LANES), lambda h, s, qi, ki: (ki[s], 2 * heads + h)),
                  vec],
        out_specs=pl.BlockSpec((tq, LANES), lambda h, s, qi, ki: (qi[s], h)),
        scratch_shapes=[pltpu.VMEM((2 * tq, LANES), BF16),
                        pltpu.VMEM((1, 2 * tq), F32),
                        pltpu.VMEM((1, 2 * tq), F32),
                        pltpu.VMEM((LANES, 2 * tq), F32)],
    )
    return pl.pallas_call(
        functools.partial(_attn_prompt_body, tq=tq, post_scale=post_scale),
        grid_spec=grid_spec,
        out_shape=jax.ShapeDtypeStruct((t, heads * LANES), BF16),
        compiler_params=_cparams(("parallel", "arbitrary")),
        name="diff_attn_prompt",
    )(qi_tab, ki_tab, lam, u, u, u, subw)


def _attn_sample_body(pt_ref, lam_ref, q_ref, *refs, heads, tn, post_scale, pages):
    groups = heads // HEAD_TILE
    n_in = pages * groups
    kc_refs = [refs[k * groups:(k + 1) * groups] for k in range(pages)]
    vc_refs = [refs[n_in + k * groups:n_in + (k + 1) * groups] for k in range(pages)]
    kn_refs = refs[2 * n_in:2 * n_in + groups]
    vn_refs = refs[2 * n_in + groups:2 * n_in + 2 * groups]
    bias_ref, biasn_ref, sw_ref, o_ref, q2_scr, m_scr, l_scr, acc_scr = refs[2 * n_in + 2 * groups:]
    j = pl.program_id(1)
    n_steps = pl.num_programs(1) - 1
    half = HEAD_TILE * tn

    @pl.when(j == 0)
    def _():
        for h in range(heads):
            g, hh = divmod(h, HEAD_TILE)
            q1, q2 = _two_maps(q_ref[:, h * LANES:(h + 1) * LANES])
            q2_scr[g, hh * tn:(hh + 1) * tn] = q1
            q2_scr[g, half + hh * tn:half + (hh + 1) * tn] = q2
        m_scr[...] = jnp.full_like(m_scr, NEG_INF)
        l_scr[...] = jnp.zeros_like(l_scr)
        acc_scr[...] = jnp.zeros_like(acc_scr)

    def attend(k_pages, v_pages, b_ref):
        bias = b_ref[...]
        flat = lambda ref: ref[...].reshape(PAGE * HEAD_TILE, LANES).astype(BF16)
        for g in range(groups):
            q2 = q2_scr[g].astype(BF16)
            ss = [_dot_nt(q2, flat(kp[g])) + bias for kp in k_pages]
            m_old = m_scr[g]
            m_new = m_old
            for s in ss:
                m_new = jnp.maximum(m_new, jnp.max(s, axis=1, keepdims=True))
            alpha = jnp.exp2(m_old - m_new)
            l_new = alpha * l_scr[g]
            acc_new = alpha * acc_scr[g]
            for s, vp in zip(ss, v_pages):
                p = jnp.exp2(s - m_new)
                l_new = l_new + jnp.sum(p, axis=1, keepdims=True)
                acc_new = acc_new + _dot(p.astype(BF16), flat(vp[g]))
            l_scr[g] = l_new
            acc_scr[g] = acc_new
            m_scr[g] = m_new

    @pl.when(j < n_steps)
    def _():
        attend(kc_refs, vc_refs, bias_ref)

    @pl.when(j == n_steps)
    def _():
        attend([kn_refs], [vn_refs], biasn_ref)
        for g in range(groups):
            acc = acc_scr[g]
            l = l_scr[g]
            o = acc[:half] / l[:half] - lam_ref[...] * (acc[half:] / l[half:])
            ms = jnp.mean(o * o, axis=-1, keepdims=True)
            o = o * lax.rsqrt(ms + SUBLN_EPS) * sw_ref[...] * post_scale
            for hh in range(HEAD_TILE):
                h = g * HEAD_TILE + hh
                o_ref[:, h * LANES:(h + 1) * LANES] = o[hh * tn:(hh + 1) * tn]


def _sample_bias(tn, causal):
    r = jnp.arange(2 * HEAD_TILE * tn)[:, None]
    c = jnp.arange(PAGE * HEAD_TILE)[None, :]
    ok = (c % HEAD_TILE) == ((r % (HEAD_TILE * tn)) // tn)
    if causal:
        ok = ok & ((c // HEAD_TILE) <= (r % tn))
    return jnp.where(ok, 0.0, NEG_INF).astype(F32)


def _attn_sample(q, cache_k, cache_v, k_new, v_new, page_table, lam, subw, *, layer, post_scale):
    b, n_pages = page_table.shape
    n_layers, n_pool, _, heads, _ = cache_k.shape
    groups = heads // HEAD_TILE
    tn = q.shape[0] // b
    rows = 2 * HEAD_TILE * tn
    pt = page_table.reshape(-1)
    pages = _pick_tile(n_pages, (SAMPLE_PAGES, 1))
    n_steps = n_pages // pages
    tiled = lambda c: c.reshape(c.shape[:-2] + (groups, HEAD_TILE, LANES))

    def page_spec(k, g):
        def page_map(i, j, pt_ref):
            return (layer, pt_ref[i * n_pages + jnp.minimum(j, n_steps - 1) * pages + k], 0, g, 0, 0)
        return pl.BlockSpec((None, None, PAGE, None, HEAD_TILE, LANES), page_map)

    def new_spec(g):
        return pl.BlockSpec((None, PAGE, None, HEAD_TILE, LANES), lambda i, j, pt_ref: (i, 0, g, 0, 0))

    page_specs = [page_spec(k, g) for k in range(pages) for g in range(groups)]
    new_specs = [new_spec(g) for g in range(groups)]
    vec = pl.BlockSpec((1, LANES), lambda i, j, pt_ref: (0, 0))
    bias_spec = pl.BlockSpec((rows, PAGE * HEAD_TILE), lambda i, j, pt_ref: (0, 0))
    grid_spec = pltpu.PrefetchScalarGridSpec(
        num_scalar_prefetch=1,
        grid=(b, n_steps + 1),
        in_specs=[vec,
                  pl.BlockSpec((tn, heads * LANES), lambda i, j, pt_ref: (i, 0)),
                  *page_specs, *page_specs, *new_specs, *new_specs, bias_spec, bias_spec, vec],
        out_specs=pl.BlockSpec((tn, heads * LANES), lambda i, j, pt_ref: (i, 0)),
        scratch_shapes=[pltpu.VMEM((groups, rows, LANES), F32),
                        pltpu.VMEM((groups, rows, 1), F32),
                        pltpu.VMEM((groups, rows, 1), F32),
                        pltpu.VMEM((groups, rows, LANES), F32)],
    )
    n_in = pages * groups
    return pl.pallas_call(
        functools.partial(_attn_sample_body, heads=heads, tn=tn, post_scale=post_scale, pages=pages),
        grid_spec=grid_spec,
        out_shape=jax.ShapeDtypeStruct((b * tn, heads * LANES), F32),
        compiler_params=_cparams(("parallel", "arbitrary")),
        name="diff_attn_sample",
    )(pt, lam, q, *([tiled(cache_k)] * n_in), *([tiled(cache_v)] * n_in),
      *([tiled(k_new)] * groups), *([tiled(v_new)] * groups),
      _sample_bias(tn, False), _sample_bias(tn, True), subw)


def _pick_tile(n, candidates):
    for c in candidates:
        if n % c == 0:
            return c
    raise ValueError(f"no tile for {n}")


def _pair_states(s):
    b, h, _, _ = s.shape
    st = jnp.swapaxes(s, -1, -2).reshape(b, h // 2, 2, HS_R, HS_R)
    z = jnp.zeros_like(st[:, :, 0])
    top = jnp.concatenate([st[:, :, 0], z], axis=-1)
    bot = jnp.concatenate([z, st[:, :, 1]], axis=-1)
    return jnp.concatenate([top, bot], axis=-2)


def _head_states(sp):
    b, hp, _, _ = sp.shape
    s0 = sp[:, :, :HS_R, :HS_R]
    s1 = sp[:, :, HS_R:, HS_R:]
    st = jnp.stack([s0, s1], axis=2).reshape(b, 2 * hp, HS_R, HS_R)
    return jnp.swapaxes(st, -1, -2)


def _layer(l, depth, xp, xs, cache_k, cache_v, state_wkv, state_shift, page_table,
           w_in, mu_shift, w_decay0, w_decay2, a0, a2, g2, k_k, k_a, r_k, lnx_w, lnx_b,
           lambda_q1, lambda_k1, lambda_q2, lambda_k2, subln_w,
           w_pa, w_pb, w_gate, b_gate, w_o, ln1_g, ln1_b,
           w_ffn_gate, w_ffn_up, w_ffn_down, ln2_g, ln2_b):
    bp, sp, d = xp.shape
    bs, ts, _ = xs.shape
    assert bp == 1
    heads = cache_k.shape[3]
    att_w = heads * LANES
    rw_w = w_decay0.shape[1]
    h_r = rw_w // HS_R
    rw_in = 3 * rw_w + W_LORA + A_LORA + G_LORA
    assert w_in.shape[2] == 3 * att_w + rw_in and att_w == rw_w
    alpha = (2 * depth) ** 0.25
    lam_init = 0.8 - 0.6 * math.exp(-0.3 * l)
    post_scale = 1.0 - lam_init

    lam = (jnp.exp(jnp.sum(lambda_q1[l] * lambda_k1[l])) - jnp.exp(jnp.sum(lambda_q2[l] * lambda_k2[l]))
           + lam_init)
    lam = jnp.broadcast_to(lam.astype(F32), (1, LANES))

    n_p = bp * sp
    n_s = bs * ts
    m = n_p + n_s
    x = jnp.concatenate([xp.reshape(n_p, d), xs.reshape(n_s, d)], axis=0)
    xb = x.astype(BF16)

    o_rw = 3 * att_w
    c_wl = o_rw + rw_w
    c_k = c_wl + W_LORA
    c_v = c_k + rw_w
    c_al = c_v + rw_w
    c_gl = c_al + A_LORA

    def permute_rw(t, axis):
        take = lambda a, b: lax.slice_in_dim(t, a, b, axis=axis)
        shape = list(t.shape)
        shape[axis] = LORA_PAD - W_LORA
        z = jnp.zeros(shape, t.dtype)
        o = 0 if t.shape[axis] == w_in.shape[2] else o_rw
        return jnp.concatenate([take(o_rw - o, c_wl - o), take(c_k - o, c_v - o), take(c_v - o, c_al - o),
                                take(c_wl - o, c_k - o), z, take(c_al - o, c_gl - o), z,
                                take(c_gl - o, c_gl - o + G_LORA)], axis=axis)

    w_in_l = w_in[l]
    w_all = jnp.concatenate([w_in_l[:, :o_rw], permute_rw(w_in_l, 1)], axis=1).astype(BF16)
    n_all = w_all.shape[1]
    rw_cols = n_all - o_rw

    tm = _pick_tile(m, (768, 512, 384, 256, 128, 64, 32, 16, 8))
    u = _matmul(xb, w_all, tm=tm, tn=_pick_tile(n_all, (512, 256, 128)), out_dtype=F32, name="in_proj")

    mu = permute_rw(mu_shift[l][None, :], 1)
    pad_rows = lambda t: jnp.concatenate([t, jnp.zeros((LORA_PAD - t.shape[0], t.shape[1]), t.dtype)], axis=0)
    w2 = pad_rows(w_decay2[l]).astype(BF16)
    a2b = pad_rows(a2[l]).astype(BF16)
    g2b = g2[l].astype(BF16)
    row = lambda t: t[l][None, :]
    prep_args = (mu, row(w_decay0), w2, row(a0), a2b, g2b, row(k_k), row(k_a))

    tp = _pick_tile(sp, (128, 64, 32, 16, 8))
    nblk_p = sp // tp
    prev_p = jnp.concatenate([jnp.zeros((1, rw_cols), F32),
                              lax.slice(u, (tp - 1, o_rw), (sp - 1, n_all), (tp, 1))], axis=0)
    prev_p = prev_p.reshape(nblk_p, 1, rw_cols)
    prep_p = _rwkv_prep(u, 0, sp, prev_p, *prep_args, tm=tp, width=rw_w, col0=o_rw)

    prev_s = permute_rw(state_shift[l], 1).reshape(bs, 1, rw_cols)
    prep_s = _rwkv_prep(u, n_p, n_s, prev_s, *prep_args, tm=ts, width=rw_w, col0=o_rw)

    wkv_params = (row(lnx_w), row(lnx_b), r_k[l].reshape(1, rw_w))
    s0_p = jnp.zeros((bp, h_r // 2, LANES, LANES), F32)
    pairs = _pick_tile(h_r // 2, (WKV_PAIRS, 2, 1))
    o_rw_p, st_p = _wkv(*prep_p, s0_p, *wkv_params, seqs=bp, chunk=WKV_CHUNK, pairs=pairs)

    t_pad = -(-ts // WKV_CHUNK) * WKV_CHUNK
    pad_t = lambda t: jnp.pad(t.reshape(bs, ts, rw_w), ((0, 0), (0, t_pad - ts), (0, 0))).reshape(bs * t_pad, rw_w)
    s0_s = _pair_states(state_wkv[l].astype(F32))
    o_rw_s, st_s = _wkv(*[pad_t(t) for t in prep_s], s0_s, *wkv_params, seqs=bs, chunk=WKV_CHUNK, pairs=pairs)
    o_rw_s = o_rw_s.reshape(bs, t_pad, rw_w)[:, :ts].reshape(n_s, rw_w)
    o_rwkv = jnp.concatenate([o_rw_p, o_rw_s], axis=0)

    new_wkv_p = _head_states(st_p).astype(state_wkv.dtype)
    new_wkv_s = _head_states(st_s).astype(state_wkv.dtype)

    def shift_row(rows):
        t = u[rows][:, o_rw:]
        w_ = rw_w
        return jnp.concatenate([t[:, :w_], t[:, 3 * w_:3 * w_ + W_LORA], t[:, w_:3 * w_],
                                t[:, 3 * w_ + LORA_PAD:3 * w_ + LORA_PAD + A_LORA],
                                t[:, 3 * w_ + 2 * LORA_PAD:]], axis=1)

    new_shift_p = shift_row(jnp.array([sp - 1]))
    new_shift_s = shift_row(n_p + ts - 1 + ts * jnp.arange(bs))

    subw = subln_w[l][None, :]
    tq = _pick_tile(sp, (1024, 512, 256, 128))
    o_att_p = _attn_prompt(u, lam, subw, t=sp, heads=heads, tq=tq, post_scale=post_scale)

    q_s = lax.slice(u, (n_p, 0), (m, att_w))
    k_s = lax.slice(u, (n_p, att_w), (m, 2 * att_w)).reshape(bs, ts, heads, DV)
    v_s = lax.slice(u, (n_p, 2 * att_w), (m, 3 * att_w)).reshape(bs, ts, heads, DV)
    pad_new = lambda t: jnp.pad(t, ((0, 0), (0, PAGE - ts), (0, 0), (0, 0)))
    o_att_s = _attn_sample(q_s, cache_k, cache_v, pad_new(k_s), pad_new(v_s), page_table, lam, subw,
                           layer=l, post_scale=post_scale)
    o_att = jnp.concatenate([o_att_p, o_att_s.astype(BF16)], axis=0)

    wg = w_gate[l].astype(BF16)
    bg = b_gate[l][None, :]
    mixin = _gated_merge(xb, o_att, o_rwkv, wg[:, :d], wg[:, d:], bg[:, :d], bg[:, d:],
                         w_pa[l].astype(BF16), w_pb[l].astype(BF16), tm=tm, tn=_pick_tile(d, (256, 128)))
    pre1 = _matmul_resid(mixin, w_o[l].astype(BF16), x, alpha=alpha, tm=tm, tn=_pick_tile(d, (512, 256, 128)),
                         name="out_proj")
    tl = _pick_tile(m, (256, 128, 64, 32, 16, 8))
    x1, x1b = _layer_norm(pre1, row(ln1_g), row(ln1_b), tm=tl)

    d_ff = w_ffn_gate.shape[2]
    hmid = _swiglu(x1b, w_ffn_gate[l].astype(BF16), w_ffn_up[l].astype(BF16), tm=tm,
                   tn=_pick_tile(d_ff, (256, 128)))
    tmd = _pick_tile(m, (384, 256, 128, 64, 32, 16, 8))
    pre2 = _matmul_resid(hmid, w_ffn_down[l].astype(BF16), x1, alpha=alpha, tm=tmd,
                         tn=_pick_tile(d, (256, 128)), name="ffn_down")
    y, _ = _layer_norm(pre2, row(ln2_g), row(ln2_b), tm=tl)

    k_all = lax.slice(u, (0, att_w), (m, 2 * att_w))
    v_all = lax.slice(u, (0, 2 * att_w), (m, 3 * att_w))
    outs = dict(
        xp=y[:n_p].reshape(bp, sp, d), xs=y[n_p:].reshape(bs, ts, d),
        kp=k_all[:n_p].reshape(bp, sp, heads, DV), vp=v_all[:n_p].reshape(bp, sp, heads, DV),
        wp=new_wkv_p, sp=new_shift_p,
        ks=k_s, vs=v_s, ws=new_wkv_s, ss=new_shift_s)
    return outs


def kernel(x_prompt, x_sample, cache_k, cache_v, state_wkv, state_shift, page_table, w_in, mu_shift, w_decay0, w_decay2, a0, a2, g2, k_k, k_a, r_k, lnx_w, lnx_b, lambda_q1, lambda_k1, lambda_q2, lambda_k2, subln_w, w_pa, w_pb, w_gate, b_gate, w_o, ln1_g, ln1_b, w_ffn_gate, w_ffn_up, w_ffn_down, ln2_g, ln2_b):
    depth = w_in.shape[0]
    xp, xs = x_prompt, x_sample
    acc = {n: [] for n in ("kp", "vp", "wp", "sp", "ks", "vs", "ws", "ss")}
    for l in range(depth):
        o = _layer(l, depth, xp, xs, cache_k, cache_v, state_wkv, state_shift, page_table,
                   w_in, mu_shift, w_decay0, w_decay2, a0, a2, g2, k_k, k_a, r_k, lnx_w, lnx_b,
                   lambda_q1, lambda_k1, lambda_q2, lambda_k2, subln_w,
                   w_pa, w_pb, w_gate, b_gate, w_o, ln1_g, ln1_b,
                   w_ffn_gate, w_ffn_up, w_ffn_down, ln2_g, ln2_b)
        xp, xs = o["xp"], o["xs"]
        for n in acc:
            acc[n].append(o[n])
    st = lambda n: jnp.stack(acc[n])
    return (xp, xs, st("kp"), st("vp"), st("wp"), st("sp"), st("ks"), st("vs"), st("ws"), st("ss"))
```

```python
import functools
import math

import jax
import jax.numpy as jnp
from jax import lax
from jax.experimental import pallas as pl
from jax.experimental.pallas import tpu as pltpu

F32 = jnp.float32
BF16 = jnp.bfloat16

LANES = 128
SUBLANES = 8
VMEM_LIMIT = 56 * 1024 * 1024

DK = 64
DV = 128
HS_R = 64
W_LORA = 96
A_LORA = 96
G_LORA = 256
LORA_PAD = 128
ATTN_SCALE = DK ** -0.5
LOG2E = math.log2(math.e)
ATTN_KEY_SUB = 256
ATTN_Q_COLS = 128
SAMPLE_PAGES = 4
SUBLN_EPS = 1e-5
NEG_INF = -1e30
GN_EPS = 64e-5
LN_EPS = 1e-5
PAGE = 128
HEAD_TILE = 8
WKV_CHUNK = 64
WKV_PAIRS = 8


def _cparams(sem):
    return pltpu.CompilerParams(dimension_semantics=sem, vmem_limit_bytes=VMEM_LIMIT)


def _dot(a, b):
    return jnp.dot(a, b, preferred_element_type=F32)


def _dot_nt(a, b):
    return lax.dot_general(a, b, (((1,), (1,)), ((), ())), preferred_element_type=F32)


def _dot_tn(a, b):
    return lax.dot_general(a, b, (((0,), (0,)), ((), ())), preferred_element_type=F32)


def _sigmoid(x):
    return 1.0 / (1.0 + jnp.exp(-x))


def _mm_body(x_ref, w_ref, o_ref):
    o_ref[...] = _dot(x_ref[...], w_ref[...].astype(BF16)).astype(o_ref.dtype)


def _matmul(x, w, *, tm, tn, out_dtype, name, n=None):
    m, k = x.shape
    n = w.shape[1] if n is None else n
    return pl.pallas_call(
        _mm_body,
        grid=(m // tm, n // tn),
        in_specs=[pl.BlockSpec((tm, k), lambda i, j: (i, 0)),
                  pl.BlockSpec((k, tn), lambda i, j: (0, j))],
        out_specs=pl.BlockSpec((tm, tn), lambda i, j: (i, j)),
        out_shape=jax.ShapeDtypeStruct((m, n), out_dtype),
        compiler_params=_cparams(("parallel", "parallel")),
        name=name,
    )(x, w)


def _gated_merge_body(x_ref, oa_ref, ob_ref, wga_ref, wgb_ref, ba_ref, bb_ref, wpa_ref, wpb_ref, o_ref):
    x = x_ref[...]
    ga = _sigmoid(_dot(x, wga_ref[...]) + ba_ref[...])
    gb = _sigmoid(_dot(x, wgb_ref[...]) + bb_ref[...])
    o = ga * _dot(oa_ref[...], wpa_ref[...]) + gb * _dot(ob_ref[...], wpb_ref[...])
    o_ref[...] = o.astype(o_ref.dtype)


def _gated_merge(x, oa, ob, wga, wgb, ba, bb, wpa, wpb, *, tm, tn):
    m, d = x.shape
    ka = oa.shape[1]
    kb = ob.shape[1]
    n = wga.shape[1]
    row = lambda w: pl.BlockSpec((tm, w), lambda i, j: (i, 0))
    col = lambda k: pl.BlockSpec((k, tn), lambda i, j: (0, j))
    return pl.pallas_call(
        _gated_merge_body,
        grid=(m // tm, n // tn),
        in_specs=[row(d), row(ka), row(kb), col(d), col(d), col(1), col(1), col(ka), col(kb)],
        out_specs=pl.BlockSpec((tm, tn), lambda i, j: (i, j)),
        out_shape=jax.ShapeDtypeStruct((m, n), BF16),
        compiler_params=_cparams(("parallel", "parallel")),
        name="gated_merge",
    )(x, oa, ob, wga, wgb, ba, bb, wpa, wpb)


def _swiglu_body(x_ref, wg_ref, wu_ref, o_ref):
    x = x_ref[...]
    g = _dot(x, wg_ref[...].astype(BF16))
    u = _dot(x, wu_ref[...].astype(BF16))
    o_ref[...] = (g * _sigmoid(g) * u).astype(o_ref.dtype)


def _swiglu(x, wg, wu, *, tm, tn):
    m, k = x.shape
    n = wg.shape[1]
    return pl.pallas_call(
        _swiglu_body,
        grid=(m // tm, n // tn),
        in_specs=[pl.BlockSpec((tm, k), lambda i, j: (i, 0)),
                  pl.BlockSpec((k, tn), lambda i, j: (0, j)),
                  pl.BlockSpec((k, tn), lambda i, j: (0, j))],
        out_specs=pl.BlockSpec((tm, tn), lambda i, j: (i, j)),
        out_shape=jax.ShapeDtypeStruct((m, n), BF16),
        compiler_params=_cparams(("parallel", "parallel")),
        name="swiglu",
    )(x, wg, wu)


def _add_norm_body(*refs, alpha, res_first, out_first, out_blocks):
    n_res = len(res_first)
    x_ref = refs[0]
    res_refs = refs[1:1 + n_res]
    g_ref, b_ref = refs[1 + n_res:3 + n_res]
    out_refs = refs[3 + n_res:]
    i = pl.program_id(0)
    r = res_refs[0][...]
    for k in range(1, n_res):
        r = jnp.where(i >= res_first[k], res_refs[k][...], r)
    x = alpha * r + x_ref[...]
    mu = jnp.mean(x, axis=-1, keepdims=True)
    d = x - mu
    var = jnp.mean(d * d, axis=-1, keepdims=True)
    y = d * lax.rsqrt(var + LN_EPS) * g_ref[...] + b_ref[...]
    for o_ref, first, nblk in zip(out_refs, out_first, out_blocks):
        @pl.when((i >= first) & (i < first + nblk))
        def _():
            o_ref[...] = y.astype(o_ref.dtype)


def _add_norm(x, resids, g, b, *, alpha, tm, outs):
    m, d = x.shape
    row_spec = lambda first, rows: pl.BlockSpec(
        (tm, d), lambda i: (jnp.clip(i - first // tm, 0, rows // tm - 1), 0))
    vec = pl.BlockSpec((1, d), lambda i: (0, 0))
    body = functools.partial(
        _add_norm_body, alpha=alpha,
        res_first=tuple(first // tm for _, first in resids),
        out_first=tuple(first // tm for _, first, _ in outs),
        out_blocks=tuple(rows // tm for rows, _, _ in outs))
    return pl.pallas_call(
        body,
        grid=(m // tm,),
        in_specs=[row_spec(0, m)] + [row_spec(first, r.shape[0]) for r, first in resids] + [vec, vec],
        out_specs=[row_spec(first, rows) for rows, first, _ in outs],
        out_shape=[jax.ShapeDtypeStruct((rows, d), dt) for rows, _, dt in outs],
        compiler_params=_cparams(("arbitrary",)),
        name="add_norm",
    )(x, *[r for r, _ in resids], g, b)


def _rwkv_prep_body(ur_ref, uk_ref, uv_ref, ul_ref, ar_ref, ak_ref, av_ref, al_ref, prev_ref, mu_ref,
                    w0_ref, w2_ref, a0_ref, a2_ref, g2_ref, kk_ref, ka_ref,
                    r_out, lw_out, k_out, v_out, kk_out, a_out, g_out, *, tm, width, chained):
    first = lax.broadcasted_iota(jnp.int32, (tm, 1), 0) == 0
    start = pl.program_id(0) == 0

    def mixed(u_ref, above_ref, lo, hi):
        u = u_ref[...]
        prev = prev_ref[:, lo:hi]
        if chained:
            prev = jnp.where(start, prev, above_ref[SUBLANES - 1:SUBLANES, :])
        shifted = jnp.where(first, jnp.broadcast_to(prev, u.shape), pltpu.roll(u, shift=1, axis=0))
        return u + (shifted - u) * mu_ref[:, lo:hi]

    w = width
    r = mixed(ur_ref, ar_ref, 0, w)
    k = mixed(uk_ref, ak_ref, w, 2 * w)
    v = mixed(uv_ref, av_ref, 2 * w, 3 * w)
    lora = mixed(ul_ref, al_ref, 3 * w, 3 * w + 2 * LORA_PAD + G_LORA)
    w_l = lora[:, 0:LORA_PAD]
    a_l = lora[:, LORA_PAD:2 * LORA_PAD]
    g_l = lora[:, 2 * LORA_PAD:]

    z = w0_ref[...] + _dot(jnp.tanh(w_l).astype(BF16), w2_ref[...])
    nz = -z
    softplus = jnp.maximum(nz, 0.0) + jnp.log(1.0 + jnp.exp(-jnp.abs(nz)))
    wdec = -softplus - 0.5
    a = _sigmoid(a0_ref[...] + _dot(a_l.astype(BF16), a2_ref[...]))
    g = _dot(_sigmoid(g_l).astype(BF16), g2_ref[...])

    r_out[...] = r
    lw_out[...] = -jnp.exp(wdec)
    k_out[...] = k * (1.0 + (a - 1.0) * ka_ref[...])
    v_out[...] = v
    kk_out[...] = k * kk_ref[...]
    a_out[...] = a
    g_out[...] = g


def _rwkv_prep(ua, ub, row0, rows, prev, mu, w0, w2, a0, a2, g2, k_k, k_a, *, tm, width, r_col, chained):
    nblk = rows // tm
    rb0 = row0 // tm
    cb = r_col // width
    lw = 2 * LORA_PAD + G_LORA
    lb = 2 * width // lw
    rw_w = prev.shape[2]
    vec = lambda n: pl.BlockSpec((1, n), lambda i: (0, 0))
    out = jax.ShapeDtypeStruct((rows, width), F32)
    tiles = tm // SUBLANES
    above = lambda i: jnp.maximum((rb0 + i) * tiles - 1, 0)
    blk = lambda wd, c: pl.BlockSpec((tm, wd), lambda i: (rb0 + i, c))
    abv = lambda wd, c: pl.BlockSpec((SUBLANES, wd), lambda i: (above(i), c))
    body = functools.partial(_rwkv_prep_body, tm=tm, width=width, chained=chained)
    return pl.pallas_call(
        body,
        grid=(nblk,),
        in_specs=[blk(width, cb), blk(width, 0), blk(width, 1), blk(lw, lb),
                  abv(width, cb), abv(width, 0), abv(width, 1), abv(lw, lb),
                  pl.BlockSpec((None, 1, rw_w), (lambda i: (0, 0, 0)) if chained else (lambda i: (i, 0, 0))),
                  vec(rw_w), vec(width),
                  pl.BlockSpec((LORA_PAD, width), lambda i: (0, 0)),
                  vec(width),
                  pl.BlockSpec((LORA_PAD, width), lambda i: (0, 0)),
                  pl.BlockSpec((G_LORA, width), lambda i: (0, 0)),
                  vec(width), vec(width)],
        out_specs=[pl.BlockSpec((tm, width), lambda i: (i, 0))] * 7,
        out_shape=[out] * 7,
        compiler_params=_cparams(("parallel",)),
        name="rwkv_prep",
    )(ua, ub, ub, ub, ua, ub, ub, ub, prev, mu, w0, w2, a0, a2, g2, k_k, k_a)


def _wkv_pair(r, lw, k, v, kk, a, g, st, lnw, lnb, rk, *, chunk):
    C = chunk
    C2 = 2 * C

    lane = lax.broadcasted_iota(jnp.int32, (C2, LANES), 1)
    rowi = lax.broadcasted_iota(jnp.int32, (C2, LANES), 0)
    own = (lane >= HS_R) == (rowi >= C)

    def twice(x):
        return jnp.concatenate([x, x], axis=0)

    def stack(x):
        return jnp.where(own, twice(x), 0.0)

    tr = lax.broadcasted_iota(jnp.int32, (C, C), 0)
    tc = lax.broadcasted_iota(jnp.int32, (C, C), 1)
    tri = (tr >= tc).astype(BF16)
    lw_hi = lw.astype(BF16)
    rem = lw - lw_hi.astype(F32)
    lw_mid = rem.astype(BF16)
    lw_lo = (rem - lw_mid.astype(F32)).astype(BF16)
    G = _dot(tri, lw_hi) + _dot(tri, lw_mid) + _dot(tri, lw_lo)
    yield None
    GC = G[C - 1:C, :]
    e_prev = jnp.exp(G - lw)
    e_neg = jnp.exp(-G)
    e_pos = jnp.exp(G)
    e_rest = jnp.exp(GC - G)
    e_all = jnp.exp(GC)

    kk2 = stack(kk)
    nrm = jnp.sqrt(jnp.sum(kk2 * kk2, axis=1, keepdims=True))
    kk2 = kk2 / jnp.maximum(nrm, 1e-12)

    A2 = (kk2 * twice(-e_prev)).astype(BF16)
    B2 = (kk2 * twice(a * e_neg)).astype(BF16)
    K2 = stack(k * e_neg).astype(BF16)
    R2 = stack(r * e_pos).astype(BF16)
    Bt2 = (kk2 * twice(a * e_rest)).astype(BF16)
    Kt2 = stack(k * e_rest).astype(BF16)
    V2f = stack(v)
    V2 = V2f.astype(BF16)

    AR = jnp.concatenate([A2, R2], axis=0)
    mr = lax.broadcasted_iota(jnp.int32, (2 * C2, 2 * C2), 0)
    mc = lax.broadcasted_iota(jnp.int32, (2 * C2, 2 * C2), 1)
    tr_r = mr % C2
    tr_c = mc % C2
    keep = tr_r + jnp.where(mr < C2, 0, 1) > tr_c
    m_all = jnp.where(keep, _dot_nt(AR, jnp.concatenate([B2, K2], axis=0)), 0.0)
    m_ab = m_all[:C2, :C2]
    m_rb = m_all[C2:, :C2].astype(BF16)
    m_xk = m_all[:, C2:].astype(BF16)
    eye = (lax.broadcasted_iota(jnp.int32, (C2, C2), 0) == lax.broadcasted_iota(jnp.int32, (C2, C2), 1)).astype(F32)
    yield None

    t_inv = eye + m_ab
    pb = m_ab.astype(BF16)
    p = _dot(pb, pb)
    yield None
    n = 2
    while 2 * n < C:
        pb = p.astype(BF16)
        both = _dot(jnp.concatenate([pb, t_inv.astype(BF16)], axis=0), pb)
        p = both[:C2]
        t_inv = t_inv + both[C2:]
        n *= 2
        yield None
    t_inv = t_inv + _dot(t_inv.astype(BF16), p.astype(BF16))
    yield None

    stb = st.astype(BF16)
    xy = _dot(AR, stb) + _dot(m_xk, V2)
    yield None
    u2 = _dot(t_inv.astype(BF16), xy[:C2].astype(BF16))
    yield None
    u2b = u2.astype(BF16)
    y2 = xy[C2:] + _dot(m_rb, u2b)

    decay_col = jnp.broadcast_to(e_all, (LANES, LANES)).T
    st_new = decay_col * st + _dot_tn(jnp.concatenate([Bt2, Kt2], axis=0), jnp.concatenate([u2b, V2], axis=0))
    yield None

    inv_hs = 1.0 / HS_R
    mean = jnp.sum(y2, axis=1, keepdims=True) * inv_hs
    d = jnp.where(own, y2 - mean, 0.0)
    var = jnp.sum(d * d, axis=1, keepdims=True) * inv_hs
    yn2 = d * lax.rsqrt(var + GN_EPS)
    bonus2 = jnp.sum(stack(r * k * rk), axis=1, keepdims=True) * V2f
    yn = yn2[:C] + yn2[C:]
    bonus = bonus2[:C] + bonus2[C:]
    yield (yn * lnw + lnb + bonus) * g, st_new


def _wkv_body(r_ref, lw_ref, k_ref, v_ref, kk_ref, a_ref, g_ref, s0_ref, lnw_ref, lnb_ref, rk_ref,
              o_ref, st_ref, st_scr, *, chunk, pairs):
    c = pl.program_id(2)

    @pl.when(c == 0)
    def _():
        st_scr[...] = s0_ref[...]

    lanes = [slice(pp * LANES, (pp + 1) * LANES) for pp in range(pairs)]
    gens = [_wkv_pair(r_ref[:, sl], lw_ref[:, sl], k_ref[:, sl], v_ref[:, sl], kk_ref[:, sl],
                      a_ref[:, sl], g_ref[:, sl], st_scr[pp], lnw_ref[:, sl], lnb_ref[:, sl],
                      rk_ref[:, sl], chunk=chunk) for pp, sl in enumerate(lanes)]
    results = [None] * pairs
    while any(res is None for res in results):
        results = [next(gen) for gen in gens]
    for pp, (o, st_new) in enumerate(results):
        st_scr[pp] = st_new
        o_ref[:, lanes[pp]] = o.astype(o_ref.dtype)

    @pl.when(c == pl.num_programs(2) - 1)
    def _():
        st_ref[...] = st_scr[...]


def _wkv(r, lw, k, v, kk, a, g, s0, lnw, lnb, rk, *, seqs, chunk, pairs):
    rows, width = r.shape
    t = rows // seqs
    nchunk = t // chunk
    npair = width // LANES
    wblk = pairs * LANES
    blk = pl.BlockSpec((chunk, wblk), lambda b, p, c: (b * nchunk + c, p))
    vec = pl.BlockSpec((1, wblk), lambda b, p, c: (0, p))
    stspec = pl.BlockSpec((None, pairs, LANES, LANES), lambda b, p, c: (b, p, 0, 0))
    return pl.pallas_call(
        functools.partial(_wkv_body, chunk=chunk, pairs=pairs),
        grid=(seqs, npair // pairs, nchunk),
        in_specs=[blk] * 7 + [stspec, vec, vec, vec],
        out_specs=[blk, stspec],
        out_shape=[jax.ShapeDtypeStruct((rows, width), BF16),
                   jax.ShapeDtypeStruct((seqs, npair, LANES, LANES), F32)],
        scratch_shapes=[pltpu.VMEM((pairs, LANES, LANES), F32)],
        compiler_params=_cparams(("parallel", "parallel", "arbitrary")),
        name="wkv7_chunked",
    )(r, lw, k, v, kk, a, g, s0, lnw, lnb, rk)


def _two_maps(q):
    lane = lax.broadcasted_iota(jnp.int32, q.shape, 1)
    q = q * (ATTN_SCALE * LOG2E)
    return jnp.where(lane < DK, q, 0.0), jnp.where(lane >= DK, q, 0.0)


def _attn_prompt_body(qi_ref, ki_ref, lam_ref, q_ref, k_ref, v_ref, sw_ref, o_ref,
                      q2_scr, m_scr, l_scr, acc_scr, *, tq, post_scale):
    step = pl.program_id(1)
    qi = qi_ref[step]
    ki = ki_ref[step]

    @pl.when(ki == 0)
    def _():
        q1, q2 = _two_maps(q_ref[...])
        q2_scr[...] = jnp.concatenate([q1, q2], axis=0).astype(BF16)
        m_scr[...] = jnp.full_like(m_scr, NEG_INF)
        l_scr[...] = jnp.zeros_like(l_scr)
        acc_scr[...] = jnp.zeros_like(acc_scr)

    def update(diagonal):
        for kb in range(tq // ATTN_KEY_SUB):
            k0 = kb * ATTN_KEY_SUB
            kblk = k_ref[k0:k0 + ATTN_KEY_SUB, :].astype(BF16)
            vtblk = v_ref[k0:k0 + ATTN_KEY_SUB, :].T.astype(BF16)
            for c in range(2 * tq // ATTN_Q_COLS):
                q0 = (c * ATTN_Q_COLS) % tq
                if diagonal and k0 > q0 + ATTN_Q_COLS - 1:
                    continue
                cs = slice(c * ATTN_Q_COLS, (c + 1) * ATTN_Q_COLS)
                st = _dot_nt(kblk, q2_scr[cs, :])
                if diagonal and k0 + ATTN_KEY_SUB - 1 > q0:
                    krow = k0 + lax.broadcasted_iota(jnp.int32, st.shape, 0)
                    qcol = q0 + lax.broadcasted_iota(jnp.int32, st.shape, 1)
                    st = jnp.where(krow <= qcol, st, NEG_INF)
                m_old = m_scr[:, cs]
                m_new = jnp.maximum(m_old, jnp.max(st, axis=0, keepdims=True))
                alpha = jnp.exp2(m_old - m_new)
                p = jnp.exp2(st - m_new)
                l_scr[:, cs] = alpha * l_scr[:, cs] + jnp.sum(p, axis=0, keepdims=True)
                acc_scr[:, cs] = alpha * acc_scr[:, cs] + _dot(vtblk, p.astype(BF16))
                m_scr[:, cs] = m_new

    @pl.when(ki < qi)
    def _():
        update(False)

    @pl.when(ki == qi)
    def _():
        update(True)
        acc = acc_scr[...]
        l = l_scr[...]
        ot = acc[:, :tq] / l[:, :tq] - lam_ref[:, 0:1] * (acc[:, tq:] / l[:, tq:])
        ms = jnp.mean(ot * ot, axis=0, keepdims=True)
        on = ot * lax.rsqrt(ms + SUBLN_EPS)
        o_ref[...] = (on.T * sw_ref[...] * post_scale).astype(o_ref.dtype)


def _attn_prompt(u, lam, subw, *, t, heads, tq, post_scale):
    nq = t // tq
    pairs = [(i, j) for i in range(nq) for j in range(i + 1)]
    qi_tab = jnp.array([p[0] for p in pairs], jnp.int32)
    ki_tab = jnp.array([p[1] for p in pairs], jnp.int32)
    vec = pl.BlockSpec((1, LANES), lambda h, s, qi, ki: (0, 0))
    grid_spec = pltpu.PrefetchScalarGridSpec(
        num_scalar_prefetch=2,
        grid=(heads, len(pairs)),
        in_specs=[vec,
                  pl.BlockSpec((tq, LANES), lambda h, s, qi, ki: (qi[s], h)),
                  pl.BlockSpec((tq, LANES), lambda h, s, qi, ki: (ki[s], heads + h)),
                  pl.BlockSpec((tq, LANES), lambda h, s, qi, ki: (ki[s], 2 * heads + h)),
                  vec],
        out_specs=pl.BlockSpec((tq, LANES), lambda h, s, qi, ki: (qi[s], h)),
        scratch_shapes=[pltpu.VMEM((2 * tq, LANES), BF16),
                        pltpu.VMEM((1, 2 * tq), F32),
                        pltpu.VMEM((1, 2 * tq), F32),
                        pltpu.VMEM((LANES, 2 * tq), F32)],
    )
    return pl.pallas_call(
        functools.partial(_attn_prompt_body, tq=tq, post_scale=post_scale),
        grid_spec=grid_spec,
        out_shape=jax.ShapeDtypeStruct((t, heads * LANES), BF16),
        compiler_params=_cparams(("parallel", "arbitrary")),
        name="diff_attn_prompt",
    )(qi_tab, ki_tab, lam, u, u, u, subw)


def _attn_sample_body(pt_ref, lam_ref, q_ref, *refs, heads, tn, post_scale, pages):
    groups = heads // HEAD_TILE
    n_in = pages * groups
    kc_refs = [refs[k * groups:(k + 1) * groups] for k in range(pages)]
    vc_refs = [refs[n_in + k * groups:n_in + (k + 1) * groups] for k in range(pages)]
    kn_refs = refs[2 * n_in:2 * n_in + groups]
    vn_refs = refs[2 * n_in + groups:2 * n_in + 2 * groups]
    bias_ref, biasn_ref, sw_ref, o_ref, q2_scr, m_scr, l_scr, acc_scr = refs[2 * n_in + 2 * groups:]
    j = pl.program_id(1)
    n_steps = pl.num_programs(1) - 1
    half = HEAD_TILE * tn

    @pl.when(j == 0)
    def _():
        for h in range(heads):
            g, hh = divmod(h, HEAD_TILE)
            q1, q2 = _two_maps(q_ref[:, h * LANES:(h + 1) * LANES])
            q2_scr[g, hh * tn:(hh + 1) * tn] = q1
            q2_scr[g, half + hh * tn:half + (hh + 1) * tn] = q2
        m_scr[...] = jnp.full_like(m_scr, NEG_INF)
        l_scr[...] = jnp.zeros_like(l_scr)
        acc_scr[...] = jnp.zeros_like(acc_scr)

    def attend(k_pages, v_pages, b_ref):
        bias = b_ref[...]
        flat = lambda ref: ref[...].reshape(PAGE * HEAD_TILE, LANES).astype(BF16)
        for g in range(groups):
            q2 = q2_scr[g].astype(BF16)
            ss = [_dot_nt(q2, flat(kp[g])) + bias for kp in k_pages]
            m_old = m_scr[g]
            m_new = m_old
            for s in ss:
                m_new = jnp.maximum(m_new, jnp.max(s, axis=1, keepdims=True))
            alpha = jnp.exp2(m_old - m_new)
            l_new = alpha * l_scr[g]
            acc_new = alpha * acc_scr[g]
            for s, vp in zip(ss, v_pages):
                p = jnp.exp2(s - m_new)
                l_new = l_new + jnp.sum(p, axis=1, keepdims=True)
                acc_new = acc_new + _dot(p.astype(BF16), flat(vp[g]))
            l_scr[g] = l_new
            acc_scr[g] = acc_new
            m_scr[g] = m_new

    @pl.when(j < n_steps)
    def _():
        attend(kc_refs, vc_refs, bias_ref)

    @pl.when(j == n_steps)
    def _():
        attend([kn_refs], [vn_refs], biasn_ref)
        for g in range(groups):
            acc = acc_scr[g]
            l = l_scr[g]
            o = acc[:half] / l[:half] - lam_ref[...] * (acc[half:] / l[half:])
            ms = jnp.mean(o * o, axis=-1, keepdims=True)
            o = o * lax.rsqrt(ms + SUBLN_EPS) * sw_ref[...] * post_scale
            for hh in range(HEAD_TILE):
                h = g * HEAD_TILE + hh
                o_ref[:, h * LANES:(h + 1) * LANES] = o[hh * tn:(hh + 1) * tn]


def _sample_bias(tn, causal):
    r = jnp.arange(2 * HEAD_TILE * tn)[:, None]
    c = jnp.arange(PAGE * HEAD_TILE)[None, :]
    ok = (c % HEAD_TILE) == ((r % (HEAD_TILE * tn)) // tn)
    if causal:
        ok = ok & ((c // HEAD_TILE) <= (r % tn))
    return jnp.where(ok, 0.0, NEG_INF).astype(F32)


def _attn_sample(q, cache_k, cache_v, k_new, v_new, page_table, lam, subw, *, layer, post_scale):
    b, n_pages = page_table.shape
    n_layers, n_pool, _, heads, _ = cache_k.shape
    groups = heads // HEAD_TILE
    tn = q.shape[0] // b
    rows = 2 * HEAD_TILE * tn
    pt = page_table.reshape(-1)
    pages = _pick_tile(n_pages, (SAMPLE_PAGES, 1))
    n_steps = n_pages // pages
    tiled = lambda c: c.reshape(c.shape[:-2] + (groups, HEAD_TILE, LANES))

    def page_spec(k, g):
        def page_map(i, j, pt_ref):
            return (layer, pt_ref[i * n_pages + jnp.minimum(j, n_steps - 1) * pages + k], 0, g, 0, 0)
        return pl.BlockSpec((None, None, PAGE, None, HEAD_TILE, LANES), page_map)

    def new_spec(g):
        return pl.BlockSpec((None, PAGE, None, HEAD_TILE, LANES), lambda i, j, pt_ref: (i, 0, g, 0, 0))

    page_specs = [page_spec(k, g) for k in range(pages) for g in range(groups)]
    new_specs = [new_spec(g) for g in range(groups)]
    vec = pl.BlockSpec((1, LANES), lambda i, j, pt_ref: (0, 0))
    bias_spec = pl.BlockSpec((rows, PAGE * HEAD_TILE), lambda i, j, pt_ref: (0, 0))
    grid_spec = pltpu.PrefetchScalarGridSpec(
        num_scalar_prefetch=1,
        grid=(b, n_steps + 1),
        in_specs=[vec,
                  pl.BlockSpec((tn, heads * LANES), lambda i, j, pt_ref: (i, 0)),
                  *page_specs, *page_specs, *new_specs, *new_specs, bias_spec, bias_spec, vec],
        out_specs=pl.BlockSpec((tn, heads * LANES), lambda i, j, pt_ref: (i, 0)),
        scratch_shapes=[pltpu.VMEM((groups, rows, LANES), F32),
                        pltpu.VMEM((groups, rows, 1), F32),
                        pltpu.VMEM((groups, rows, 1), F32),
                        pltpu.VMEM((groups, rows, LANES), F32)],
    )
    n_in = pages * groups
    return pl.pallas_call(
        functools.partial(_attn_sample_body, heads=heads, tn=tn, post_scale=post_scale, pages=pages),
        grid_spec=grid_spec,
        out_shape=jax.ShapeDtypeStruct((b * tn, heads * LANES), F32),
        compiler_params=_cparams(("parallel", "arbitrary")),
        name="diff_attn_sample",
    )(pt, lam, q, *([tiled(cache_k)] * n_in), *([tiled(cache_v)] * n_in),
      *([tiled(k_new)] * groups), *([tiled(v_new)] * groups),
      _sample_bias(tn, False), _sample_bias(tn, True), subw)


def _pick_tile(n, candidates):
    for c in candidates:
        if n % c == 0:
            return c
    raise ValueError(f"no tile for {n}")


def _pair_states(s):
    b, h, _, _ = s.shape
    st = jnp.swapaxes(s, -1, -2).reshape(b, h // 2, 2, HS_R, HS_R)
    z = jnp.zeros_like(st[:, :, 0])
    top = jnp.concatenate([st[:, :, 0], z], axis=-1)
    bot = jnp.concatenate([z, st[:, :, 1]], axis=-1)
    return jnp.concatenate([top, bot], axis=-2)


def _head_states(sp):
    b, hp, _, _ = sp.shape
    s0 = sp[:, :, :HS_R, :HS_R]
    s1 = sp[:, :, HS_R:, HS_R:]
    st = jnp.stack([s0, s1], axis=2).reshape(b, 2 * hp, HS_R, HS_R)
    return jnp.swapaxes(st, -1, -2)


def _layer(l, depth, xp, xs, cache_k, cache_v, state_wkv, state_shift, page_table,
           w_in, mu_shift, w_decay0, w_decay2, a0, a2, g2, k_k, k_a, r_k, lnx_w, lnx_b,
           lambda_q1, lambda_k1, lambda_q2, lambda_k2, subln_w,
           w_pa, w_pb, w_gate, b_gate, w_o, ln1_g, ln1_b,
           w_ffn_gate, w_ffn_up, w_ffn_down, ln2_g, ln2_b):
    bp, sp, d = xp.shape
    bs, ts, _ = xs.shape
    assert bp == 1
    heads = cache_k.shape[3]
    att_w = heads * LANES
    rw_w = w_decay0.shape[1]
    h_r = rw_w // HS_R
    rw_in = 3 * rw_w + W_LORA + A_LORA + G_LORA
    assert w_in.shape[2] == 3 * att_w + rw_in and att_w == rw_w
    alpha = (2 * depth) ** 0.25
    lam_init = 0.8 - 0.6 * math.exp(-0.3 * l)
    post_scale = 1.0 - lam_init

    lam = (jnp.exp(jnp.sum(lambda_q1[l] * lambda_k1[l])) - jnp.exp(jnp.sum(lambda_q2[l] * lambda_k2[l]))
           + lam_init)
    lam = jnp.broadcast_to(lam.astype(F32), (1, LANES))

    n_p = bp * sp
    n_s = bs * ts
    m = n_p + n_s
    xp2 = xp.reshape(n_p, d)
    xs2 = xs.reshape(n_s, d)
    xb = jnp.concatenate([xp2.astype(BF16), xs2.astype(BF16)], axis=0)

    o_rw = 3 * att_w
    c_wl = o_rw + rw_w
    c_k = c_wl + W_LORA
    c_al = c_k + 2 * rw_w
    c_gl = c_al + A_LORA
    lora_w = 2 * LORA_PAD + G_LORA
    rw_cols = 3 * rw_w + lora_w

    def rest_cols(t, off):
        take = lambda a, b: lax.slice_in_dim(t, a - off, b - off, axis=1)
        z = jnp.zeros((t.shape[0], LORA_PAD - W_LORA), t.dtype)
        return jnp.concatenate([take(c_k, c_al), take(c_wl, c_k), z, take(c_al, c_gl), z,
                                take(c_gl, c_gl + G_LORA)], axis=1)

    def prep_cols(t):
        return jnp.concatenate([t[:, :rw_w], rest_cols(t, o_rw)], axis=1)

    w_bf = w_in[l].astype(BF16)
    w_rest = rest_cols(w_bf, 0)
    tm = _pick_tile(m, (768, 512, 384, 256, 128, 64, 32, 16, 8))
    ua = _matmul(xb, w_bf, n=c_wl, tm=tm, tn=_pick_tile(c_wl, (512, 256, 128)), out_dtype=F32,
                 name="in_proj_qkvr")
    ub = _matmul(xb, w_rest, tm=tm, tn=_pick_tile(w_rest.shape[1], (512, 256, 128)), out_dtype=F32,
                 name="in_proj_rest")

    mu = prep_cols(mu_shift[l][None, :])
    pad_rows = lambda t: jnp.concatenate([t, jnp.zeros((LORA_PAD - t.shape[0], t.shape[1]), t.dtype)], axis=0)
    w2 = pad_rows(w_decay2[l]).astype(BF16)
    a2b = pad_rows(a2[l]).astype(BF16)
    g2b = g2[l].astype(BF16)
    row = lambda t: t[l][None, :]
    prep_args = (mu, row(w_decay0), w2, row(a0), a2b, g2b, row(k_k), row(k_a))

    tp = _pick_tile(sp, (128, 64, 32, 16, 8))
    prev_p = jnp.zeros((bp, 1, rw_cols), F32)
    prep_p = _rwkv_prep(ua, ub, 0, sp, prev_p, *prep_args, tm=tp, width=rw_w, r_col=o_rw, chained=True)

    prev_s = prep_cols(state_shift[l]).reshape(bs, 1, rw_cols)
    prep_s = _rwkv_prep(ua, ub, n_p, n_s, prev_s, *prep_args, tm=ts, width=rw_w, r_col=o_rw, chained=False)

    wkv_params = (row(lnx_w), row(lnx_b), r_k[l].reshape(1, rw_w))
    s0_p = jnp.zeros((bp, h_r // 2, LANES, LANES), F32)
    pairs = _pick_tile(h_r // 2, (WKV_PAIRS, 2, 1))
    o_rw_p, st_p = _wkv(*prep_p, s0_p, *wkv_params, seqs=bp, chunk=WKV_CHUNK, pairs=pairs)

    t_pad = -(-ts // WKV_CHUNK) * WKV_CHUNK
    pad_t = lambda t: jnp.pad(t.reshape(bs, ts, rw_w), ((0, 0), (0, t_pad - ts), (0, 0))).reshape(bs * t_pad, rw_w)
    s0_s = _pair_states(state_wkv[l].astype(F32))
    o_rw_s, st_s = _wkv(*[pad_t(t) for t in prep_s], s0_s, *wkv_params, seqs=bs, chunk=WKV_CHUNK, pairs=pairs)
    o_rw_s = o_rw_s.reshape(bs, t_pad, rw_w)[:, :ts].reshape(n_s, rw_w)
    o_rwkv = jnp.concatenate([o_rw_p, o_rw_s], axis=0)

    new_wkv_p = _head_states(st_p).astype(state_wkv.dtype)
    new_wkv_s = _head_states(st_s).astype(state_wkv.dtype)

    def shift_row(rows):
        ta = ua[rows]
        tb = ub[rows]
        w_ = rw_w
        return jnp.concatenate([ta[:, o_rw:], tb[:, 2 * w_:2 * w_ + W_LORA], tb[:, :2 * w_],
                                tb[:, 2 * w_ + LORA_PAD:2 * w_ + LORA_PAD + A_LORA],
                                tb[:, 2 * w_ + 2 * LORA_PAD:]], axis=1)

    new_shift_p = shift_row(jnp.array([sp - 1]))
    new_shift_s = shift_row(n_p + ts - 1 + ts * jnp.arange(bs))

    subw = subln_w[l][None, :]
    tq = _pick_tile(sp, (1024, 512, 256, 128))
    o_att_p = _attn_prompt(ua, lam, subw, t=sp, heads=heads, tq=tq, post_scale=post_scale)

    q_s = lax.slice(ua, (n_p, 0), (m, att_w))
    k_s = lax.slice(ua, (n_p, att_w), (m, 2 * att_w)).reshape(bs, ts, heads, DV)
    v_s = lax.slice(ua, (n_p, 2 * att_w), (m, 3 * att_w)).reshape(bs, ts, heads, DV)
    pad_new = lambda t: jnp.pad(t, ((0, 0), (0, PAGE - ts), (0, 0), (0, 0)))
    o_att_s = _attn_sample(q_s, cache_k, cache_v, pad_new(k_s), pad_new(v_s), page_table, lam, subw,
                           layer=l, post_scale=post_scale)
    o_att = jnp.concatenate([o_att_p, o_att_s.astype(BF16)], axis=0)

    wg = w_gate[l].astype(BF16)
    bg = b_gate[l][None, :]
    mixin = _gated_merge(xb, o_att, o_rwkv, wg[:, :d], wg[:, d:], bg[:, :d], bg[:, d:],
                         w_pa[l].astype(BF16), w_pb[l].astype(BF16), tm=tm, tn=_pick_tile(d, (256, 128)))
    mix = _matmul(mixin, w_o[l], tm=tm, tn=_pick_tile(d, (512, 256, 128)), out_dtype=F32, name="out_proj")
    tl = _pick_tile(math.gcd(n_p, n_s), (256, 128, 64, 32, 16, 8))
    x1, x1b = _add_norm(mix, [(xp2, 0), (xs2, n_p)], row(ln1_g), row(ln1_b), alpha=alpha, tm=tl,
                        outs=[(m, 0, F32), (m, 0, BF16)])

    d_ff = w_ffn_gate.shape[2]
    hmid = _swiglu(x1b, w_ffn_gate[l], w_ffn_up[l], tm=tm, tn=_pick_tile(d_ff, (256, 128)))
    tmd = _pick_tile(m, (384, 256, 128, 64, 32, 16, 8))
    down = _matmul(hmid, w_ffn_down[l].astype(BF16), tm=tmd, tn=_pick_tile(d, (256, 128)), out_dtype=F32,
                   name="ffn_down")
    y_p, y_s = _add_norm(down, [(x1, 0)], row(ln2_g), row(ln2_b), alpha=alpha, tm=tl,
                         outs=[(n_p, 0, F32), (n_s, n_p, F32)])

    k_all = lax.slice(ua, (0, att_w), (m, 2 * att_w))
    v_all = lax.slice(ua, (0, 2 * att_w), (m, 3 * att_w))
    outs = dict(
        xp=y_p.reshape(bp, sp, d), xs=y_s.reshape(bs, ts, d),
        kp=k_all[:n_p].reshape(bp, sp, heads, DV), vp=v_all[:n_p].reshape(bp, sp, heads, DV),
        wp=new_wkv_p, sp=new_shift_p,
        ks=k_s, vs=v_s, ws=new_wkv_s, ss=new_shift_s)
    return outs


def kernel(x_prompt, x_sample, cache_k, cache_v, state_wkv, state_shift, page_table, w_in, mu_shift, w_decay0, w_decay2, a0, a2, g2, k_k, k_a, r_k, lnx_w, lnx_b, lambda_q1, lambda_k1, lambda_q2, lambda_k2, subln_w, w_pa, w_pb, w_gate, b_gate, w_o, ln1_g, ln1_b, w_ffn_gate, w_ffn_up, w_ffn_down, ln2_g, ln2_b):
    depth = w_in.shape[0]
    xp, xs = x_prompt, x_sample
    acc = {n: [] for n in ("kp", "vp", "wp", "sp", "ks", "vs", "ws", "ss")}
    for l in range(depth):
        o = _layer(l, depth, xp, xs, cache_k, cache_v, state_wkv, state_shift, page_table,
                   w_in, mu_shift, w_decay0, w_decay2, a0, a2, g2, k_k, k_a, r_k, lnx_w, lnx_b,
                   lambda_q1, lambda_k1, lambda_q2, lambda_k2, subln_w,
                   w_pa, w_pb, w_gate, b_gate, w_o, ln1_g, ln1_b,
                   w_ffn_gate, w_ffn_up, w_ffn_down, ln2_g, ln2_b)
        xp, xs = o["xp"], o["xs"]
        for n in acc:
            acc[n].append(o[n])
    st = lambda n: jnp.stack(acc[n])
    return (xp, xs, st("kp"), st("vp"), st("wp"), st("sp"), st("ks"), st("vs"), st("ws"), st("ss"))
```

```python
import functools
import math

import jax
import jax.numpy as jnp
from jax import lax
from jax.experimental import pallas as pl
from jax.experimental.pallas import tpu as pltpu

F32 = jnp.float32
BF16 = jnp.bfloat16

LANES = 128
SUBLANES = 8
VMEM_LIMIT = 56 * 1024 * 1024

DK = 64
DV = 128
HS_R = 64
W_LORA = 96
A_LORA = 96
G_LORA = 256
LORA_PAD = 128
ATTN_SCALE = DK ** -0.5
LOG2E = math.log2(math.e)
ATTN_KEY_SUB = 256
ATTN_Q_COLS = 128
SAMPLE_PAGES = 4
SUBLN_EPS = 1e-5
NEG_INF = -1e30
GN_EPS = 64e-5
LN_EPS = 1e-5
PAGE = 128
HEAD_TILE = 8
WKV_CHUNK = 64
WKV_PAIRS = 8


def _cparams(sem):
    return pltpu.CompilerParams(dimension_semantics=sem, vmem_limit_bytes=VMEM_LIMIT)


def _dot(a, b):
    return jnp.dot(a, b, preferred_element_type=F32)


def _dot_nt(a, b):
    return lax.dot_general(a, b, (((1,), (1,)), ((), ())), preferred_element_type=F32)


def _dot_tn(a, b):
    return lax.dot_general(a, b, (((0,), (0,)), ((), ())), preferred_element_type=F32)


def _sigmoid(x):
    return 1.0 / (1.0 + jnp.exp(-x))


def _mm_body(x_ref, w_ref, o_ref):
    o_ref[...] = _dot(x_ref[...], w_ref[...].astype(BF16)).astype(o_ref.dtype)


def _matmul(x, w, *, tm, tn, out_dtype, name, n=None):
    m, k = x.shape
    n = w.shape[1] if n is None else n
    return pl.pallas_call(
        _mm_body,
        grid=(m // tm, n // tn),
        in_specs=[pl.BlockSpec((tm, k), lambda i, j: (i, 0)),
                  pl.BlockSpec((k, tn), lambda i, j: (0, j))],
        out_specs=pl.BlockSpec((tm, tn), lambda i, j: (i, j)),
        out_shape=jax.ShapeDtypeStruct((m, n), out_dtype),
        compiler_params=_cparams(("parallel", "parallel")),
        name=name,
    )(x, w)


def _mm_nt_body(x_ref, wt_ref, o_ref):
    o_ref[...] = _dot_nt(x_ref[...], wt_ref[...].astype(BF16)).astype(o_ref.dtype)


def _matmul_nt(x, wt, *, n, tm, tn, out_dtype, name):
    m, k = x.shape
    return pl.pallas_call(
        _mm_nt_body,
        grid=(m // tm, n // tn),
        in_specs=[pl.BlockSpec((tm, k), lambda i, j: (i, 0)),
                  pl.BlockSpec((tn, k), lambda i, j: (j, 0))],
        out_specs=pl.BlockSpec((tm, tn), lambda i, j: (i, j)),
        out_shape=jax.ShapeDtypeStruct((m, n), out_dtype),
        compiler_params=_cparams(("parallel", "parallel")),
        name=name,
    )(x, wt)


def _gated_merge_body(x_ref, oa_ref, ob_ref, wga_ref, wgb_ref, ba_ref, bb_ref, wpa_ref, wpb_ref, o_ref):
    x = x_ref[...]
    ga = _sigmoid(_dot(x, wga_ref[...]) + ba_ref[...])
    gb = _sigmoid(_dot(x, wgb_ref[...]) + bb_ref[...])
    o = ga * _dot(oa_ref[...], wpa_ref[...]) + gb * _dot(ob_ref[...], wpb_ref[...])
    o_ref[...] = o.astype(o_ref.dtype)


def _gated_merge(x, oa, ob, wga, wgb, ba, bb, wpa, wpb, *, tm, tn):
    m, d = x.shape
    ka = oa.shape[1]
    kb = ob.shape[1]
    n = wga.shape[1]
    row = lambda w: pl.BlockSpec((tm, w), lambda i, j: (i, 0))
    col = lambda k: pl.BlockSpec((k, tn), lambda i, j: (0, j))
    return pl.pallas_call(
        _gated_merge_body,
        grid=(m // tm, n // tn),
        in_specs=[row(d), row(ka), row(kb), col(d), col(d), col(1), col(1), col(ka), col(kb)],
        out_specs=pl.BlockSpec((tm, tn), lambda i, j: (i, j)),
        out_shape=jax.ShapeDtypeStruct((m, n), BF16),
        compiler_params=_cparams(("parallel", "parallel")),
        name="gated_merge",
    )(x, oa, ob, wga, wgb, ba, bb, wpa, wpb)


def _swiglu_body(x_ref, wg_ref, wu_ref, o_ref):
    x = x_ref[...]
    g = _dot(x, wg_ref[...].astype(BF16))
    u = _dot(x, wu_ref[...].astype(BF16))
    o_ref[...] = (g * _sigmoid(g) * u).astype(o_ref.dtype)


def _swiglu(x, wg, wu, *, tm, tn):
    m, k = x.shape
    n = wg.shape[1]
    return pl.pallas_call(
        _swiglu_body,
        grid=(m // tm, n // tn),
        in_specs=[pl.BlockSpec((tm, k), lambda i, j: (i, 0)),
                  pl.BlockSpec((k, tn), lambda i, j: (0, j)),
                  pl.BlockSpec((k, tn), lambda i, j: (0, j))],
        out_specs=pl.BlockSpec((tm, tn), lambda i, j: (i, j)),
        out_shape=jax.ShapeDtypeStruct((m, n), BF16),
        compiler_params=_cparams(("parallel", "parallel")),
        name="swiglu",
    )(x, wg, wu)


def _add_norm_body(*refs, alpha, res_first, out_first, out_blocks):
    n_res = len(res_first)
    x_ref = refs[0]
    res_refs = refs[1:1 + n_res]
    g_ref, b_ref = refs[1 + n_res:3 + n_res]
    out_refs = refs[3 + n_res:]
    i = pl.program_id(0)
    r = res_refs[0][...]
    for k in range(1, n_res):
        r = jnp.where(i >= res_first[k], res_refs[k][...], r)
    x = alpha * r + x_ref[...]
    mu = jnp.mean(x, axis=-1, keepdims=True)
    d = x - mu
    var = jnp.mean(d * d, axis=-1, keepdims=True)
    y = d * lax.rsqrt(var + LN_EPS) * g_ref[...] + b_ref[...]
    for o_ref, first, nblk in zip(out_refs, out_first, out_blocks):
        @pl.when((i >= first) & (i < first + nblk))
        def _():
            o_ref[...] = y.astype(o_ref.dtype)


def _add_norm(x, resids, g, b, *, alpha, tm, outs):
    m, d = x.shape
    row_spec = lambda first, rows: pl.BlockSpec(
        (tm, d), lambda i: (jnp.clip(i - first // tm, 0, rows // tm - 1), 0))
    vec = pl.BlockSpec((1, d), lambda i: (0, 0))
    body = functools.partial(
        _add_norm_body, alpha=alpha,
        res_first=tuple(first // tm for _, first in resids),
        out_first=tuple(first // tm for _, first, _ in outs),
        out_blocks=tuple(rows // tm for rows, _, _ in outs))
    return pl.pallas_call(
        body,
        grid=(m // tm,),
        in_specs=[row_spec(0, m)] + [row_spec(first, r.shape[0]) for r, first in resids] + [vec, vec],
        out_specs=[row_spec(first, rows) for rows, first, _ in outs],
        out_shape=[jax.ShapeDtypeStruct((rows, d), dt) for rows, _, dt in outs],
        compiler_params=_cparams(("arbitrary",)),
        name="add_norm",
    )(x, *[r for r, _ in resids], g, b)


def _rwkv_prep_body(ur_ref, uk_ref, uv_ref, ul_ref, ar_ref, ak_ref, av_ref, al_ref, prev_ref, mu_ref,
                    w0_ref, w2_ref, a0_ref, a2_ref, g2_ref, kk_ref, ka_ref,
                    r_out, lw_out, k_out, v_out, kk_out, a_out, g_out, *, tm, width, chained):
    first = lax.broadcasted_iota(jnp.int32, (tm, 1), 0) == 0
    start = pl.program_id(0) == 0

    def mixed(u_ref, above_ref, lo, hi):
        u = u_ref[...]
        prev = prev_ref[:, lo:hi]
        if chained:
            prev = jnp.where(start, prev, above_ref[SUBLANES - 1:SUBLANES, :])
        shifted = jnp.where(first, jnp.broadcast_to(prev, u.shape), pltpu.roll(u, shift=1, axis=0))
        return u + (shifted - u) * mu_ref[:, lo:hi]

    w = width
    r = mixed(ur_ref, ar_ref, 0, w)
    k = mixed(uk_ref, ak_ref, w, 2 * w)
    v = mixed(uv_ref, av_ref, 2 * w, 3 * w)
    lora = mixed(ul_ref, al_ref, 3 * w, 3 * w + 2 * LORA_PAD + G_LORA)
    w_l = lora[:, 0:LORA_PAD]
    a_l = lora[:, LORA_PAD:2 * LORA_PAD]
    g_l = lora[:, 2 * LORA_PAD:]

    z = w0_ref[...] + _dot(jnp.tanh(w_l).astype(BF16), w2_ref[...])
    nz = -z
    softplus = jnp.maximum(nz, 0.0) + jnp.log(1.0 + jnp.exp(-jnp.abs(nz)))
    wdec = -softplus - 0.5
    a = _sigmoid(a0_ref[...] + _dot(a_l.astype(BF16), a2_ref[...]))
    g = _dot(_sigmoid(g_l).astype(BF16), g2_ref[...])

    r_out[...] = r
    lw_out[...] = -jnp.exp(wdec)
    k_out[...] = k * (1.0 + (a - 1.0) * ka_ref[...])
    v_out[...] = v
    kk_out[...] = k * kk_ref[...]
    a_out[...] = a
    g_out[...] = g


def _rwkv_prep(ua, ub, row0, rows, prev, mu, w0, w2, a0, a2, g2, k_k, k_a, *, tm, width, r_col, chained):
    nblk = rows // tm
    rb0 = row0 // tm
    cb = r_col // width
    lw = 2 * LORA_PAD + G_LORA
    lb = 2 * width // lw
    rw_w = prev.shape[2]
    vec = lambda n: pl.BlockSpec((1, n), lambda i: (0, 0))
    out = jax.ShapeDtypeStruct((rows, width), F32)
    tiles = tm // SUBLANES
    above = lambda i: jnp.maximum((rb0 + i) * tiles - 1, 0)
    blk = lambda wd, c: pl.BlockSpec((tm, wd), lambda i: (rb0 + i, c))
    abv = lambda wd, c: pl.BlockSpec((SUBLANES, wd), lambda i: (above(i), c))
    body = functools.partial(_rwkv_prep_body, tm=tm, width=width, chained=chained)
    return pl.pallas_call(
        body,
        grid=(nblk,),
        in_specs=[blk(width, cb), blk(width, 0), blk(width, 1), blk(lw, lb),
                  abv(width, cb), abv(width, 0), abv(width, 1), abv(lw, lb),
                  pl.BlockSpec((None, 1, rw_w), (lambda i: (0, 0, 0)) if chained else (lambda i: (i, 0, 0))),
                  vec(rw_w), vec(width),
                  pl.BlockSpec((LORA_PAD, width), lambda i: (0, 0)),
                  vec(width),
                  pl.BlockSpec((LORA_PAD, width), lambda i: (0, 0)),
                  pl.BlockSpec((G_LORA, width), lambda i: (0, 0)),
                  vec(width), vec(width)],
        out_specs=[pl.BlockSpec((tm, width), lambda i: (i, 0))] * 7,
        out_shape=[out] * 7,
        compiler_params=_cparams(("parallel",)),
        name="rwkv_prep",
    )(ua, ub, ub, ub, ua, ub, ub, ub, prev, mu, w0, w2, a0, a2, g2, k_k, k_a)


def _wkv_pair(r, lw, k, v, kk, a, g, st, lnw, lnb, rk, *, chunk):
    C = chunk
    C2 = 2 * C

    lane = lax.broadcasted_iota(jnp.int32, (C2, LANES), 1)
    rowi = lax.broadcasted_iota(jnp.int32, (C2, LANES), 0)
    own = (lane >= HS_R) == (rowi >= C)

    def twice(x):
        return jnp.concatenate([x, x], axis=0)

    def stack(x):
        return jnp.where(own, twice(x), 0.0)

    tr = lax.broadcasted_iota(jnp.int32, (C, C), 0)
    tc = lax.broadcasted_iota(jnp.int32, (C, C), 1)
    tri = (tr >= tc).astype(BF16)
    lw_hi = lw.astype(BF16)
    rem = lw - lw_hi.astype(F32)
    lw_mid = rem.astype(BF16)
    lw_lo = (rem - lw_mid.astype(F32)).astype(BF16)
    G = _dot(tri, lw_hi) + _dot(tri, lw_mid) + _dot(tri, lw_lo)
    yield None
    GC = G[C - 1:C, :]
    e_prev = jnp.exp(G - lw)
    e_neg = jnp.exp(-G)
    e_pos = jnp.exp(G)
    e_rest = jnp.exp(GC - G)
    e_all = jnp.exp(GC)

    kk2 = stack(kk)
    nrm = jnp.sqrt(jnp.sum(kk2 * kk2, axis=1, keepdims=True))
    kk2 = kk2 / jnp.maximum(nrm, 1e-12)

    A2 = (kk2 * twice(-e_prev)).astype(BF16)
    B2 = (kk2 * twice(a * e_neg)).astype(BF16)
    K2 = stack(k * e_neg).astype(BF16)
    R2 = stack(r * e_pos).astype(BF16)
    Bt2 = (kk2 * twice(a * e_rest)).astype(BF16)
    Kt2 = stack(k * e_rest).astype(BF16)
    V2f = stack(v)
    V2 = V2f.astype(BF16)

    AR = jnp.concatenate([A2, R2], axis=0)
    mr = lax.broadcasted_iota(jnp.int32, (2 * C2, 2 * C2), 0)
    mc = lax.broadcasted_iota(jnp.int32, (2 * C2, 2 * C2), 1)
    tr_r = mr % C2
    tr_c = mc % C2
    keep = tr_r + jnp.where(mr < C2, 0, 1) > tr_c
    m_all = jnp.where(keep, _dot_nt(AR, jnp.concatenate([B2, K2], axis=0)), 0.0)
    m_ab = m_all[:C2, :C2]
    m_rb = m_all[C2:, :C2].astype(BF16)
    m_xk = m_all[:, C2:].astype(BF16)
    eye = (lax.broadcasted_iota(jnp.int32, (C2, C2), 0) == lax.broadcasted_iota(jnp.int32, (C2, C2), 1)).astype(F32)
    yield None

    t_inv = eye + m_ab
    pb = m_ab.astype(BF16)
    p = _dot(pb, pb)
    yield None
    n = 2
    while 2 * n < C:
        pb = p.astype(BF16)
        both = _dot(jnp.concatenate([pb, t_inv.astype(BF16)], axis=0), pb)
        p = both[:C2]
        t_inv = t_inv + both[C2:]
        n *= 2
        yield None
    t_inv = t_inv + _dot(t_inv.astype(BF16), p.astype(BF16))
    yield None

    stb = st.astype(BF16)
    xy = _dot(AR, stb) + _dot(m_xk, V2)
    yield None
    u2 = _dot(t_inv.astype(BF16), xy[:C2].astype(BF16))
    yield None
    u2b = u2.astype(BF16)
    y2 = xy[C2:] + _dot(m_rb, u2b)

    decay_col = jnp.broadcast_to(e_all, (LANES, LANES)).T
    st_new = decay_col * st + _dot_tn(jnp.concatenate([Bt2, Kt2], axis=0), jnp.concatenate([u2b, V2], axis=0))
    yield None

    inv_hs = 1.0 / HS_R
    mean = jnp.sum(y2, axis=1, keepdims=True) * inv_hs
    d = jnp.where(own, y2 - mean, 0.0)
    var = jnp.sum(d * d, axis=1, keepdims=True) * inv_hs
    yn2 = d * lax.rsqrt(var + GN_EPS)
    bonus2 = jnp.sum(stack(r * k * rk), axis=1, keepdims=True) * V2f
    yn = yn2[:C] + yn2[C:]
    bonus = bonus2[:C] + bonus2[C:]
    yield (yn * lnw + lnb + bonus) * g, st_new


def _wkv_body(r_ref, lw_ref, k_ref, v_ref, kk_ref, a_ref, g_ref, s0_ref, lnw_ref, lnb_ref, rk_ref,
              o_ref, st_ref, st_scr, *, chunk, pairs):
    c = pl.program_id(2)

    @pl.when(c == 0)
    def _():
        st_scr[...] = s0_ref[...]

    lanes = [slice(pp * LANES, (pp + 1) * LANES) for pp in range(pairs)]
    gens = [_wkv_pair(r_ref[:, sl], lw_ref[:, sl], k_ref[:, sl], v_ref[:, sl], kk_ref[:, sl],
                      a_ref[:, sl], g_ref[:, sl], st_scr[pp], lnw_ref[:, sl], lnb_ref[:, sl],
                      rk_ref[:, sl], chunk=chunk) for pp, sl in enumerate(lanes)]
    results = [None] * pairs
    while any(res is None for res in results):
        results = [next(gen) for gen in gens]
    for pp, (o, st_new) in enumerate(results):
        st_scr[pp] = st_new
        o_ref[:, lanes[pp]] = o.astype(o_ref.dtype)

    @pl.when(c == pl.num_programs(2) - 1)
    def _():
        st_ref[...] = st_scr[...]


def _wkv(r, lw, k, v, kk, a, g, s0, lnw, lnb, rk, *, seqs, chunk, pairs):
    rows, width = r.shape
    t = rows // seqs
    nchunk = t // chunk
    npair = width // LANES
    wblk = pairs * LANES
    blk = pl.BlockSpec((chunk, wblk), lambda b, p, c: (b * nchunk + c, p))
    vec = pl.BlockSpec((1, wblk), lambda b, p, c: (0, p))
    stspec = pl.BlockSpec((None, pairs, LANES, LANES), lambda b, p, c: (b, p, 0, 0))
    return pl.pallas_call(
        functools.partial(_wkv_body, chunk=chunk, pairs=pairs),
        grid=(seqs, npair // pairs, nchunk),
        in_specs=[blk] * 7 + [stspec, vec, vec, vec],
        out_specs=[blk, stspec],
        out_shape=[jax.ShapeDtypeStruct((rows, width), BF16),
                   jax.ShapeDtypeStruct((seqs, npair, LANES, LANES), F32)],
        scratch_shapes=[pltpu.VMEM((pairs, LANES, LANES), F32)],
        compiler_params=_cparams(("parallel", "parallel", "arbitrary")),
        name="wkv7_chunked",
    )(r, lw, k, v, kk, a, g, s0, lnw, lnb, rk)


def _two_maps(q):
    lane = lax.broadcasted_iota(jnp.int32, q.shape, 1)
    q = q * (ATTN_SCALE * LOG2E)
    return jnp.where(lane < DK, q, 0.0), jnp.where(lane >= DK, q, 0.0)


def _attn_prompt_body(qi_ref, ki_ref, lam_ref, q_ref, k_ref, v_ref, sw_ref, o_ref,
                      q2_scr, m_scr, l_scr, acc_scr, *, tq, post_scale):
    step = pl.program_id(1)
    qi = qi_ref[step]
    ki = ki_ref[step]

    @pl.when(ki == 0)
    def _():
        q1, q2 = _two_maps(q_ref[...])
        q2_scr[...] = jnp.concatenate([q1, q2], axis=0).astype(BF16)
        m_scr[...] = jnp.full_like(m_scr, NEG_INF)
        l_scr[...] = jnp.zeros_like(l_scr)
        acc_scr[...] = jnp.zeros_like(acc_scr)

    def update(diagonal):
        for kb in range(tq // ATTN_KEY_SUB):
            k0 = kb * ATTN_KEY_SUB
            kblk = k_ref[k0:k0 + ATTN_KEY_SUB, :].astype(BF16)
            vtblk = v_ref[k0:k0 + ATTN_KEY_SUB, :].T.astype(BF16)
            for c in range(2 * tq // ATTN_Q_COLS):
                q0 = (c * ATTN_Q_COLS) % tq
                if diagonal and k0 > q0 + ATTN_Q_COLS - 1:
                    continue
                cs = slice(c * ATTN_Q_COLS, (c + 1) * ATTN_Q_COLS)
                st = _dot_nt(kblk, q2_scr[cs, :])
                if diagonal and k0 + ATTN_KEY_SUB - 1 > q0:
                    krow = k0 + lax.broadcasted_iota(jnp.int32, st.shape, 0)
                    qcol = q0 + lax.broadcasted_iota(jnp.int32, st.shape, 1)
                    st = jnp.where(krow <= qcol, st, NEG_INF)
                m_old = m_scr[:, cs]
                m_new = jnp.maximum(m_old, jnp.max(st, axis=0, keepdims=True))
                alpha = jnp.exp2(m_old - m_new)
                p = jnp.exp2(st - m_new)
                l_scr[:, cs] = alpha * l_scr[:, cs] + jnp.sum(p, axis=0, keepdims=True)
                acc_scr[:, cs] = alpha * acc_scr[:, cs] + _dot(vtblk, p.astype(BF16))
                m_scr[:, cs] = m_new

    @pl.when(ki < qi)
    def _():
        update(False)

    @pl.when(ki == qi)
    def _():
        update(True)
        acc = acc_scr[...]
        l = l_scr[...]
        ot = acc[:, :tq] / l[:, :tq] - lam_ref[:, 0:1] * (acc[:, tq:] / l[:, tq:])
        ms = jnp.mean(ot * ot, axis=0, keepdims=True)
        on = ot * lax.rsqrt(ms + SUBLN_EPS)
        o_ref[...] = (on.T * sw_ref[...] * post_scale).astype(o_ref.dtype)


def _attn_prompt(u, lam, subw, *, t, heads, tq, post_scale):
    nq = t // tq
    pairs = [(i, j) for i in range(nq) for j in range(i + 1)]
    qi_tab = jnp.array([p[0] for p in pairs], jnp.int32)
    ki_tab = jnp.array([p[1] for p in pairs], jnp.int32)
    vec = pl.BlockSpec((1, LANES), lambda h, s, qi, ki: (0, 0))
    grid_spec = pltpu.PrefetchScalarGridSpec(
        num_scalar_prefetch=2,
        grid=(heads, len(pairs)),
        in_specs=[vec,
                  pl.BlockSpec((tq, LANES), lambda h, s, qi, ki: (qi[s], h)),
                  pl.BlockSpec((tq, LANES), lambda h, s, qi, ki: (ki[s], heads + h)),
                  pl.BlockSpec((tq, LANES), lambda h, s, qi, ki: (ki[s], 2 * heads + h)),
                  vec],
        out_specs=pl.BlockSpec((tq, LANES), lambda h, s, qi, ki: (qi[s], h)),
        scratch_shapes=[pltpu.VMEM((2 * tq, LANES), BF16),
                        pltpu.VMEM((1, 2 * tq), F32),
                        pltpu.VMEM((1, 2 * tq), F32),
                        pltpu.VMEM((LANES, 2 * tq), F32)],
    )
    return pl.pallas_call(
        functools.partial(_attn_prompt_body, tq=tq, post_scale=post_scale),
        grid_spec=grid_spec,
        out_shape=jax.ShapeDtypeStruct((t, heads * LANES), BF16),
        compiler_params=_cparams(("parallel", "arbitrary")),
        name="diff_attn_prompt",
    )(qi_tab, ki_tab, lam, u, u, u, subw)


def _attn_sample_body(pt_ref, lam_ref, q_ref, *refs, heads, tn, post_scale, pages):
    groups = heads // HEAD_TILE
    n_in = pages * groups
    kc_refs = [refs[k * groups:(k + 1) * groups] for k in range(pages)]
    vc_refs = [refs[n_in + k * groups:n_in + (k + 1) * groups] for k in range(pages)]
    kn_refs = refs[2 * n_in:2 * n_in + groups]
    vn_refs = refs[2 * n_in + groups:2 * n_in + 2 * groups]
    bias_ref, biasn_ref, sw_ref, o_ref, q2_scr, m_scr, l_scr, acc_scr = refs[2 * n_in + 2 * groups:]
    j = pl.program_id(1)
    n_steps = pl.num_programs(1) - 1
    half = HEAD_TILE * tn

    @pl.when(j == 0)
    def _():
        for h in range(heads):
            g, hh = divmod(h, HEAD_TILE)
            q1, q2 = _two_maps(q_ref[:, h * LANES:(h + 1) * LANES])
            q2_scr[g, hh * tn:(hh + 1) * tn] = q1
            q2_scr[g, half + hh * tn:half + (hh + 1) * tn] = q2
        m_scr[...] = jnp.full_like(m_scr, NEG_INF)
        l_scr[...] = jnp.zeros_like(l_scr)
        acc_scr[...] = jnp.zeros_like(acc_scr)

    def attend(k_pages, v_pages, b_ref):
        bias = b_ref[...]
        flat = lambda ref: ref[...].reshape(PAGE * HEAD_TILE, LANES).astype(BF16)
        for g in range(groups):
            q2 = q2_scr[g].astype(BF16)
            ss = [_dot_nt(q2, flat(kp[g])) + bias for kp in k_pages]
            m_old = m_scr[g]
            m_new = m_old
            for s in ss:
                m_new = jnp.maximum(m_new, jnp.max(s, axis=1, keepdims=True))
            alpha = jnp.exp2(m_old - m_new)
            l_new = alpha * l_scr[g]
            acc_new = alpha * acc_scr[g]
            for s, vp in zip(ss, v_pages):
                p = jnp.exp2(s - m_new)
                l_new = l_new + jnp.sum(p, axis=1, keepdims=True)
                acc_new = acc_new + _dot(p.astype(BF16), flat(vp[g]))
            l_scr[g] = l_new
            acc_scr[g] = acc_new
            m_scr[g] = m_new

    @pl.when(j < n_steps)
    def _():
        attend(kc_refs, vc_refs, bias_ref)

    @pl.when(j == n_steps)
    def _():
        attend([kn_refs], [vn_refs], biasn_ref)
        for g in range(groups):
            acc = acc_scr[g]
            l = l_scr[g]
            o = acc[:half] / l[:half] - lam_ref[...] * (acc[half:] / l[half:])
            ms = jnp.mean(o * o, axis=-1, keepdims=True)
            o = o * lax.rsqrt(ms + SUBLN_EPS) * sw_ref[...] * post_scale
            for hh in range(HEAD_TILE):
                h = g * HEAD_TILE + hh
                o_ref[:, h * LANES:(h + 1) * LANES] = o[hh * tn:(hh + 1) * tn]


def _sample_bias(tn, causal):
    r = jnp.arange(2 * HEAD_TILE * tn)[:, None]
    c = jnp.arange(PAGE * HEAD_TILE)[None, :]
    ok = (c % HEAD_TILE) == ((r % (HEAD_TILE * tn)) // tn)
    if causal:
        ok = ok & ((c // HEAD_TILE) <= (r % tn))
    return jnp.where(ok, 0.0, NEG_INF).astype(F32)


def _attn_sample(q, cache_k, cache_v, k_new, v_new, page_table, lam, subw, *, layer, post_scale):
    b, n_pages = page_table.shape
    n_layers, n_pool, _, heads, _ = cache_k.shape
    groups = heads // HEAD_TILE
    tn = q.shape[0] // b
    rows = 2 * HEAD_TILE * tn
    pt = page_table.reshape(-1)
    pages = _pick_tile(n_pages, (SAMPLE_PAGES, 1))
    n_steps = n_pages // pages
    tiled = lambda c: c.reshape(c.shape[:-2] + (groups, HEAD_TILE, LANES))

    def page_spec(k, g):
        def page_map(i, j, pt_ref):
            return (layer, pt_ref[i * n_pages + jnp.minimum(j, n_steps - 1) * pages + k], 0, g, 0, 0)
        return pl.BlockSpec((None, None, PAGE, None, HEAD_TILE, LANES), page_map)

    def new_spec(g):
        return pl.BlockSpec((None, PAGE, None, HEAD_TILE, LANES), lambda i, j, pt_ref: (i, 0, g, 0, 0))

    page_specs = [page_spec(k, g) for k in range(pages) for g in range(groups)]
    new_specs = [new_spec(g) for g in range(groups)]
    vec = pl.BlockSpec((1, LANES), lambda i, j, pt_ref: (0, 0))
    bias_spec = pl.BlockSpec((rows, PAGE * HEAD_TILE), lambda i, j, pt_ref: (0, 0))
    grid_spec = pltpu.PrefetchScalarGridSpec(
        num_scalar_prefetch=1,
        grid=(b, n_steps + 1),
        in_specs=[vec,
                  pl.BlockSpec((tn, heads * LANES), lambda i, j, pt_ref: (i, 0)),
                  *page_specs, *page_specs, *new_specs, *new_specs, bias_spec, bias_spec, vec],
        out_specs=pl.BlockSpec((tn, heads * LANES), lambda i, j, pt_ref: (i, 0)),
        scratch_shapes=[pltpu.VMEM((groups, rows, LANES), F32),
                        pltpu.VMEM((groups, rows, 1), F32),
                        pltpu.VMEM((groups, rows, 1), F32),
                        pltpu.VMEM((groups, rows, LANES), F32)],
    )
    n_in = pages * groups
    return pl.pallas_call(
        functools.partial(_attn_sample_body, heads=heads, tn=tn, post_scale=post_scale, pages=pages),
        grid_spec=grid_spec,
        out_shape=jax.ShapeDtypeStruct((b * tn, heads * LANES), F32),
        compiler_params=_cparams(("parallel", "arbitrary")),
        name="diff_attn_sample",
    )(pt, lam, q, *([tiled(cache_k)] * n_in), *([tiled(cache_v)] * n_in),
      *([tiled(k_new)] * groups), *([tiled(v_new)] * groups),
      _sample_bias(tn, False), _sample_bias(tn, True), subw)


def _pick_tile(n, candidates):
    for c in candidates:
        if n % c == 0:
            return c
    raise ValueError(f"no tile for {n}")


def _pair_states(s):
    b, h, _, _ = s.shape
    st = jnp.swapaxes(s, -1, -2).reshape(b, h // 2, 2, HS_R, HS_R)
    z = jnp.zeros_like(st[:, :, 0])
    top = jnp.concatenate([st[:, :, 0], z], axis=-1)
    bot = jnp.concatenate([z, st[:, :, 1]], axis=-1)
    return jnp.concatenate([top, bot], axis=-2)


def _head_states(sp):
    b, hp, _, _ = sp.shape
    s0 = sp[:, :, :HS_R, :HS_R]
    s1 = sp[:, :, HS_R:, HS_R:]
    st = jnp.stack([s0, s1], axis=2).reshape(b, 2 * hp, HS_R, HS_R)
    return jnp.swapaxes(st, -1, -2)


def _layer(l, depth, xp, xs, cache_k, cache_v, state_wkv, state_shift, page_table,
           w_in, mu_shift, w_decay0, w_decay2, a0, a2, g2, k_k, k_a, r_k, lnx_w, lnx_b,
           lambda_q1, lambda_k1, lambda_q2, lambda_k2, subln_w,
           w_pa, w_pb, w_gate, b_gate, w_o, ln1_g, ln1_b,
           w_ffn_gate, w_ffn_up, w_ffn_down, ln2_g, ln2_b):
    bp, sp, d = xp.shape
    bs, ts, _ = xs.shape
    assert bp == 1
    heads = cache_k.shape[3]
    att_w = heads * LANES
    rw_w = w_decay0.shape[1]
    h_r = rw_w // HS_R
    rw_in = 3 * rw_w + W_LORA + A_LORA + G_LORA
    assert w_in.shape[2] == 3 * att_w + rw_in and att_w == rw_w
    alpha = (2 * depth) ** 0.25
    lam_init = 0.8 - 0.6 * math.exp(-0.3 * l)
    post_scale = 1.0 - lam_init

    lam = (jnp.exp(jnp.sum(lambda_q1[l] * lambda_k1[l])) - jnp.exp(jnp.sum(lambda_q2[l] * lambda_k2[l]))
           + lam_init)
    lam = jnp.broadcast_to(lam.astype(F32), (1, LANES))

    n_p = bp * sp
    n_s = bs * ts
    m = n_p + n_s
    xp2 = xp.reshape(n_p, d)
    xs2 = xs.reshape(n_s, d)
    xb = jnp.concatenate([xp2.astype(BF16), xs2.astype(BF16)], axis=0)

    o_rw = 3 * att_w
    c_wl = o_rw + rw_w
    c_k = c_wl + W_LORA
    c_al = c_k + 2 * rw_w
    c_gl = c_al + A_LORA
    lora_w = 2 * LORA_PAD + G_LORA
    rw_cols = 3 * rw_w + lora_w

    def rest_cols(t, off, axis=1):
        take = lambda a, b: lax.slice_in_dim(t, a - off, b - off, axis=axis)
        zshape = list(t.shape)
        zshape[axis] = LORA_PAD - W_LORA
        z = jnp.zeros(zshape, t.dtype)
        return jnp.concatenate([take(c_k, c_al), take(c_wl, c_k), z, take(c_al, c_gl), z,
                                take(c_gl, c_gl + G_LORA)], axis=axis)

    def prep_cols(t):
        return jnp.concatenate([t[:, :rw_w], rest_cols(t, o_rw)], axis=1)

    wt = w_in[l].T
    wt_rest = rest_cols(wt, 0, axis=0).astype(BF16)
    tm = _pick_tile(m, (768, 512, 384, 256, 128, 64, 32, 16, 8))
    tm_big = _pick_tile(m, (1408, 768, 512, 384, 256, 128, 64, 32, 16, 8))
    ua = _matmul_nt(xb, wt, n=c_wl, tm=tm_big, tn=_pick_tile(c_wl, (256, 128)), out_dtype=F32,
                    name="in_proj_qkvr")
    ub = _matmul_nt(xb, wt_rest, n=wt_rest.shape[0], tm=tm, tn=_pick_tile(wt_rest.shape[0], (512, 256, 128)),
                    out_dtype=F32, name="in_proj_rest")

    mu = prep_cols(mu_shift[l][None, :])
    pad_rows = lambda t: jnp.concatenate([t, jnp.zeros((LORA_PAD - t.shape[0], t.shape[1]), t.dtype)], axis=0)
    w2 = pad_rows(w_decay2[l]).astype(BF16)
    a2b = pad_rows(a2[l]).astype(BF16)
    g2b = g2[l].astype(BF16)
    row = lambda t: t[l][None, :]
    prep_args = (mu, row(w_decay0), w2, row(a0), a2b, g2b, row(k_k), row(k_a))

    tp = _pick_tile(sp, (128, 64, 32, 16, 8))
    prev_p = jnp.zeros((bp, 1, rw_cols), F32)
    prep_p = _rwkv_prep(ua, ub, 0, sp, prev_p, *prep_args, tm=tp, width=rw_w, r_col=o_rw, chained=True)

    prev_s = prep_cols(state_shift[l]).reshape(bs, 1, rw_cols)
    prep_s = _rwkv_prep(ua, ub, n_p, n_s, prev_s, *prep_args, tm=ts, width=rw_w, r_col=o_rw, chained=False)

    wkv_params = (row(lnx_w), row(lnx_b), r_k[l].reshape(1, rw_w))
    s0_p = jnp.zeros((bp, h_r // 2, LANES, LANES), F32)
    pairs = _pick_tile(h_r // 2, (WKV_PAIRS, 2, 1))
    o_rw_p, st_p = _wkv(*prep_p, s0_p, *wkv_params, seqs=bp, chunk=WKV_CHUNK, pairs=pairs)

    t_pad = -(-ts // WKV_CHUNK) * WKV_CHUNK
    pad_t = lambda t: jnp.pad(t.reshape(bs, ts, rw_w), ((0, 0), (0, t_pad - ts), (0, 0))).reshape(bs * t_pad, rw_w)
    s0_s = _pair_states(state_wkv[l].astype(F32))
    o_rw_s, st_s = _wkv(*[pad_t(t) for t in prep_s], s0_s, *wkv_params, seqs=bs, chunk=WKV_CHUNK, pairs=pairs)
    o_rw_s = o_rw_s.reshape(bs, t_pad, rw_w)[:, :ts].reshape(n_s, rw_w)
    o_rwkv = jnp.concatenate([o_rw_p, o_rw_s], axis=0)

    new_wkv_p = _head_states(st_p).astype(state_wkv.dtype)
    new_wkv_s = _head_states(st_s).astype(state_wkv.dtype)

    def shift_row(rows):
        ta = ua[rows]
        tb = ub[rows]
        w_ = rw_w
        return jnp.concatenate([ta[:, o_rw:], tb[:, 2 * w_:2 * w_ + W_LORA], tb[:, :2 * w_],
                                tb[:, 2 * w_ + LORA_PAD:2 * w_ + LORA_PAD + A_LORA],
                                tb[:, 2 * w_ + 2 * LORA_PAD:]], axis=1)

    new_shift_p = shift_row(jnp.array([sp - 1]))
    new_shift_s = shift_row(n_p + ts - 1 + ts * jnp.arange(bs))

    subw = subln_w[l][None, :]
    tq = _pick_tile(sp, (1024, 512, 256, 128))
    o_att_p = _attn_prompt(ua, lam, subw, t=sp, heads=heads, tq=tq, post_scale=post_scale)

    q_s = lax.slice(ua, (n_p, 0), (m, att_w))
    k_s = lax.slice(ua, (n_p, att_w), (m, 2 * att_w)).reshape(bs, ts, heads, DV)
    v_s = lax.slice(ua, (n_p, 2 * att_w), (m, 3 * att_w)).reshape(bs, ts, heads, DV)
    pad_new = lambda t: jnp.pad(t, ((0, 0), (0, PAGE - ts), (0, 0), (0, 0)))
    o_att_s = _attn_sample(q_s, cache_k, cache_v, pad_new(k_s), pad_new(v_s), page_table, lam, subw,
                           layer=l, post_scale=post_scale)
    o_att = jnp.concatenate([o_att_p, o_att_s.astype(BF16)], axis=0)

    wg = w_gate[l].astype(BF16)
    bg = b_gate[l][None, :]
    mixin = _gated_merge(xb, o_att, o_rwkv, wg[:, :d], wg[:, d:], bg[:, :d], bg[:, d:],
                         w_pa[l].astype(BF16), w_pb[l].astype(BF16), tm=tm, tn=_pick_tile(d, (256, 128)))
    mix = _matmul(mixin, w_o[l], tm=tm, tn=_pick_tile(d, (512, 256, 128)), out_dtype=F32, name="out_proj")
    tl = _pick_tile(math.gcd(n_p, n_s), (256, 128, 64, 32, 16, 8))
    x1, x1b = _add_norm(mix, [(xp2, 0), (xs2, n_p)], row(ln1_g), row(ln1_b), alpha=alpha, tm=tl,
                        outs=[(m, 0, F32), (m, 0, BF16)])

    d_ff = w_ffn_gate.shape[2]
    hmid = _swiglu(x1b, w_ffn_gate[l], w_ffn_up[l], tm=tm_big, tn=_pick_tile(d_ff, (256, 128)))
    down = _matmul(hmid, w_ffn_down[l].astype(BF16), tm=tm, tn=_pick_tile(d, (256, 128)), out_dtype=F32,
                   name="ffn_down")
    y_p, y_s = _add_norm(down, [(x1, 0)], row(ln2_g), row(ln2_b), alpha=alpha, tm=tl,
                         outs=[(n_p, 0, F32), (n_s, n_p, F32)])

    k_all = lax.slice(ua, (0, att_w), (m, 2 * att_w))
    v_all = lax.slice(ua, (0, 2 * att_w), (m, 3 * att_w))
    outs = dict(
        xp=y_p.reshape(bp, sp, d), xs=y_s.reshape(bs, ts, d),
        kp=k_all[:n_p].reshape(bp, sp, heads, DV), vp=v_all[:n_p].reshape(bp, sp, heads, DV),
        wp=new_wkv_p, sp=new_shift_p,
        ks=k_s, vs=v_s, ws=new_wkv_s, ss=new_shift_s)
    return outs


def kernel(x_prompt, x_sample, cache_k, cache_v, state_wkv, state_shift, page_table, w_in, mu_shift, w_decay0, w_decay2, a0, a2, g2, k_k, k_a, r_k, lnx_w, lnx_b, lambda_q1, lambda_k1, lambda_q2, lambda_k2, subln_w, w_pa, w_pb, w_gate, b_gate, w_o, ln1_g, ln1_b, w_ffn_gate, w_ffn_up, w_ffn_down, ln2_g, ln2_b):
    depth = w_in.shape[0]
    xp, xs = x_prompt, x_sample
    acc = {n: [] for n in ("kp", "vp", "wp", "sp", "ks", "vs", "ws", "ss")}
    for l in range(depth):
        o = _layer(l, depth, xp, xs, cache_k, cache_v, state_wkv, state_shift, page_table,
                   w_in, mu_shift, w_decay0, w_decay2, a0, a2, g2, k_k, k_a, r_k, lnx_w, lnx_b,
                   lambda_q1, lambda_k1, lambda_q2, lambda_k2, subln_w,
                   w_pa, w_pb, w_gate, b_gate, w_o, ln1_g, ln1_b,
                   w_ffn_gate, w_ffn_up, w_ffn_down, ln2_g, ln2_b)
        xp, xs = o["xp"], o["xs"]
        for n in acc:
            acc[n].append(o[n])
    st = lambda n: jnp.stack(acc[n])
    return (xp, xs, st("kp"), st("vp"), st("wp"), st("sp"), st("ks"), st("vs"), st("ws"), st("ss"))
```

```python
import functools
import math

import jax
import jax.numpy as jnp
from jax import lax
from jax.experimental import pallas as pl
from jax.experimental.pallas import tpu as pltpu

F32 = jnp.float32
BF16 = jnp.bfloat16

LANES = 128
SUBLANES = 8
VMEM_LIMIT = 56 * 1024 * 1024

DK = 64
DV = 128
HS_R = 64
W_LORA = 96
A_LORA = 96
G_LORA = 256
LORA_PAD = 128
ATTN_SCALE = DK ** -0.5
LOG2E = math.log2(math.e)
ATTN_KEY_SUB = 256
ATTN_Q_COLS = 128
SAMPLE_PAGES = 4
SUBLN_EPS = 1e-5
NEG_INF = -1e30
GN_EPS = 64e-5
LN_EPS = 1e-5
PAGE = 128
HEAD_TILE = 8
WKV_CHUNK = 64
WKV_PAIRS = 8


def _cparams(sem):
    return pltpu.CompilerParams(dimension_semantics=sem, vmem_limit_bytes=VMEM_LIMIT)


def _dot(a, b):
    return jnp.dot(a, b, preferred_element_type=F32)


def _dot_nt(a, b):
    return lax.dot_general(a, b, (((1,), (1,)), ((), ())), preferred_element_type=F32)


def _dot_tn(a, b):
    return lax.dot_general(a, b, (((0,), (0,)), ((), ())), preferred_element_type=F32)


def _sigmoid(x):
    return 1.0 / (1.0 + jnp.exp(-x))


def _mm_body(x_ref, w_ref, o_ref):
    o_ref[...] = _dot(x_ref[...], w_ref[...].astype(BF16)).astype(o_ref.dtype)


def _matmul(x, w, *, tm, tn, out_dtype, name, n=None):
    m, k = x.shape
    n = w.shape[1] if n is None else n
    return pl.pallas_call(
        _mm_body,
        grid=(m // tm, n // tn),
        in_specs=[pl.BlockSpec((tm, k), lambda i, j: (i, 0)),
                  pl.BlockSpec((k, tn), lambda i, j: (0, j))],
        out_specs=pl.BlockSpec((tm, tn), lambda i, j: (i, j)),
        out_shape=jax.ShapeDtypeStruct((m, n), out_dtype),
        compiler_params=_cparams(("parallel", "parallel")),
        name=name,
    )(x, w)


def _mm_nt_body(x_ref, wt_ref, o_ref):
    o_ref[...] = _dot_nt(x_ref[...], wt_ref[...].astype(BF16)).astype(o_ref.dtype)


def _matmul_nt(x, wt, *, n, tm, tn, out_dtype, name):
    m, k = x.shape
    return pl.pallas_call(
        _mm_nt_body,
        grid=(m // tm, n // tn),
        in_specs=[pl.BlockSpec((tm, k), lambda i, j: (i, 0)),
                  pl.BlockSpec((tn, k), lambda i, j: (j, 0))],
        out_specs=pl.BlockSpec((tm, tn), lambda i, j: (i, j)),
        out_shape=jax.ShapeDtypeStruct((m, n), out_dtype),
        compiler_params=_cparams(("parallel", "parallel")),
        name=name,
    )(x, wt)


def _gated_merge_body(x_ref, oa_ref, ob_ref, wga_ref, wgb_ref, ba_ref, bb_ref, wpa_ref, wpb_ref, o_ref):
    x = x_ref[...]
    ga = _sigmoid(_dot(x, wga_ref[...]) + ba_ref[...])
    gb = _sigmoid(_dot(x, wgb_ref[...]) + bb_ref[...])
    o = ga * _dot(oa_ref[...], wpa_ref[...]) + gb * _dot(ob_ref[...], wpb_ref[...])
    o_ref[...] = o.astype(o_ref.dtype)


def _gated_merge(x, oa, ob, wg, bg, wpa, wpb, *, tm, tn):
    m, d = x.shape
    ka = oa.shape[1]
    kb = ob.shape[1]
    n = wpa.shape[1]
    nb = n // tn
    row = lambda w: pl.BlockSpec((tm, w), lambda i, j: (i, 0))
    col = lambda k, off: pl.BlockSpec((k, tn), lambda i, j: (0, j + off))
    return pl.pallas_call(
        _gated_merge_body,
        grid=(m // tm, nb),
        in_specs=[row(d), row(ka), row(kb), col(d, 0), col(d, nb), col(1, 0), col(1, nb), col(ka, 0), col(kb, 0)],
        out_specs=pl.BlockSpec((tm, tn), lambda i, j: (i, j)),
        out_shape=jax.ShapeDtypeStruct((m, n), BF16),
        compiler_params=_cparams(("parallel", "parallel")),
        name="gated_merge",
    )(x, oa, ob, wg, wg, bg, bg, wpa, wpb)


def _swiglu_body(x_ref, wg_ref, wu_ref, o_ref):
    x = x_ref[...]
    g = _dot(x, wg_ref[...].astype(BF16))
    u = _dot(x, wu_ref[...].astype(BF16))
    o_ref[...] = (g * _sigmoid(g) * u).astype(o_ref.dtype)


def _swiglu(x, wg, wu, *, tm, tn):
    m, k = x.shape
    n = wg.shape[1]
    return pl.pallas_call(
        _swiglu_body,
        grid=(m // tm, n // tn),
        in_specs=[pl.BlockSpec((tm, k), lambda i, j: (i, 0)),
                  pl.BlockSpec((k, tn), lambda i, j: (0, j)),
                  pl.BlockSpec((k, tn), lambda i, j: (0, j))],
        out_specs=pl.BlockSpec((tm, tn), lambda i, j: (i, j)),
        out_shape=jax.ShapeDtypeStruct((m, n), BF16),
        compiler_params=_cparams(("parallel", "parallel")),
        name="swiglu",
    )(x, wg, wu)


def _add_norm_body(*refs, alpha, res_first, out_first, out_blocks):
    n_res = len(res_first)
    x_ref = refs[0]
    res_refs = refs[1:1 + n_res]
    g_ref, b_ref = refs[1 + n_res:3 + n_res]
    out_refs = refs[3 + n_res:]
    i = pl.program_id(0)
    r = res_refs[0][...]
    for k in range(1, n_res):
        r = jnp.where(i >= res_first[k], res_refs[k][...], r)
    x = alpha * r + x_ref[...]
    mu = jnp.mean(x, axis=-1, keepdims=True)
    d = x - mu
    var = jnp.mean(d * d, axis=-1, keepdims=True)
    y = d * lax.rsqrt(var + LN_EPS) * g_ref[...] + b_ref[...]
    for o_ref, first, nblk in zip(out_refs, out_first, out_blocks):
        @pl.when((i >= first) & (i < first + nblk))
        def _():
            o_ref[...] = y.astype(o_ref.dtype)


def _add_norm(x, resids, g, b, *, alpha, tm, outs):
    m, d = x.shape
    row_spec = lambda first, rows: pl.BlockSpec(
        (tm, d), lambda i: (jnp.clip(i - first // tm, 0, rows // tm - 1), 0))
    vec = pl.BlockSpec((1, d), lambda i: (0, 0))
    body = functools.partial(
        _add_norm_body, alpha=alpha,
        res_first=tuple(first // tm for _, first in resids),
        out_first=tuple(first // tm for _, first, _ in outs),
        out_blocks=tuple(rows // tm for rows, _, _ in outs))
    return pl.pallas_call(
        body,
        grid=(m // tm,),
        in_specs=[row_spec(0, m)] + [row_spec(first, r.shape[0]) for r, first in resids] + [vec, vec],
        out_specs=[row_spec(first, rows) for rows, first, _ in outs],
        out_shape=[jax.ShapeDtypeStruct((rows, d), dt) for rows, _, dt in outs],
        compiler_params=_cparams(("arbitrary",)),
        name="add_norm",
    )(x, *[r for r, _ in resids], g, b)


def _rwkv_prep_body(ur_ref, uk_ref, uv_ref, ul_ref, ar_ref, ak_ref, av_ref, al_ref, prev_ref, mu_ref,
                    w0_ref, w2_ref, a0_ref, a2_ref, g2_ref, kk_ref, ka_ref,
                    r_out, lw_out, k_out, v_out, kk_out, a_out, g_out, *, tm, width, chained):
    first = lax.broadcasted_iota(jnp.int32, (tm, 1), 0) == 0
    start = pl.program_id(0) == 0

    def mixed(u_ref, above_ref, lo, hi):
        u = u_ref[...]
        prev = prev_ref[:, lo:hi]
        if chained:
            prev = jnp.where(start, prev, above_ref[SUBLANES - 1:SUBLANES, :])
        shifted = jnp.where(first, jnp.broadcast_to(prev, u.shape), pltpu.roll(u, shift=1, axis=0))
        return u + (shifted - u) * mu_ref[:, lo:hi]

    w = width
    r = mixed(ur_ref, ar_ref, 0, w)
    k = mixed(uk_ref, ak_ref, w, 2 * w)
    v = mixed(uv_ref, av_ref, 2 * w, 3 * w)
    lora = mixed(ul_ref, al_ref, 3 * w, 3 * w + 2 * LORA_PAD + G_LORA)
    w_l = lora[:, 0:LORA_PAD]
    a_l = lora[:, LORA_PAD:2 * LORA_PAD]
    g_l = lora[:, 2 * LORA_PAD:]

    z = w0_ref[...] + _dot(jnp.tanh(w_l).astype(BF16), w2_ref[...])
    nz = -z
    softplus = jnp.maximum(nz, 0.0) + jnp.log(1.0 + jnp.exp(-jnp.abs(nz)))
    wdec = -softplus - 0.5
    a = _sigmoid(a0_ref[...] + _dot(a_l.astype(BF16), a2_ref[...]))
    g = _dot(_sigmoid(g_l).astype(BF16), g2_ref[...])

    r_out[...] = r
    lw_out[...] = -jnp.exp(wdec)
    k_out[...] = k * (1.0 + (a - 1.0) * ka_ref[...])
    v_out[...] = v
    kk_out[...] = k * kk_ref[...]
    a_out[...] = a
    g_out[...] = g


def _rwkv_prep(ua, ub, row0, rows, prev, mu, w0, w2, a0, a2, g2, k_k, k_a, *, tm, width, r_col, chained):
    nblk = rows // tm
    rb0 = row0 // tm
    cb = r_col // width
    lw = 2 * LORA_PAD + G_LORA
    lb = 2 * width // lw
    rw_w = prev.shape[2]
    vec = lambda n: pl.BlockSpec((1, n), lambda i: (0, 0))
    out = jax.ShapeDtypeStruct((rows, width), F32)
    tiles = tm // SUBLANES
    above = lambda i: jnp.maximum((rb0 + i) * tiles - 1, 0)
    blk = lambda wd, c: pl.BlockSpec((tm, wd), lambda i: (rb0 + i, c))
    abv = lambda wd, c: pl.BlockSpec((SUBLANES, wd), lambda i: (above(i), c))
    body = functools.partial(_rwkv_prep_body, tm=tm, width=width, chained=chained)
    return pl.pallas_call(
        body,
        grid=(nblk,),
        in_specs=[blk(width, cb), blk(width, 0), blk(width, 1), blk(lw, lb),
                  abv(width, cb), abv(width, 0), abv(width, 1), abv(lw, lb),
                  pl.BlockSpec((None, 1, rw_w), (lambda i: (0, 0, 0)) if chained else (lambda i: (i, 0, 0))),
                  vec(rw_w), vec(width),
                  pl.BlockSpec((LORA_PAD, width), lambda i: (0, 0)),
                  vec(width),
                  pl.BlockSpec((LORA_PAD, width), lambda i: (0, 0)),
                  pl.BlockSpec((G_LORA, width), lambda i: (0, 0)),
                  vec(width), vec(width)],
        out_specs=[pl.BlockSpec((tm, width), lambda i: (i, 0))] * 7,
        out_shape=[out] * 7,
        compiler_params=_cparams(("parallel",)),
        name="rwkv_prep",
    )(ua, ub, ub, ub, ua, ub, ub, ub, prev, mu, w0, w2, a0, a2, g2, k_k, k_a)


def _wkv_pair(r, lw, k, v, kk, a, g, st, lnw, lnb, rk, *, chunk):
    C = chunk
    C2 = 2 * C

    lane = lax.broadcasted_iota(jnp.int32, (C2, LANES), 1)
    rowi = lax.broadcasted_iota(jnp.int32, (C2, LANES), 0)
    own = (lane >= HS_R) == (rowi >= C)

    def twice(x):
        return jnp.concatenate([x, x], axis=0)

    def stack(x):
        return jnp.where(own, twice(x), 0.0)

    tr = lax.broadcasted_iota(jnp.int32, (C, C), 0)
    tc = lax.broadcasted_iota(jnp.int32, (C, C), 1)
    tri = (tr >= tc).astype(BF16)
    lw_hi = lw.astype(BF16)
    rem = lw - lw_hi.astype(F32)
    lw_mid = rem.astype(BF16)
    lw_lo = (rem - lw_mid.astype(F32)).astype(BF16)
    G = _dot(tri, lw_hi) + _dot(tri, lw_mid) + _dot(tri, lw_lo)
    yield None
    GC = G[C - 1:C, :]
    e_prev = jnp.exp(G - lw)
    e_neg = jnp.exp(-G)
    e_pos = jnp.exp(G)
    e_rest = jnp.exp(GC - G)
    e_all = jnp.exp(GC)

    kk2 = stack(kk)
    nrm = jnp.sqrt(jnp.sum(kk2 * kk2, axis=1, keepdims=True))
    kk2 = kk2 / jnp.maximum(nrm, 1e-12)

    A2 = (kk2 * twice(-e_prev)).astype(BF16)
    B2 = (kk2 * twice(a * e_neg)).astype(BF16)
    K2 = stack(k * e_neg).astype(BF16)
    R2 = stack(r * e_pos).astype(BF16)
    Bt2 = (kk2 * twice(a * e_rest)).astype(BF16)
    Kt2 = stack(k * e_rest).astype(BF16)
    V2f = stack(v)
    V2 = V2f.astype(BF16)

    AR = jnp.concatenate([A2, R2], axis=0)
    mr = lax.broadcasted_iota(jnp.int32, (2 * C2, 2 * C2), 0)
    mc = lax.broadcasted_iota(jnp.int32, (2 * C2, 2 * C2), 1)
    tr_r = mr % C2
    tr_c = mc % C2
    keep = tr_r + jnp.where(mr < C2, 0, 1) > tr_c
    m_all = jnp.where(keep, _dot_nt(AR, jnp.concatenate([B2, K2], axis=0)), 0.0)
    m_ab = m_all[:C2, :C2]
    m_rb = m_all[C2:, :C2].astype(BF16)
    m_xk = m_all[:, C2:].astype(BF16)
    eye = (lax.broadcasted_iota(jnp.int32, (C2, C2), 0) == lax.broadcasted_iota(jnp.int32, (C2, C2), 1)).astype(F32)
    yield None

    t_inv = eye + m_ab
    pb = m_ab.astype(BF16)
    p = _dot(pb, pb)
    yield None
    n = 2
    while 2 * n < C:
        pb = p.astype(BF16)
        both = _dot(jnp.concatenate([pb, t_inv.astype(BF16)], axis=0), pb)
        p = both[:C2]
        t_inv = t_inv + both[C2:]
        n *= 2
        yield None
    t_inv = t_inv + _dot(t_inv.astype(BF16), p.astype(BF16))
    yield None

    stb = st.astype(BF16)
    xy = _dot(AR, stb) + _dot(m_xk, V2)
    yield None
    u2 = _dot(t_inv.astype(BF16), xy[:C2].astype(BF16))
    yield None
    u2b = u2.astype(BF16)
    y2 = xy[C2:] + _dot(m_rb, u2b)

    decay_col = jnp.broadcast_to(e_all, (LANES, LANES)).T
    st_new = decay_col * st + _dot_tn(jnp.concatenate([Bt2, Kt2], axis=0), jnp.concatenate([u2b, V2], axis=0))
    yield None

    inv_hs = 1.0 / HS_R
    mean = jnp.sum(y2, axis=1, keepdims=True) * inv_hs
    d = jnp.where(own, y2 - mean, 0.0)
    var = jnp.sum(d * d, axis=1, keepdims=True) * inv_hs
    yn2 = d * lax.rsqrt(var + GN_EPS)
    bonus2 = jnp.sum(stack(r * k * rk), axis=1, keepdims=True) * V2f
    yn = yn2[:C] + yn2[C:]
    bonus = bonus2[:C] + bonus2[C:]
    yield (yn * lnw + lnb + bonus) * g, st_new


def _wkv_body(r_ref, lw_ref, k_ref, v_ref, kk_ref, a_ref, g_ref, s0_ref, lnw_ref, lnb_ref, rk_ref,
              o_ref, st_ref, st_scr, *, chunk, pairs):
    c = pl.program_id(2)

    @pl.when(c == 0)
    def _():
        st_scr[...] = s0_ref[...]

    lanes = [slice(pp * LANES, (pp + 1) * LANES) for pp in range(pairs)]
    gens = [_wkv_pair(r_ref[:, sl], lw_ref[:, sl], k_ref[:, sl], v_ref[:, sl], kk_ref[:, sl],
                      a_ref[:, sl], g_ref[:, sl], st_scr[pp], lnw_ref[:, sl], lnb_ref[:, sl],
                      rk_ref[:, sl], chunk=chunk) for pp, sl in enumerate(lanes)]
    results = [None] * pairs
    while any(res is None for res in results):
        results = [next(gen) for gen in gens]
    for pp, (o, st_new) in enumerate(results):
        st_scr[pp] = st_new
        o_ref[:, lanes[pp]] = o.astype(o_ref.dtype)

    @pl.when(c == pl.num_programs(2) - 1)
    def _():
        st_ref[...] = st_scr[...]


def _wkv(r, lw, k, v, kk, a, g, s0, lnw, lnb, rk, *, seqs, chunk, pairs):
    rows, width = r.shape
    t = rows // seqs
    nchunk = t // chunk
    npair = width // LANES
    wblk = pairs * LANES
    blk = pl.BlockSpec((chunk, wblk), lambda b, p, c: (b * nchunk + c, p))
    vec = pl.BlockSpec((1, wblk), lambda b, p, c: (0, p))
    stspec = pl.BlockSpec((None, pairs, LANES, LANES), lambda b, p, c: (b, p, 0, 0))
    return pl.pallas_call(
        functools.partial(_wkv_body, chunk=chunk, pairs=pairs),
        grid=(seqs, npair // pairs, nchunk),
        in_specs=[blk] * 7 + [stspec, vec, vec, vec],
        out_specs=[blk, stspec],
        out_shape=[jax.ShapeDtypeStruct((rows, width), BF16),
                   jax.ShapeDtypeStruct((seqs, npair, LANES, LANES), F32)],
        scratch_shapes=[pltpu.VMEM((pairs, LANES, LANES), F32)],
        compiler_params=_cparams(("parallel", "parallel", "arbitrary")),
        name="wkv7_chunked",
    )(r, lw, k, v, kk, a, g, s0, lnw, lnb, rk)


def _two_maps(q):
    lane = lax.broadcasted_iota(jnp.int32, q.shape, 1)
    q = q * (ATTN_SCALE * LOG2E)
    return jnp.where(lane < DK, q, 0.0), jnp.where(lane >= DK, q, 0.0)


def _attn_prompt_body(qi_ref, ki_ref, lam_ref, q_ref, k_ref, v_ref, sw_ref, o_ref,
                      q2_scr, m_scr, l_scr, acc_scr, *, tq, post_scale):
    step = pl.program_id(1)
    qi = qi_ref[step]
    ki = ki_ref[step]

    @pl.when(ki == 0)
    def _():
        q1, q2 = _two_maps(q_ref[...])
        q2_scr[...] = jnp.concatenate([q1, q2], axis=0).astype(BF16)
        m_scr[...] = jnp.full_like(m_scr, NEG_INF)
        l_scr[...] = jnp.zeros_like(l_scr)
        acc_scr[...] = jnp.zeros_like(acc_scr)

    def update(diagonal):
        for kb in range(tq // ATTN_KEY_SUB):
            k0 = kb * ATTN_KEY_SUB
            kblk = k_ref[k0:k0 + ATTN_KEY_SUB, :].astype(BF16)
            vtblk = v_ref[k0:k0 + ATTN_KEY_SUB, :].T.astype(BF16)
            for c in range(2 * tq // ATTN_Q_COLS):
                q0 = (c * ATTN_Q_COLS) % tq
                if diagonal and k0 > q0 + ATTN_Q_COLS - 1:
                    continue
                cs = slice(c * ATTN_Q_COLS, (c + 1) * ATTN_Q_COLS)
                st = _dot_nt(kblk, q2_scr[cs, :])
                if diagonal and k0 + ATTN_KEY_SUB - 1 > q0:
                    krow = k0 + lax.broadcasted_iota(jnp.int32, st.shape, 0)
                    qcol = q0 + lax.broadcasted_iota(jnp.int32, st.shape, 1)
                    st = jnp.where(krow <= qcol, st, NEG_INF)
                m_old = m_scr[:, cs]
                m_new = jnp.maximum(m_old, jnp.max(st, axis=0, keepdims=True))
                alpha = jnp.exp2(m_old - m_new)
                p = jnp.exp2(st - m_new)
                l_scr[:, cs] = alpha * l_scr[:, cs] + jnp.sum(p, axis=0, keepdims=True)
                acc_scr[:, cs] = alpha * acc_scr[:, cs] + _dot(vtblk, p.astype(BF16))
                m_scr[:, cs] = m_new

    @pl.when(ki < qi)
    def _():
        update(False)

    @pl.when(ki == qi)
    def _():
        update(True)
        acc = acc_scr[...]
        l = l_scr[...]
        ot = acc[:, :tq] / l[:, :tq] - lam_ref[:, 0:1] * (acc[:, tq:] / l[:, tq:])
        ms = jnp.mean(ot * ot, axis=0, keepdims=True)
        on = ot * lax.rsqrt(ms + SUBLN_EPS)
        o_ref[...] = (on.T * sw_ref[...] * post_scale).astype(o_ref.dtype)


def _attn_prompt(u, lam, subw, *, t, heads, tq, post_scale):
    nq = t // tq
    pairs = [(i, j) for i in range(nq) for j in range(i + 1)]
    qi_tab = jnp.array([p[0] for p in pairs], jnp.int32)
    ki_tab = jnp.array([p[1] for p in pairs], jnp.int32)
    vec = pl.BlockSpec((1, LANES), lambda h, s, qi, ki: (0, 0))
    grid_spec = pltpu.PrefetchScalarGridSpec(
        num_scalar_prefetch=2,
        grid=(heads, len(pairs)),
        in_specs=[vec,
                  pl.BlockSpec((tq, LANES), lambda h, s, qi, ki: (qi[s], h)),
                  pl.BlockSpec((tq, LANES), lambda h, s, qi, ki: (ki[s], heads + h)),
                  pl.BlockSpec((tq, LANES), lambda h, s, qi, ki: (ki[s], 2 * heads + h)),
                  vec],
        out_specs=pl.BlockSpec((tq, LANES), lambda h, s, qi, ki: (qi[s], h)),
        scratch_shapes=[pltpu.VMEM((2 * tq, LANES), BF16),
                        pltpu.VMEM((1, 2 * tq), F32),
                        pltpu.VMEM((1, 2 * tq), F32),
                        pltpu.VMEM((LANES, 2 * tq), F32)],
    )
    return pl.pallas_call(
        functools.partial(_attn_prompt_body, tq=tq, post_scale=post_scale),
        grid_spec=grid_spec,
        out_shape=jax.ShapeDtypeStruct((t, heads * LANES), BF16),
        compiler_params=_cparams(("parallel", "arbitrary")),
        name="diff_attn_prompt",
    )(qi_tab, ki_tab, lam, u, u, u, subw)


def _attn_sample_body(pt_ref, lam_ref, q_ref, *refs, heads, tn, post_scale, pages):
    groups = heads // HEAD_TILE
    kc_refs = refs[:pages]
    vc_refs = refs[pages:2 * pages]
    kn_ref, vn_ref, bias_ref, biasn_ref, sw_ref, o_ref, q2_scr, m_scr, l_scr, acc_scr = refs[2 * pages:]
    j = pl.program_id(1)
    n_steps = pl.num_programs(1) - 1
    half = HEAD_TILE * tn

    @pl.when(j == 0)
    def _():
        for h in range(heads):
            g, hh = divmod(h, HEAD_TILE)
            q1, q2 = _two_maps(q_ref[:, h * LANES:(h + 1) * LANES])
            q2_scr[g, hh * tn:(hh + 1) * tn] = q1
            q2_scr[g, half + hh * tn:half + (hh + 1) * tn] = q2
        m_scr[...] = jnp.full_like(m_scr, NEG_INF)
        l_scr[...] = jnp.zeros_like(l_scr)
        acc_scr[...] = jnp.zeros_like(acc_scr)

    def attend(k_pages, v_pages, b_ref):
        bias = b_ref[...]
        def flat(ref, g):
            tile = ref[:, g * HEAD_TILE:(g + 1) * HEAD_TILE, :]
            return tile.reshape(PAGE * HEAD_TILE, LANES).astype(BF16)

        for g in range(groups):
            q2 = q2_scr[g].astype(BF16)
            ss = [_dot_nt(q2, flat(kp, g)) + bias for kp in k_pages]
            m_old = m_scr[g]
            m_new = m_old
            for s in ss:
                m_new = jnp.maximum(m_new, jnp.max(s, axis=1, keepdims=True))
            alpha = jnp.exp2(m_old - m_new)
            l_new = alpha * l_scr[g]
            acc_new = alpha * acc_scr[g]
            for s, vp in zip(ss, v_pages):
                p = jnp.exp2(s - m_new)
                l_new = l_new + jnp.sum(p, axis=1, keepdims=True)
                acc_new = acc_new + _dot(p.astype(BF16), flat(vp, g))
            l_scr[g] = l_new
            acc_scr[g] = acc_new
            m_scr[g] = m_new

    @pl.when(j < n_steps)
    def _():
        attend(kc_refs, vc_refs, bias_ref)

    @pl.when(j == n_steps)
    def _():
        attend([kn_ref], [vn_ref], biasn_ref)
        for g in range(groups):
            acc = acc_scr[g]
            l = l_scr[g]
            o = acc[:half] / l[:half] - lam_ref[...] * (acc[half:] / l[half:])
            ms = jnp.mean(o * o, axis=-1, keepdims=True)
            o = o * lax.rsqrt(ms + SUBLN_EPS) * sw_ref[...] * post_scale
            for hh in range(HEAD_TILE):
                h = g * HEAD_TILE + hh
                o_ref[:, h * LANES:(h + 1) * LANES] = o[hh * tn:(hh + 1) * tn]


def _sample_bias(tn, causal):
    r = jnp.arange(2 * HEAD_TILE * tn)[:, None]
    c = jnp.arange(PAGE * HEAD_TILE)[None, :]
    ok = (c % HEAD_TILE) == ((r % (HEAD_TILE * tn)) // tn)
    if causal:
        ok = ok & ((c // HEAD_TILE) <= (r % tn))
    return jnp.where(ok, 0.0, NEG_INF).astype(F32)


def _attn_sample(q, cache_k, cache_v, k_new, v_new, page_table, lam, subw, *, layer, post_scale):
    b, n_pages = page_table.shape
    heads = cache_k.shape[3]
    groups = heads // HEAD_TILE
    tn = q.shape[0] // b
    rows = 2 * HEAD_TILE * tn
    pt = page_table.reshape(-1)
    pages = _pick_tile(n_pages, (SAMPLE_PAGES, 1))
    n_steps = n_pages // pages

    def page_spec(k):
        def page_map(i, j, pt_ref):
            return (layer, pt_ref[i * n_pages + jnp.minimum(j, n_steps - 1) * pages + k], 0, 0, 0)
        return pl.BlockSpec((None, None, PAGE, heads, LANES), page_map)

    page_specs = [page_spec(k) for k in range(pages)]
    new_spec = pl.BlockSpec((None, PAGE, heads, LANES), lambda i, j, pt_ref: (i, 0, 0, 0))
    vec = pl.BlockSpec((1, LANES), lambda i, j, pt_ref: (0, 0))
    bias_spec = pl.BlockSpec((rows, PAGE * HEAD_TILE), lambda i, j, pt_ref: (0, 0))
    grid_spec = pltpu.PrefetchScalarGridSpec(
        num_scalar_prefetch=1,
        grid=(b, n_steps + 1),
        in_specs=[vec,
                  pl.BlockSpec((tn, heads * LANES), lambda i, j, pt_ref: (i, 0)),
                  *page_specs, *page_specs, new_spec, new_spec, bias_spec, bias_spec, vec],
        out_specs=pl.BlockSpec((tn, heads * LANES), lambda i, j, pt_ref: (i, 0)),
        scratch_shapes=[pltpu.VMEM((groups, rows, LANES), F32),
                        pltpu.VMEM((groups, rows, 1), F32),
                        pltpu.VMEM((groups, rows, 1), F32),
                        pltpu.VMEM((groups, rows, LANES), F32)],
    )
    return pl.pallas_call(
        functools.partial(_attn_sample_body, heads=heads, tn=tn, post_scale=post_scale, pages=pages),
        grid_spec=grid_spec,
        out_shape=jax.ShapeDtypeStruct((b * tn, heads * LANES), F32),
        compiler_params=_cparams(("parallel", "arbitrary")),
        name="diff_attn_sample",
    )(pt, lam, q, *([cache_k] * pages), *([cache_v] * pages), k_new, v_new,
      _sample_bias(tn, False), _sample_bias(tn, True), subw)


def _pick_tile(n, candidates):
    for c in candidates:
        if n % c == 0:
            return c
    raise ValueError(f"no tile for {n}")


def _pair_states(s):
    b, h, _, _ = s.shape
    st = jnp.swapaxes(s, -1, -2).reshape(b, h // 2, 2, HS_R, HS_R)
    z = jnp.zeros_like(st[:, :, 0])
    top = jnp.concatenate([st[:, :, 0], z], axis=-1)
    bot = jnp.concatenate([z, st[:, :, 1]], axis=-1)
    return jnp.concatenate([top, bot], axis=-2)


def _head_states(sp):
    b, hp, _, _ = sp.shape
    s0 = sp[:, :, :HS_R, :HS_R]
    s1 = sp[:, :, HS_R:, HS_R:]
    st = jnp.stack([s0, s1], axis=2).reshape(b, 2 * hp, HS_R, HS_R)
    return jnp.swapaxes(st, -1, -2)


def _layer(l, depth, xp, xs, cache_k, cache_v, state_wkv, state_shift, page_table,
           w_in, mu_shift, w_decay0, w_decay2, a0, a2, g2, k_k, k_a, r_k, lnx_w, lnx_b,
           lambda_q1, lambda_k1, lambda_q2, lambda_k2, subln_w,
           w_pa, w_pb, w_gate, b_gate, w_o, ln1_g, ln1_b,
           w_ffn_gate, w_ffn_up, w_ffn_down, ln2_g, ln2_b):
    bp, sp, d = xp.shape
    bs, ts, _ = xs.shape
    assert bp == 1
    heads = cache_k.shape[3]
    att_w = heads * LANES
    rw_w = w_decay0.shape[1]
    h_r = rw_w // HS_R
    rw_in = 3 * rw_w + W_LORA + A_LORA + G_LORA
    assert w_in.shape[2] == 3 * att_w + rw_in and att_w == rw_w
    alpha = (2 * depth) ** 0.25
    lam_init = 0.8 - 0.6 * math.exp(-0.3 * l)
    post_scale = 1.0 - lam_init

    lam = (jnp.exp(jnp.sum(lambda_q1[l] * lambda_k1[l])) - jnp.exp(jnp.sum(lambda_q2[l] * lambda_k2[l]))
           + lam_init)
    lam = jnp.broadcast_to(lam.astype(F32), (1, LANES))

    n_p = bp * sp
    n_s = bs * ts
    m = n_p + n_s
    xp2 = xp.reshape(n_p, d)
    xs2 = xs.reshape(n_s, d)
    xb = jnp.concatenate([xp2.astype(BF16), xs2.astype(BF16)], axis=0)

    o_rw = 3 * att_w
    c_wl = o_rw + rw_w
    c_k = c_wl + W_LORA
    c_al = c_k + 2 * rw_w
    c_gl = c_al + A_LORA
    lora_w = 2 * LORA_PAD + G_LORA
    rw_cols = 3 * rw_w + lora_w

    def rest_cols(t, off, axis=1):
        take = lambda a, b: lax.slice_in_dim(t, a - off, b - off, axis=axis)
        zshape = list(t.shape)
        zshape[axis] = LORA_PAD - W_LORA
        z = jnp.zeros(zshape, t.dtype)
        return jnp.concatenate([take(c_k, c_al), take(c_wl, c_k), z, take(c_al, c_gl), z,
                                take(c_gl, c_gl + G_LORA)], axis=axis)

    def prep_cols(t):
        return jnp.concatenate([t[:, :rw_w], rest_cols(t, o_rw)], axis=1)

    wt = w_in[l].T
    wt_rest = rest_cols(wt, 0, axis=0)
    tm = _pick_tile(m, (768, 512, 384, 256, 128, 64, 32, 16, 8))
    tm_big = _pick_tile(m, (1408, 768, 512, 384, 256, 128, 64, 32, 16, 8))
    ua = _matmul_nt(xb, wt, n=c_wl, tm=tm_big, tn=_pick_tile(c_wl, (256, 128)), out_dtype=F32,
                    name="in_proj_qkvr")
    ub = _matmul_nt(xb, wt_rest, n=wt_rest.shape[0], tm=tm, tn=_pick_tile(wt_rest.shape[0], (512, 256, 128)),
                    out_dtype=F32, name="in_proj_rest")

    mu = prep_cols(mu_shift[l][None, :])
    pad_rows = lambda t: jnp.concatenate([t, jnp.zeros((LORA_PAD - t.shape[0], t.shape[1]), t.dtype)], axis=0)
    w2 = pad_rows(w_decay2[l]).astype(BF16)
    a2b = pad_rows(a2[l]).astype(BF16)
    g2b = g2[l].astype(BF16)
    row = lambda t: t[l][None, :]
    prep_args = (mu, row(w_decay0), w2, row(a0), a2b, g2b, row(k_k), row(k_a))

    tp = _pick_tile(sp, (128, 64, 32, 16, 8))
    prev_p = jnp.zeros((bp, 1, rw_cols), F32)
    prep_p = _rwkv_prep(ua, ub, 0, sp, prev_p, *prep_args, tm=tp, width=rw_w, r_col=o_rw, chained=True)

    prev_s = prep_cols(state_shift[l]).reshape(bs, 1, rw_cols)
    prep_s = _rwkv_prep(ua, ub, n_p, n_s, prev_s, *prep_args, tm=ts, width=rw_w, r_col=o_rw, chained=False)

    wkv_params = (row(lnx_w), row(lnx_b), r_k[l].reshape(1, rw_w))
    s0_p = jnp.zeros((bp, h_r // 2, LANES, LANES), F32)
    pairs = _pick_tile(h_r // 2, (WKV_PAIRS, 2, 1))
    o_rw_p, st_p = _wkv(*prep_p, s0_p, *wkv_params, seqs=bp, chunk=WKV_CHUNK, pairs=pairs)

    t_pad = -(-ts // WKV_CHUNK) * WKV_CHUNK
    pad_t = lambda t: jnp.pad(t.reshape(bs, ts, rw_w), ((0, 0), (0, t_pad - ts), (0, 0))).reshape(bs * t_pad, rw_w)
    s0_s = _pair_states(state_wkv[l].astype(F32))
    o_rw_s, st_s = _wkv(*[pad_t(t) for t in prep_s], s0_s, *wkv_params, seqs=bs, chunk=WKV_CHUNK, pairs=pairs)
    o_rw_s = o_rw_s.reshape(bs, t_pad, rw_w)[:, :ts].reshape(n_s, rw_w)
    o_rwkv = jnp.concatenate([o_rw_p, o_rw_s], axis=0)

    new_wkv_p = _head_states(st_p).astype(state_wkv.dtype)
    new_wkv_s = _head_states(st_s).astype(state_wkv.dtype)

    def shift_row(rows):
        ta = ua[rows]
        tb = ub[rows]
        w_ = rw_w
        return jnp.concatenate([ta[:, o_rw:], tb[:, 2 * w_:2 * w_ + W_LORA], tb[:, :2 * w_],
                                tb[:, 2 * w_ + LORA_PAD:2 * w_ + LORA_PAD + A_LORA],
                                tb[:, 2 * w_ + 2 * LORA_PAD:]], axis=1)

    new_shift_p = shift_row(jnp.array([sp - 1]))
    new_shift_s = shift_row(n_p + ts - 1 + ts * jnp.arange(bs))

    subw = subln_w[l][None, :]
    tq = _pick_tile(sp, (1024, 512, 256, 128))
    o_att_p = _attn_prompt(ua, lam, subw, t=sp, heads=heads, tq=tq, post_scale=post_scale)

    q_s = lax.slice(ua, (n_p, 0), (m, att_w))
    k_s = lax.slice(ua, (n_p, att_w), (m, 2 * att_w)).reshape(bs, ts, heads, DV)
    v_s = lax.slice(ua, (n_p, 2 * att_w), (m, 3 * att_w)).reshape(bs, ts, heads, DV)
    pad_new = lambda t: jnp.pad(t, ((0, 0), (0, PAGE - ts), (0, 0), (0, 0)))
    o_att_s = _attn_sample(q_s, cache_k, cache_v, pad_new(k_s), pad_new(v_s), page_table, lam, subw,
                           layer=l, post_scale=post_scale)
    o_att = jnp.concatenate([o_att_p, o_att_s.astype(BF16)], axis=0)

    wg = w_gate[l].astype(BF16)
    bg = b_gate[l][None, :]
    mixin = _gated_merge(xb, o_att, o_rwkv, wg, bg, w_pa[l].astype(BF16), w_pb[l].astype(BF16),
                         tm=tm, tn=_pick_tile(d, (256, 128)))
    mix = _matmul(mixin, w_o[l], tm=tm, tn=_pick_tile(d, (512, 256, 128)), out_dtype=F32, name="out_proj")
    tl = _pick_tile(math.gcd(n_p, n_s), (256, 128, 64, 32, 16, 8))
    x1, x1b = _add_norm(mix, [(xp2, 0), (xs2, n_p)], row(ln1_g), row(ln1_b), alpha=alpha, tm=tl,
                        outs=[(m, 0, F32), (m, 0, BF16)])

    d_ff = w_ffn_gate.shape[2]
    hmid = _swiglu(x1b, w_ffn_gate[l], w_ffn_up[l], tm=tm_big, tn=_pick_tile(d_ff, (256, 128)))
    down = _matmul(hmid, w_ffn_down[l].astype(BF16), tm=tm, tn=_pick_tile(d, (256, 128)), out_dtype=F32,
                   name="ffn_down")
    y_p, y_s = _add_norm(down, [(x1, 0)], row(ln2_g), row(ln2_b), alpha=alpha, tm=tl,
                         outs=[(n_p, 0, F32), (n_s, n_p, F32)])

    k_all = lax.slice(ua, (0, att_w), (m, 2 * att_w))
    v_all = lax.slice(ua, (0, 2 * att_w), (m, 3 * att_w))
    outs = dict(
        xp=y_p.reshape(bp, sp, d), xs=y_s.reshape(bs, ts, d),
        kp=k_all[:n_p].reshape(bp, sp, heads, DV), vp=v_all[:n_p].reshape(bp, sp, heads, DV),
        wp=new_wkv_p, sp=new_shift_p,
        ks=k_s, vs=v_s, ws=new_wkv_s, ss=new_shift_s)
    return outs


def kernel(x_prompt, x_sample, cache_k, cache_v, state_wkv, state_shift, page_table, w_in, mu_shift, w_decay0, w_decay2, a0, a2, g2, k_k, k_a, r_k, lnx_w, lnx_b, lambda_q1, lambda_k1, lambda_q2, lambda_k2, subln_w, w_pa, w_pb, w_gate, b_gate, w_o, ln1_g, ln1_b, w_ffn_gate, w_ffn_up, w_ffn_down, ln2_g, ln2_b):
    depth = w_in.shape[0]
    xp, xs = x_prompt, x_sample
    acc = {n: [] for n in ("kp", "vp", "wp", "sp", "ks", "vs", "ws", "ss")}
    for l in range(depth):
        o = _layer(l, depth, xp, xs, cache_k, cache_v, state_wkv, state_shift, page_table,
                   w_in, mu_shift, w_decay0, w_decay2, a0, a2, g2, k_k, k_a, r_k, lnx_w, lnx_b,
                   lambda_q1, lambda_k1, lambda_q2, lambda_k2, subln_w,
                   w_pa, w_pb, w_gate, b_gate, w_o, ln1_g, ln1_b,
                   w_ffn_gate, w_ffn_up, w_ffn_down, ln2_g, ln2_b)
        xp, xs = o["xp"], o["xs"]
        for n in acc:
            acc[n].append(o[n])
    st = lambda n: jnp.stack(acc[n])
    return (xp, xs, st("kp"), st("vp"), st("wp"), st("sp"), st("ks"), st("vs"), st("ws"), st("ss"))
```

```python
import functools
import math

import jax
import jax.numpy as jnp
from jax import lax
from jax.experimental import pallas as pl
from jax.experimental.pallas import tpu as pltpu

F32 = jnp.float32
BF16 = jnp.bfloat16

LANES = 128
SUBLANES = 8
VMEM_LIMIT = 56 * 1024 * 1024

DK = 64
DV = 128
HS_R = 64
W_LORA = 96
A_LORA = 96
G_LORA = 256
LORA_PAD = 128
ATTN_SCALE = DK ** -0.5
LOG2E = math.log2(math.e)
ATTN_KEY_SUB = 256
ATTN_Q_COLS = 128
SAMPLE_PAGES = 4
SUBLN_EPS = 1e-5
NEG_INF = -1e30
GN_EPS = 64e-5
LN_EPS = 1e-5
PAGE = 128
HEAD_TILE = 8
WKV_CHUNK = 64
WKV_PAIRS = 8


def _cparams(sem):
    return pltpu.CompilerParams(dimension_semantics=sem, vmem_limit_bytes=VMEM_LIMIT)


def _dot(a, b):
    return jnp.dot(a, b, preferred_element_type=F32)


def _dot_nt(a, b):
    return lax.dot_general(a, b, (((1,), (1,)), ((), ())), preferred_element_type=F32)


def _dot_tn(a, b):
    return lax.dot_general(a, b, (((0,), (0,)), ((), ())), preferred_element_type=F32)


def _sigmoid(x):
    return 1.0 / (1.0 + jnp.exp(-x))


def _mm_body(x_ref, w_ref, o_ref):
    o_ref[...] = _dot(x_ref[...], w_ref[...].astype(BF16)).astype(o_ref.dtype)


def _matmul(x, w, *, tm, tn, out_dtype, name, n=None):
    m, k = x.shape
    n = w.shape[1] if n is None else n
    return pl.pallas_call(
        _mm_body,
        grid=(m // tm, n // tn),
        in_specs=[pl.BlockSpec((tm, k), lambda i, j: (i, 0)),
                  pl.BlockSpec((k, tn), lambda i, j: (0, j))],
        out_specs=pl.BlockSpec((tm, tn), lambda i, j: (i, j)),
        out_shape=jax.ShapeDtypeStruct((m, n), out_dtype),
        compiler_params=_cparams(("parallel", "parallel")),
        name=name,
    )(x, w)


def _mm_nt_body(x_ref, wt_ref, o_ref):
    o_ref[...] = _dot_nt(x_ref[...], wt_ref[...].astype(BF16)).astype(o_ref.dtype)


def _matmul_nt(x, wt, *, n, tm, tn, out_dtype, name):
    m, k = x.shape
    return pl.pallas_call(
        _mm_nt_body,
        grid=(m // tm, n // tn),
        in_specs=[pl.BlockSpec((tm, k), lambda i, j: (i, 0)),
                  pl.BlockSpec((tn, k), lambda i, j: (j, 0))],
        out_specs=pl.BlockSpec((tm, tn), lambda i, j: (i, j)),
        out_shape=jax.ShapeDtypeStruct((m, n), out_dtype),
        compiler_params=_cparams(("parallel", "parallel")),
        name=name,
    )(x, wt)


def _gated_merge_body(x_ref, oa_ref, ob_ref, wga_ref, wgb_ref, ba_ref, bb_ref, wpa_ref, wpb_ref, o_ref):
    x = x_ref[...]
    ga = _sigmoid(_dot(x, wga_ref[...]) + ba_ref[...])
    gb = _sigmoid(_dot(x, wgb_ref[...]) + bb_ref[...])
    o = ga * _dot(oa_ref[...], wpa_ref[...]) + gb * _dot(ob_ref[...], wpb_ref[...])
    o_ref[...] = o.astype(o_ref.dtype)


def _gated_merge(x, oa, ob, wg, bg, wpa, wpb, *, tm, tn):
    m, d = x.shape
    ka = oa.shape[1]
    kb = ob.shape[1]
    n = wpa.shape[1]
    nb = n // tn
    row = lambda w: pl.BlockSpec((tm, w), lambda i, j: (i, 0))
    col = lambda k, off: pl.BlockSpec((k, tn), lambda i, j: (0, j + off))
    return pl.pallas_call(
        _gated_merge_body,
        grid=(m // tm, nb),
        in_specs=[row(d), row(ka), row(kb), col(d, 0), col(d, nb), col(1, 0), col(1, nb), col(ka, 0), col(kb, 0)],
        out_specs=pl.BlockSpec((tm, tn), lambda i, j: (i, j)),
        out_shape=jax.ShapeDtypeStruct((m, n), BF16),
        compiler_params=_cparams(("parallel", "parallel")),
        name="gated_merge",
    )(x, oa, ob, wg, wg, bg, bg, wpa, wpb)


def _swiglu_body(x_ref, wg_ref, wu_ref, o_ref):
    x = x_ref[...]
    g = _dot(x, wg_ref[...].astype(BF16))
    u = _dot(x, wu_ref[...].astype(BF16))
    o_ref[...] = (g * _sigmoid(g) * u).astype(o_ref.dtype)


def _swiglu(x, wg, wu, *, tm, tn):
    m, k = x.shape
    n = wg.shape[1]
    return pl.pallas_call(
        _swiglu_body,
        grid=(m // tm, n // tn),
        in_specs=[pl.BlockSpec((tm, k), lambda i, j: (i, 0)),
                  pl.BlockSpec((k, tn), lambda i, j: (0, j)),
                  pl.BlockSpec((k, tn), lambda i, j: (0, j))],
        out_specs=pl.BlockSpec((tm, tn), lambda i, j: (i, j)),
        out_shape=jax.ShapeDtypeStruct((m, n), BF16),
        compiler_params=_cparams(("parallel", "parallel")),
        name="swiglu",
    )(x, wg, wu)


def _add_norm_body(*refs, alpha, res_first, out_first, out_blocks):
    n_res = len(res_first)
    x_ref = refs[0]
    res_refs = refs[1:1 + n_res]
    g_ref, b_ref = refs[1 + n_res:3 + n_res]
    out_refs = refs[3 + n_res:]
    i = pl.program_id(0)
    r = res_refs[0][...]
    for k in range(1, n_res):
        r = jnp.where(i >= res_first[k], res_refs[k][...], r)
    x = alpha * r + x_ref[...]
    mu = jnp.mean(x, axis=-1, keepdims=True)
    d = x - mu
    var = jnp.mean(d * d, axis=-1, keepdims=True)
    y = d * lax.rsqrt(var + LN_EPS) * g_ref[...] + b_ref[...]
    for o_ref, first, nblk in zip(out_refs, out_first, out_blocks):
        @pl.when((i >= first) & (i < first + nblk))
        def _():
            o_ref[...] = y.astype(o_ref.dtype)


def _add_norm(x, resids, g, b, *, alpha, tm, outs):
    m, d = x.shape
    row_spec = lambda first, rows: pl.BlockSpec(
        (tm, d), lambda i: (jnp.clip(i - first // tm, 0, rows // tm - 1), 0))
    vec = pl.BlockSpec((1, d), lambda i: (0, 0))
    body = functools.partial(
        _add_norm_body, alpha=alpha,
        res_first=tuple(first // tm for _, first in resids),
        out_first=tuple(first // tm for _, first, _ in outs),
        out_blocks=tuple(rows // tm for rows, _, _ in outs))
    return pl.pallas_call(
        body,
        grid=(m // tm,),
        in_specs=[row_spec(0, m)] + [row_spec(first, r.shape[0]) for r, first in resids] + [vec, vec],
        out_specs=[row_spec(first, rows) for rows, first, _ in outs],
        out_shape=[jax.ShapeDtypeStruct((rows, d), dt) for rows, _, dt in outs],
        compiler_params=_cparams(("arbitrary",)),
        name="add_norm",
    )(x, *[r for r, _ in resids], g, b)


def _rwkv_prep_body(ur_ref, uk_ref, uv_ref, ul_ref, ar_ref, ak_ref, av_ref, al_ref, prev_ref, mu_ref,
                    w0_ref, w2_ref, a0_ref, a2_ref, g2_ref, kk_ref, ka_ref,
                    r_out, lw_out, k_out, v_out, kk_out, a_out, g_out, *, tm, width, chained):
    first = lax.broadcasted_iota(jnp.int32, (tm, 1), 0) == 0
    start = pl.program_id(0) == 0

    def mixed(u_ref, above_ref, lo, hi):
        u = u_ref[...]
        prev = prev_ref[:, lo:hi]
        if chained:
            prev = jnp.where(start, prev, above_ref[SUBLANES - 1:SUBLANES, :])
        shifted = jnp.where(first, jnp.broadcast_to(prev, u.shape), pltpu.roll(u, shift=1, axis=0))
        return u + (shifted - u) * mu_ref[:, lo:hi]

    w = width
    r = mixed(ur_ref, ar_ref, 0, w)
    k = mixed(uk_ref, ak_ref, w, 2 * w)
    v = mixed(uv_ref, av_ref, 2 * w, 3 * w)
    lora = mixed(ul_ref, al_ref, 3 * w, 3 * w + 2 * LORA_PAD + G_LORA)
    w_l = lora[:, 0:LORA_PAD]
    a_l = lora[:, LORA_PAD:2 * LORA_PAD]
    g_l = lora[:, 2 * LORA_PAD:]

    z = w0_ref[...] + _dot(jnp.tanh(w_l).astype(BF16), w2_ref[...])
    nz = -z
    softplus = jnp.maximum(nz, 0.0) + jnp.log(1.0 + jnp.exp(-jnp.abs(nz)))
    wdec = -softplus - 0.5
    a = _sigmoid(a0_ref[...] + _dot(a_l.astype(BF16), a2_ref[...]))
    g = _dot(_sigmoid(g_l).astype(BF16), g2_ref[...])

    r_out[...] = r
    lw_out[...] = -jnp.exp(wdec)
    k_out[...] = k * (1.0 + (a - 1.0) * ka_ref[...])
    v_out[...] = v
    kk_out[...] = k * kk_ref[...]
    a_out[...] = a
    g_out[...] = g


def _rwkv_prep(ua, ub, row0, rows, prev, mu, w0, w2, a0, a2, g2, k_k, k_a, *, tm, width, r_col, chained):
    nblk = rows // tm
    rb0 = row0 // tm
    cb = r_col // width
    lw = 2 * LORA_PAD + G_LORA
    lb = 2 * width // lw
    rw_w = prev.shape[2]
    vec = lambda n: pl.BlockSpec((1, n), lambda i: (0, 0))
    out = jax.ShapeDtypeStruct((rows, width), F32)
    tiles = tm // SUBLANES
    above = lambda i: jnp.maximum((rb0 + i) * tiles - 1, 0)
    blk = lambda wd, c: pl.BlockSpec((tm, wd), lambda i: (rb0 + i, c))
    abv = lambda wd, c: pl.BlockSpec((SUBLANES, wd), lambda i: (above(i), c))
    body = functools.partial(_rwkv_prep_body, tm=tm, width=width, chained=chained)
    return pl.pallas_call(
        body,
        grid=(nblk,),
        in_specs=[blk(width, cb), blk(width, 0), blk(width, 1), blk(lw, lb),
                  abv(width, cb), abv(width, 0), abv(width, 1), abv(lw, lb),
                  pl.BlockSpec((None, 1, rw_w), (lambda i: (0, 0, 0)) if chained else (lambda i: (i, 0, 0))),
                  vec(rw_w), vec(width),
                  pl.BlockSpec((LORA_PAD, width), lambda i: (0, 0)),
                  vec(width),
                  pl.BlockSpec((LORA_PAD, width), lambda i: (0, 0)),
                  pl.BlockSpec((G_LORA, width), lambda i: (0, 0)),
                  vec(width), vec(width)],
        out_specs=[pl.BlockSpec((tm, width), lambda i: (i, 0))] * 7,
        out_shape=[out] * 7,
        compiler_params=_cparams(("parallel",)),
        name="rwkv_prep",
    )(ua, ub, ub, ub, ua, ub, ub, ub, prev, mu, w0, w2, a0, a2, g2, k_k, k_a)


def _wkv_pair(r, lw, k, v, kk, a, g, st, lnw, lnb, rk, *, chunk):
    C = chunk
    C2 = 2 * C

    lane = lax.broadcasted_iota(jnp.int32, (C2, LANES), 1)
    rowi = lax.broadcasted_iota(jnp.int32, (C2, LANES), 0)
    own = (lane >= HS_R) == (rowi >= C)

    def twice(x):
        return jnp.concatenate([x, x], axis=0)

    def stack(x):
        return jnp.where(own, twice(x), 0.0)

    tr = lax.broadcasted_iota(jnp.int32, (C, C), 0)
    tc = lax.broadcasted_iota(jnp.int32, (C, C), 1)
    tri = (tr >= tc).astype(BF16)
    lw_hi = lw.astype(BF16)
    rem = lw - lw_hi.astype(F32)
    lw_mid = rem.astype(BF16)
    lw_lo = (rem - lw_mid.astype(F32)).astype(BF16)
    G = _dot(tri, lw_hi) + _dot(tri, lw_mid) + _dot(tri, lw_lo)
    yield None
    GC = G[C - 1:C, :]
    e_prev = jnp.exp(G - lw)
    e_neg = jnp.exp(-G)
    e_pos = jnp.exp(G)
    e_rest = jnp.exp(GC - G)
    e_all = jnp.exp(GC)

    kk2 = stack(kk)
    nrm = jnp.sqrt(jnp.sum(kk2 * kk2, axis=1, keepdims=True))
    kk2 = kk2 / jnp.maximum(nrm, 1e-12)

    A2 = (kk2 * twice(-e_prev)).astype(BF16)
    B2 = (kk2 * twice(a * e_neg)).astype(BF16)
    K2 = stack(k * e_neg).astype(BF16)
    R2 = stack(r * e_pos).astype(BF16)
    Bt2 = (kk2 * twice(a * e_rest)).astype(BF16)
    Kt2 = stack(k * e_rest).astype(BF16)
    V2f = stack(v)
    V2 = V2f.astype(BF16)

    AR = jnp.concatenate([A2, R2], axis=0)
    mr = lax.broadcasted_iota(jnp.int32, (2 * C2, 2 * C2), 0)
    mc = lax.broadcasted_iota(jnp.int32, (2 * C2, 2 * C2), 1)
    tr_r = mr % C2
    tr_c = mc % C2
    keep = tr_r + jnp.where(mr < C2, 0, 1) > tr_c
    m_all = jnp.where(keep, _dot_nt(AR, jnp.concatenate([B2, K2], axis=0)), 0.0)
    m_ab = m_all[:C2, :C2]
    m_rb = m_all[C2:, :C2].astype(BF16)
    m_xk = m_all[:, C2:].astype(BF16)
    eye = (lax.broadcasted_iota(jnp.int32, (C2, C2), 0) == lax.broadcasted_iota(jnp.int32, (C2, C2), 1)).astype(F32)
    yield None

    t_inv = eye + m_ab
    pb = m_ab.astype(BF16)
    p = _dot(pb, pb)
    yield None
    n = 2
    while 2 * n < C:
        pb = p.astype(BF16)
        both = _dot(jnp.concatenate([pb, t_inv.astype(BF16)], axis=0), pb)
        p = both[:C2]
        t_inv = t_inv + both[C2:]
        n *= 2
        yield None
    t_inv = t_inv + _dot(t_inv.astype(BF16), p.astype(BF16))
    yield None

    stb = st.astype(BF16)
    xy = _dot(AR, stb) + _dot(m_xk, V2)
    yield None
    u2 = _dot(t_inv.astype(BF16), xy[:C2].astype(BF16))
    yield None
    u2b = u2.astype(BF16)
    y2 = xy[C2:] + _dot(m_rb, u2b)

    decay_col = jnp.broadcast_to(e_all, (LANES, LANES)).T
    st_new = decay_col * st + _dot_tn(jnp.concatenate([Bt2, Kt2], axis=0), jnp.concatenate([u2b, V2], axis=0))
    yield None

    inv_hs = 1.0 / HS_R
    mean = jnp.sum(y2, axis=1, keepdims=True) * inv_hs
    d = jnp.where(own, y2 - mean, 0.0)
    var = jnp.sum(d * d, axis=1, keepdims=True) * inv_hs
    yn2 = d * lax.rsqrt(var + GN_EPS)
    bonus2 = jnp.sum(stack(r * k * rk), axis=1, keepdims=True) * V2f
    yn = yn2[:C] + yn2[C:]
    bonus = bonus2[:C] + bonus2[C:]
    yield (yn * lnw + lnb + bonus) * g, st_new


def _wkv_body(r_ref, lw_ref, k_ref, v_ref, kk_ref, a_ref, g_ref, s0_ref, lnw_ref, lnb_ref, rk_ref,
              o_ref, st_ref, st_scr, *, chunk, pairs):
    c = pl.program_id(2)

    @pl.when(c == 0)
    def _():
        st_scr[...] = s0_ref[...]

    lanes = [slice(pp * LANES, (pp + 1) * LANES) for pp in range(pairs)]
    gens = [_wkv_pair(r_ref[:, sl], lw_ref[:, sl], k_ref[:, sl], v_ref[:, sl], kk_ref[:, sl],
                      a_ref[:, sl], g_ref[:, sl], st_scr[pp], lnw_ref[:, sl], lnb_ref[:, sl],
                      rk_ref[:, sl], chunk=chunk) for pp, sl in enumerate(lanes)]
    results = [None] * pairs
    while any(res is None for res in results):
        results = [next(gen) for gen in gens]
    for pp, (o, st_new) in enumerate(results):
        st_scr[pp] = st_new
        o_ref[:, lanes[pp]] = o.astype(o_ref.dtype)

    @pl.when(c == pl.num_programs(2) - 1)
    def _():
        st_ref[...] = st_scr[...]


def _wkv(r, lw, k, v, kk, a, g, s0, lnw, lnb, rk, *, seqs, chunk, pairs):
    rows, width = r.shape
    t = rows // seqs
    nchunk = t // chunk
    npair = width // LANES
    wblk = pairs * LANES
    blk = pl.BlockSpec((chunk, wblk), lambda b, p, c: (b * nchunk + c, p))
    vec = pl.BlockSpec((1, wblk), lambda b, p, c: (0, p))
    stspec = pl.BlockSpec((None, pairs, LANES, LANES), lambda b, p, c: (b, p, 0, 0))
    return pl.pallas_call(
        functools.partial(_wkv_body, chunk=chunk, pairs=pairs),
        grid=(seqs, npair // pairs, nchunk),
        in_specs=[blk] * 7 + [stspec, vec, vec, vec],
        out_specs=[blk, stspec],
        out_shape=[jax.ShapeDtypeStruct((rows, width), BF16),
                   jax.ShapeDtypeStruct((seqs, npair, LANES, LANES), F32)],
        scratch_shapes=[pltpu.VMEM((pairs, LANES, LANES), F32)],
        compiler_params=_cparams(("parallel", "parallel", "arbitrary")),
        name="wkv7_chunked",
    )(r, lw, k, v, kk, a, g, s0, lnw, lnb, rk)


def _two_maps(q):
    lane = lax.broadcasted_iota(jnp.int32, q.shape, 1)
    q = q * (ATTN_SCALE * LOG2E)
    return jnp.where(lane < DK, q, 0.0), jnp.where(lane >= DK, q, 0.0)


def _attn_prompt_body(qi_ref, ki_ref, lam_ref, q_ref, k_ref, v_ref, sw_ref, o_ref,
                      q2_scr, m_scr, l_scr, acc_scr, *, tq, post_scale):
    step = pl.program_id(1)
    qi = qi_ref[step]
    ki = ki_ref[step]

    @pl.when(ki == 0)
    def _():
        q1, q2 = _two_maps(q_ref[...])
        q2_scr[...] = jnp.concatenate([q1, q2], axis=0).astype(BF16)
        m_scr[...] = jnp.full_like(m_scr, NEG_INF)
        l_scr[...] = jnp.zeros_like(l_scr)
        acc_scr[...] = jnp.zeros_like(acc_scr)

    def update(diagonal):
        for kb in range(tq // ATTN_KEY_SUB):
            k0 = kb * ATTN_KEY_SUB
            kblk = k_ref[k0:k0 + ATTN_KEY_SUB, :].astype(BF16)
            vtblk = v_ref[k0:k0 + ATTN_KEY_SUB, :].T.astype(BF16)
            for c in range(2 * tq // ATTN_Q_COLS):
                q0 = (c * ATTN_Q_COLS) % tq
                if diagonal and k0 > q0 + ATTN_Q_COLS - 1:
                    continue
                cs = slice(c * ATTN_Q_COLS, (c + 1) * ATTN_Q_COLS)
                st = _dot_nt(kblk, q2_scr[cs, :])
                if diagonal and k0 + ATTN_KEY_SUB - 1 > q0:
                    krow = k0 + lax.broadcasted_iota(jnp.int32, st.shape, 0)
                    qcol = q0 + lax.broadcasted_iota(jnp.int32, st.shape, 1)
                    st = jnp.where(krow <= qcol, st, NEG_INF)
                m_old = m_scr[:, cs]
                m_new = jnp.maximum(m_old, jnp.max(st, axis=0, keepdims=True))
                alpha = jnp.exp2(m_old - m_new)
                p = jnp.exp2(st - m_new)
                l_scr[:, cs] = alpha * l_scr[:, cs] + jnp.sum(p, axis=0, keepdims=True)
                acc_scr[:, cs] = alpha * acc_scr[:, cs] + _dot(vtblk, p.astype(BF16))
                m_scr[:, cs] = m_new

    @pl.when(ki < qi)
    def _():
        update(False)

    @pl.when(ki == qi)
    def _():
        update(True)
        acc = acc_scr[...]
        l = l_scr[...]
        ot = acc[:, :tq] / l[:, :tq] - lam_ref[:, 0:1] * (acc[:, tq:] / l[:, tq:])
        ms = jnp.mean(ot * ot, axis=0, keepdims=True)
        on = ot * lax.rsqrt(ms + SUBLN_EPS)
        o_ref[...] = (on.T * sw_ref[...] * post_scale).astype(o_ref.dtype)


def _attn_prompt(u, lam, subw, *, t, heads, tq, post_scale):
    nq = t // tq
    pairs = [(i, j) for i in range(nq) for j in range(i + 1)]
    qi_tab = jnp.array([p[0] for p in pairs], jnp.int32)
    ki_tab = jnp.array([p[1] for p in pairs], jnp.int32)
    vec = pl.BlockSpec((1, LANES), lambda h, s, qi, ki: (0, 0))
    grid_spec = pltpu.PrefetchScalarGridSpec(
        num_scalar_prefetch=2,
        grid=(heads, len(pairs)),
        in_specs=[vec,
                  pl.BlockSpec((tq, LANES), lambda h, s, qi, ki: (qi[s], h)),
                  pl.BlockSpec((tq, LANES), lambda h, s, qi, ki: (ki[s], heads + h)),
                  pl.BlockSpec((tq, LANES), lambda h, s, qi, ki: (ki[s], 2 * heads + h)),
                  vec],
        out_specs=pl.BlockSpec((tq, LANES), lambda h, s, qi, ki: (qi[s], h)),
        scratch_shapes=[pltpu.VMEM((2 * tq, LANES), BF16),
                        pltpu.VMEM((1, 2 * tq), F32),
                        pltpu.VMEM((1, 2 * tq), F32),
                        pltpu.VMEM((LANES, 2 * tq), F32)],
    )
    return pl.pallas_call(
        functools.partial(_attn_prompt_body, tq=tq, post_scale=post_scale),
        grid_spec=grid_spec,
        out_shape=jax.ShapeDtypeStruct((t, heads * LANES), BF16),
        compiler_params=_cparams(("parallel", "arbitrary")),
        name="diff_attn_prompt",
    )(qi_tab, ki_tab, lam, u, u, u, subw)


def _attn_sample_body(pt_ref, lam_ref, q_ref, *refs, heads, tn, post_scale, pages):
    groups = heads // HEAD_TILE
    kc_refs = refs[:pages]
    vc_refs = refs[pages:2 * pages]
    kn_ref, vn_ref, bias_ref, biasn_ref, sw_ref, o_ref, q2_scr, m_scr, l_scr, acc_scr = refs[2 * pages:]
    j = pl.program_id(1)
    n_steps = pl.num_programs(1) - 1
    half = HEAD_TILE * tn

    @pl.when(j == 0)
    def _():
        for h in range(heads):
            g, hh = divmod(h, HEAD_TILE)
            q1, q2 = _two_maps(q_ref[:, h * LANES:(h + 1) * LANES])
            q2_scr[g, hh * tn:(hh + 1) * tn] = q1
            q2_scr[g, half + hh * tn:half + (hh + 1) * tn] = q2
        m_scr[...] = jnp.full_like(m_scr, NEG_INF)
        l_scr[...] = jnp.zeros_like(l_scr)
        acc_scr[...] = jnp.zeros_like(acc_scr)

    def attend(k_pages, v_pages, b_ref):
        bias = b_ref[...]
        def flat(ref, g):
            tile = ref[:, g * HEAD_TILE:(g + 1) * HEAD_TILE, :]
            return tile.reshape(PAGE * HEAD_TILE, LANES).astype(BF16)

        for g in range(groups):
            q2 = q2_scr[g].astype(BF16)
            ss = [_dot_nt(q2, flat(kp, g)) + bias for kp in k_pages]
            m_old = m_scr[g]
            m_new = m_old
            for s in ss:
                m_new = jnp.maximum(m_new, jnp.max(s, axis=1, keepdims=True))
            alpha = jnp.exp2(m_old - m_new)
            l_new = alpha * l_scr[g]
            acc_new = alpha * acc_scr[g]
            for s, vp in zip(ss, v_pages):
                p = jnp.exp2(s - m_new)
                l_new = l_new + jnp.sum(p, axis=1, keepdims=True)
                acc_new = acc_new + _dot(p.astype(BF16), flat(vp, g))
            l_scr[g] = l_new
            acc_scr[g] = acc_new
            m_scr[g] = m_new

    @pl.when(j < n_steps)
    def _():
        attend(kc_refs, vc_refs, bias_ref)

    @pl.when(j == n_steps)
    def _():
        attend([kn_ref], [vn_ref], biasn_ref)
        for g in range(groups):
            acc = acc_scr[g]
            l = l_scr[g]
            o = acc[:half] / l[:half] - lam_ref[...] * (acc[half:] / l[half:])
            ms = jnp.mean(o * o, axis=-1, keepdims=True)
            o = o * lax.rsqrt(ms + SUBLN_EPS) * sw_ref[...] * post_scale
            for hh in range(HEAD_TILE):
                h = g * HEAD_TILE + hh
                o_ref[:, h * LANES:(h + 1) * LANES] = o[hh * tn:(hh + 1) * tn]


def _sample_bias(tn, causal):
    r = jnp.arange(2 * HEAD_TILE * tn)[:, None]
    c = jnp.arange(PAGE * HEAD_TILE)[None, :]
    ok = (c % HEAD_TILE) == ((r % (HEAD_TILE * tn)) // tn)
    if causal:
        ok = ok & ((c // HEAD_TILE) <= (r % tn))
    return jnp.where(ok, 0.0, NEG_INF).astype(F32)


def _attn_sample(q, cache_k, cache_v, k_new, v_new, page_table, lam, subw, *, layer, post_scale):
    b, n_pages = page_table.shape
    heads = cache_k.shape[3]
    groups = heads // HEAD_TILE
    tn = q.shape[0] // b
    rows = 2 * HEAD_TILE * tn
    pt = page_table.reshape(-1)
    pages = _pick_tile(n_pages, (SAMPLE_PAGES, 1))
    n_steps = n_pages // pages

    def page_spec(k):
        def page_map(i, j, pt_ref):
            return (layer, pt_ref[i * n_pages + jnp.minimum(j, n_steps - 1) * pages + k], 0, 0, 0)
        return pl.BlockSpec((None, None, PAGE, heads, LANES), page_map)

    page_specs = [page_spec(k) for k in range(pages)]
    new_spec = pl.BlockSpec((None, PAGE, heads, LANES), lambda i, j, pt_ref: (i, 0, 0, 0))
    vec = pl.BlockSpec((1, LANES), lambda i, j, pt_ref: (0, 0))
    bias_spec = pl.BlockSpec((rows, PAGE * HEAD_TILE), lambda i, j, pt_ref: (0, 0))
    grid_spec = pltpu.PrefetchScalarGridSpec(
        num_scalar_prefetch=1,
        grid=(b, n_steps + 1),
        in_specs=[vec,
                  pl.BlockSpec((tn, heads * LANES), lambda i, j, pt_ref: (i, 0)),
                  *page_specs, *page_specs, new_spec, new_spec, bias_spec, bias_spec, vec],
        out_specs=pl.BlockSpec((tn, heads * LANES), lambda i, j, pt_ref: (i, 0)),
        scratch_shapes=[pltpu.VMEM((groups, rows, LANES), F32),
                        pltpu.VMEM((groups, rows, 1), F32),
                        pltpu.VMEM((groups, rows, 1), F32),
                        pltpu.VMEM((groups, rows, LANES), F32)],
    )
    return pl.pallas_call(
        functools.partial(_attn_sample_body, heads=heads, tn=tn, post_scale=post_scale, pages=pages),
        grid_spec=grid_spec,
        out_shape=jax.ShapeDtypeStruct((b * tn, heads * LANES), F32),
        compiler_params=_cparams(("parallel", "arbitrary")),
        name="diff_attn_sample",
    )(pt, lam, q, *([cache_k] * pages), *([cache_v] * pages), k_new, v_new,
      _sample_bias(tn, False), _sample_bias(tn, True), subw)


def _pick_tile(n, candidates):
    for c in candidates:
        if n % c == 0:
            return c
    raise ValueError(f"no tile for {n}")


def _pair_states(s):
    b, h, _, _ = s.shape
    st = jnp.swapaxes(s, -1, -2).reshape(b, h // 2, 2, HS_R, HS_R)
    z = jnp.zeros_like(st[:, :, 0])
    top = jnp.concatenate([st[:, :, 0], z], axis=-1)
    bot = jnp.concatenate([z, st[:, :, 1]], axis=-1)
    return jnp.concatenate([top, bot], axis=-2)


def _head_states(sp):
    b, hp, _, _ = sp.shape
    s0 = sp[:, :, :HS_R, :HS_R]
    s1 = sp[:, :, HS_R:, HS_R:]
    st = jnp.stack([s0, s1], axis=2).reshape(b, 2 * hp, HS_R, HS_R)
    return jnp.swapaxes(st, -1, -2)


def _layer(l, depth, xp, xs, cache_k, cache_v, state_wkv, state_shift, page_table,
           w_in, mu_shift, w_decay0, w_decay2, a0, a2, g2, k_k, k_a, r_k, lnx_w, lnx_b,
           lambda_q1, lambda_k1, lambda_q2, lambda_k2, subln_w,
           w_pa, w_pb, w_gate, b_gate, w_o, ln1_g, ln1_b,
           w_ffn_gate, w_ffn_up, w_ffn_down, ln2_g, ln2_b):
    bp, sp, d = xp.shape
    bs, ts, _ = xs.shape
    assert bp == 1
    heads = cache_k.shape[3]
    att_w = heads * LANES
    rw_w = w_decay0.shape[1]
    h_r = rw_w // HS_R
    rw_in = 3 * rw_w + W_LORA + A_LORA + G_LORA
    assert w_in.shape[2] == 3 * att_w + rw_in and att_w == rw_w
    alpha = (2 * depth) ** 0.25
    lam_init = 0.8 - 0.6 * math.exp(-0.3 * l)
    post_scale = 1.0 - lam_init

    lam = (jnp.exp(jnp.sum(lambda_q1[l] * lambda_k1[l])) - jnp.exp(jnp.sum(lambda_q2[l] * lambda_k2[l]))
           + lam_init)
    lam = jnp.broadcast_to(lam.astype(F32), (1, LANES))

    n_p = bp * sp
    n_s = bs * ts
    m = n_p + n_s
    xp2 = xp.reshape(n_p, d)
    xs2 = xs.reshape(n_s, d)
    xb = jnp.concatenate([xp2.astype(BF16), xs2.astype(BF16)], axis=0)

    o_rw = 3 * att_w
    c_wl = o_rw + rw_w
    c_k = c_wl + W_LORA
    c_al = c_k + 2 * rw_w
    c_gl = c_al + A_LORA
    lora_w = 2 * LORA_PAD + G_LORA
    rw_cols = 3 * rw_w + lora_w

    def rest_cols(t, off, axis=1):
        take = lambda a, b: lax.slice_in_dim(t, a - off, b - off, axis=axis)
        zshape = list(t.shape)
        zshape[axis] = LORA_PAD - W_LORA
        z = jnp.zeros(zshape, t.dtype)
        return jnp.concatenate([take(c_k, c_al), take(c_wl, c_k), z, take(c_al, c_gl), z,
                                take(c_gl, c_gl + G_LORA)], axis=axis)

    def prep_cols(t):
        return jnp.concatenate([t[:, :rw_w], rest_cols(t, o_rw)], axis=1)

    wt = w_in[l].T
    wt_rest = rest_cols(wt, 0, axis=0)
    tm = _pick_tile(m, (768, 512, 384, 256, 128, 64, 32, 16, 8))
    tm_big = _pick_tile(m, (1408, 768, 512, 384, 256, 128, 64, 32, 16, 8))
    ua = _matmul_nt(xb, wt, n=c_wl, tm=tm_big, tn=_pick_tile(c_wl, (256, 128)), out_dtype=F32,
                    name="in_proj_qkvr")
    ub = _matmul_nt(xb, wt_rest, n=wt_rest.shape[0], tm=tm, tn=_pick_tile(wt_rest.shape[0], (512, 256, 128)),
                    out_dtype=F32, name="in_proj_rest")

    mu = prep_cols(mu_shift[l][None, :])
    pad_rows = lambda t: jnp.concatenate([t, jnp.zeros((LORA_PAD - t.shape[0], t.shape[1]), t.dtype)], axis=0)
    w2 = pad_rows(w_decay2[l]).astype(BF16)
    a2b = pad_rows(a2[l]).astype(BF16)
    g2b = g2[l].astype(BF16)
    row = lambda t: t[l][None, :]
    prep_args = (mu, row(w_decay0), w2, row(a0), a2b, g2b, row(k_k), row(k_a))

    tp = _pick_tile(sp, (128, 64, 32, 16, 8))
    prev_p = jnp.zeros((bp, 1, rw_cols), F32)
    prep_p = _rwkv_prep(ua, ub, 0, sp, prev_p, *prep_args, tm=tp, width=rw_w, r_col=o_rw, chained=True)

    prev_s = prep_cols(state_shift[l]).reshape(bs, 1, rw_cols)
    prep_s = _rwkv_prep(ua, ub, n_p, n_s, prev_s, *prep_args, tm=ts, width=rw_w, r_col=o_rw, chained=False)

    wkv_params = (row(lnx_w), row(lnx_b), r_k[l].reshape(1, rw_w))
    s0_p = jnp.zeros((bp, h_r // 2, LANES, LANES), F32)
    pairs = _pick_tile(h_r // 2, (WKV_PAIRS, 2, 1))
    o_rw_p, st_p = _wkv(*prep_p, s0_p, *wkv_params, seqs=bp, chunk=WKV_CHUNK, pairs=pairs)

    t_pad = -(-ts // WKV_CHUNK) * WKV_CHUNK
    pad_t = lambda t: jnp.pad(t.reshape(bs, ts, rw_w), ((0, 0), (0, t_pad - ts), (0, 0))).reshape(bs * t_pad, rw_w)
    s0_s = _pair_states(state_wkv[l].astype(F32))
    o_rw_s, st_s = _wkv(*[pad_t(t) for t in prep_s], s0_s, *wkv_params, seqs=bs, chunk=WKV_CHUNK, pairs=pairs)
    o_rw_s = o_rw_s.reshape(bs, t_pad, rw_w)[:, :ts].reshape(n_s, rw_w)
    o_rwkv = jnp.concatenate([o_rw_p, o_rw_s], axis=0)

    new_wkv_p = _head_states(st_p).astype(state_wkv.dtype)
    new_wkv_s = _head_states(st_s).astype(state_wkv.dtype)

    def shift_row(rows):
        ta = ua[rows]
        tb = ub[rows]
        w_ = rw_w
        return jnp.concatenate([ta[:, o_rw:], tb[:, 2 * w_:2 * w_ + W_LORA], tb[:, :2 * w_],
                                tb[:, 2 * w_ + LORA_PAD:2 * w_ + LORA_PAD + A_LORA],
                                tb[:, 2 * w_ + 2 * LORA_PAD:]], axis=1)

    new_shift_p = shift_row(jnp.array([sp - 1]))
    new_shift_s = shift_row(n_p + ts - 1 + ts * jnp.arange(bs))

    subw = subln_w[l][None, :]
    tq = _pick_tile(sp, (2048, 1024, 512, 256, 128))
    o_att_p = _attn_prompt(ua, lam, subw, t=sp, heads=heads, tq=tq, post_scale=post_scale)

    q_s = lax.slice(ua, (n_p, 0), (m, att_w))
    k_s = lax.slice(ua, (n_p, att_w), (m, 2 * att_w)).reshape(bs, ts, heads, DV)
    v_s = lax.slice(ua, (n_p, 2 * att_w), (m, 3 * att_w)).reshape(bs, ts, heads, DV)
    pad_new = lambda t: jnp.pad(t, ((0, 0), (0, PAGE - ts), (0, 0), (0, 0)))
    o_att_s = _attn_sample(q_s, cache_k, cache_v, pad_new(k_s), pad_new(v_s), page_table, lam, subw,
                           layer=l, post_scale=post_scale)
    o_att = jnp.concatenate([o_att_p, o_att_s.astype(BF16)], axis=0)

    wg = w_gate[l].astype(BF16)
    bg = b_gate[l][None, :]
    mixin = _gated_merge(xb, o_att, o_rwkv, wg, bg, w_pa[l].astype(BF16), w_pb[l].astype(BF16),
                         tm=tm, tn=_pick_tile(d, (256, 128)))
    mix = _matmul(mixin, w_o[l], tm=tm, tn=_pick_tile(d, (512, 256, 128)), out_dtype=F32, name="out_proj")
    tl = _pick_tile(math.gcd(n_p, n_s), (256, 128, 64, 32, 16, 8))
    x1, x1b = _add_norm(mix, [(xp2, 0), (xs2, n_p)], row(ln1_g), row(ln1_b), alpha=alpha, tm=tl,
                        outs=[(m, 0, F32), (m, 0, BF16)])

    d_ff = w_ffn_gate.shape[2]
    hmid = _swiglu(x1b, w_ffn_gate[l], w_ffn_up[l], tm=tm_big, tn=_pick_tile(d_ff, (256, 128)))
    down = _matmul(hmid, w_ffn_down[l].astype(BF16), tm=tm, tn=_pick_tile(d, (256, 128)), out_dtype=F32,
                   name="ffn_down")
    y_p, y_s = _add_norm(down, [(x1, 0)], row(ln2_g), row(ln2_b), alpha=alpha, tm=tl,
                         outs=[(n_p, 0, F32), (n_s, n_p, F32)])

    k_all = lax.slice(ua, (0, att_w), (m, 2 * att_w))
    v_all = lax.slice(ua, (0, 2 * att_w), (m, 3 * att_w))
    outs = dict(
        xp=y_p.reshape(bp, sp, d), xs=y_s.reshape(bs, ts, d),
        kp=k_all[:n_p].reshape(bp, sp, heads, DV), vp=v_all[:n_p].reshape(bp, sp, heads, DV),
        wp=new_wkv_p, sp=new_shift_p,
        ks=k_s, vs=v_s, ws=new_wkv_s, ss=new_shift_s)
    return outs


def kernel(x_prompt, x_sample, cache_k, cache_v, state_wkv, state_shift, page_table, w_in, mu_shift, w_decay0, w_decay2, a0, a2, g2, k_k, k_a, r_k, lnx_w, lnx_b, lambda_q1, lambda_k1, lambda_q2, lambda_k2, subln_w, w_pa, w_pb, w_gate, b_gate, w_o, ln1_g, ln1_b, w_ffn_gate, w_ffn_up, w_ffn_down, ln2_g, ln2_b):
    depth = w_in.shape[0]
    xp, xs = x_prompt, x_sample
    acc = {n: [] for n in ("kp", "vp", "wp", "sp", "ks", "vs", "ws", "ss")}
    for l in range(depth):
        o = _layer(l, depth, xp, xs, cache_k, cache_v, state_wkv, state_shift, page_table,
                   w_in, mu_shift, w_decay0, w_decay2, a0, a2, g2, k_k, k_a, r_k, lnx_w, lnx_b,
                   lambda_q1, lambda_k1, lambda_q2, lambda_k2, subln_w,
                   w_pa, w_pb, w_gate, b_gate, w_o, ln1_g, ln1_b,
                   w_ffn_gate, w_ffn_up, w_ffn_down, ln2_g, ln2_b)
        xp, xs = o["xp"], o["xs"]
        for n in acc:
            acc[n].append(o[n])
    st = lambda n: jnp.stack(acc[n])
    return (xp, xs, st("kp"), st("vp"), st("wp"), st("sp"), st("ks"), st("vs"), st("ws"), st("ss"))
```

```python
import functools
import math

import jax
import jax.numpy as jnp
from jax import lax
from jax.experimental import pallas as pl
from jax.experimental.pallas import tpu as pltpu

F32 = jnp.float32
BF16 = jnp.bfloat16

LANES = 128
SUBLANES = 8
VMEM_LIMIT = 56 * 1024 * 1024

DK = 64
DV = 128
HS_R = 64
W_LORA = 96
A_LORA = 96
G_LORA = 256
LORA_PAD = 128
ATTN_SCALE = DK ** -0.5
LOG2E = math.log2(math.e)
ATTN_KEY_SUB = 256
ATTN_Q_COLS = 128
SAMPLE_PAGES = 4
SUBLN_EPS = 1e-5
NEG_INF = -1e30
GN_EPS = 64e-5
LN_EPS = 1e-5
PAGE = 128
HEAD_TILE = 8
WKV_CHUNK = 64
WKV_PAIRS = 16


def _cparams(sem):
    return pltpu.CompilerParams(dimension_semantics=sem, vmem_limit_bytes=VMEM_LIMIT)


def _dot(a, b):
    return jnp.dot(a, b, preferred_element_type=F32)


def _dot_nt(a, b):
    return lax.dot_general(a, b, (((1,), (1,)), ((), ())), preferred_element_type=F32)


def _dot_tn(a, b):
    return lax.dot_general(a, b, (((0,), (0,)), ((), ())), preferred_element_type=F32)


def _sigmoid(x):
    return 1.0 / (1.0 + jnp.exp(-x))


def _mm_body(x_ref, w_ref, o_ref):
    o_ref[...] = _dot(x_ref[...], w_ref[...].astype(BF16)).astype(o_ref.dtype)


def _matmul(x, w, *, tm, tn, out_dtype, name, n=None):
    m, k = x.shape
    n = w.shape[1] if n is None else n
    return pl.pallas_call(
        _mm_body,
        grid=(m // tm, n // tn),
        in_specs=[pl.BlockSpec((tm, k), lambda i, j: (i, 0)),
                  pl.BlockSpec((k, tn), lambda i, j: (0, j))],
        out_specs=pl.BlockSpec((tm, tn), lambda i, j: (i, j)),
        out_shape=jax.ShapeDtypeStruct((m, n), out_dtype),
        compiler_params=_cparams(("parallel", "parallel")),
        name=name,
    )(x, w)


def _mm_nt_body(x_ref, wt_ref, o_ref):
    o_ref[...] = _dot_nt(x_ref[...], wt_ref[...].astype(BF16)).astype(o_ref.dtype)


def _matmul_nt(x, wt, *, n, tm, tn, out_dtype, name):
    m, k = x.shape
    return pl.pallas_call(
        _mm_nt_body,
        grid=(m // tm, n // tn),
        in_specs=[pl.BlockSpec((tm, k), lambda i, j: (i, 0)),
                  pl.BlockSpec((tn, k), lambda i, j: (j, 0))],
        out_specs=pl.BlockSpec((tm, tn), lambda i, j: (i, j)),
        out_shape=jax.ShapeDtypeStruct((m, n), out_dtype),
        compiler_params=_cparams(("parallel", "parallel")),
        name=name,
    )(x, wt)


def _gated_merge_body(x_ref, oa_ref, ob_ref, wga_ref, wgb_ref, ba_ref, bb_ref, wpa_ref, wpb_ref, o_ref):
    x = x_ref[...]
    ga = _sigmoid(_dot(x, wga_ref[...]) + ba_ref[...])
    gb = _sigmoid(_dot(x, wgb_ref[...]) + bb_ref[...])
    o = ga * _dot(oa_ref[...], wpa_ref[...]) + gb * _dot(ob_ref[...], wpb_ref[...])
    o_ref[...] = o.astype(o_ref.dtype)


def _gated_merge(x, oa, ob, wg, bg, wpa, wpb, *, tm, tn):
    m, d = x.shape
    ka = oa.shape[1]
    kb = ob.shape[1]
    n = wpa.shape[1]
    nb = n // tn
    row = lambda w: pl.BlockSpec((tm, w), lambda i, j: (i, 0))
    col = lambda k, off: pl.BlockSpec((k, tn), lambda i, j: (0, j + off))
    return pl.pallas_call(
        _gated_merge_body,
        grid=(m // tm, nb),
        in_specs=[row(d), row(ka), row(kb), col(d, 0), col(d, nb), col(1, 0), col(1, nb), col(ka, 0), col(kb, 0)],
        out_specs=pl.BlockSpec((tm, tn), lambda i, j: (i, j)),
        out_shape=jax.ShapeDtypeStruct((m, n), BF16),
        compiler_params=_cparams(("parallel", "parallel")),
        name="gated_merge",
    )(x, oa, ob, wg, wg, bg, bg, wpa, wpb)


def _swiglu_body(x_ref, wg_ref, wu_ref, o_ref):
    x = x_ref[...]
    g = _dot(x, wg_ref[...].astype(BF16))
    u = _dot(x, wu_ref[...].astype(BF16))
    o_ref[...] = (g * _sigmoid(g) * u).astype(o_ref.dtype)


def _swiglu(x, wg, wu, *, tm, tn):
    m, k = x.shape
    n = wg.shape[1]
    return pl.pallas_call(
        _swiglu_body,
        grid=(m // tm, n // tn),
        in_specs=[pl.BlockSpec((tm, k), lambda i, j: (i, 0)),
                  pl.BlockSpec((k, tn), lambda i, j: (0, j)),
                  pl.BlockSpec((k, tn), lambda i, j: (0, j))],
        out_specs=pl.BlockSpec((tm, tn), lambda i, j: (i, j)),
        out_shape=jax.ShapeDtypeStruct((m, n), BF16),
        compiler_params=_cparams(("parallel", "parallel")),
        name="swiglu",
    )(x, wg, wu)


def _add_norm_body(*refs, alpha, res_first, out_first, out_blocks):
    n_res = len(res_first)
    x_ref = refs[0]
    res_refs = refs[1:1 + n_res]
    g_ref, b_ref = refs[1 + n_res:3 + n_res]
    out_refs = refs[3 + n_res:]
    i = pl.program_id(0)
    r = res_refs[0][...]
    for k in range(1, n_res):
        r = jnp.where(i >= res_first[k], res_refs[k][...], r)
    x = alpha * r + x_ref[...]
    mu = jnp.mean(x, axis=-1, keepdims=True)
    d = x - mu
    var = jnp.mean(d * d, axis=-1, keepdims=True)
    y = d * lax.rsqrt(var + LN_EPS) * g_ref[...] + b_ref[...]
    for o_ref, first, nblk in zip(out_refs, out_first, out_blocks):
        @pl.when((i >= first) & (i < first + nblk))
        def _():
            o_ref[...] = y.astype(o_ref.dtype)


def _add_norm(x, resids, g, b, *, alpha, tm, outs):
    m, d = x.shape
    row_spec = lambda first, rows: pl.BlockSpec(
        (tm, d), lambda i: (jnp.clip(i - first // tm, 0, rows // tm - 1), 0))
    vec = pl.BlockSpec((1, d), lambda i: (0, 0))
    body = functools.partial(
        _add_norm_body, alpha=alpha,
        res_first=tuple(first // tm for _, first in resids),
        out_first=tuple(first // tm for _, first, _ in outs),
        out_blocks=tuple(rows // tm for rows, _, _ in outs))
    return pl.pallas_call(
        body,
        grid=(m // tm,),
        in_specs=[row_spec(0, m)] + [row_spec(first, r.shape[0]) for r, first in resids] + [vec, vec],
        out_specs=[row_spec(first, rows) for rows, first, _ in outs],
        out_shape=[jax.ShapeDtypeStruct((rows, d), dt) for rows, _, dt in outs],
        compiler_params=_cparams(("arbitrary",)),
        name="add_norm",
    )(x, *[r for r, _ in resids], g, b)


def _rwkv_prep_body(ur_ref, uk_ref, uv_ref, ul_ref, ar_ref, ak_ref, av_ref, al_ref, prev_ref, mu_ref,
                    w0_ref, w2_ref, a0_ref, a2_ref, g2_ref, kk_ref, ka_ref,
                    r_out, lw_out, k_out, v_out, kk_out, a_out, g_out, *, tm, width, chained):
    first = lax.broadcasted_iota(jnp.int32, (tm, 1), 0) == 0
    start = pl.program_id(0) == 0

    def mixed(u_ref, above_ref, lo, hi):
        u = u_ref[...]
        prev = prev_ref[:, lo:hi]
        if chained:
            prev = jnp.where(start, prev, above_ref[SUBLANES - 1:SUBLANES, :])
        shifted = jnp.where(first, jnp.broadcast_to(prev, u.shape), pltpu.roll(u, shift=1, axis=0))
        return u + (shifted - u) * mu_ref[:, lo:hi]

    w = width
    r = mixed(ur_ref, ar_ref, 0, w)
    k = mixed(uk_ref, ak_ref, w, 2 * w)
    v = mixed(uv_ref, av_ref, 2 * w, 3 * w)
    lora = mixed(ul_ref, al_ref, 3 * w, 3 * w + 2 * LORA_PAD + G_LORA)
    w_l = lora[:, 0:LORA_PAD]
    a_l = lora[:, LORA_PAD:2 * LORA_PAD]
    g_l = lora[:, 2 * LORA_PAD:]

    z = w0_ref[...] + _dot(jnp.tanh(w_l).astype(BF16), w2_ref[...])
    nz = -z
    softplus = jnp.maximum(nz, 0.0) + jnp.log(1.0 + jnp.exp(-jnp.abs(nz)))
    wdec = -softplus - 0.5
    a = _sigmoid(a0_ref[...] + _dot(a_l.astype(BF16), a2_ref[...]))
    g = _dot(_sigmoid(g_l).astype(BF16), g2_ref[...])

    r_out[...] = r
    lw_out[...] = -jnp.exp(wdec)
    k_out[...] = k * (1.0 + (a - 1.0) * ka_ref[...])
    v_out[...] = v
    kk_out[...] = k * kk_ref[...]
    a_out[...] = a
    g_out[...] = g


def _rwkv_prep(ua, ub, row0, rows, prev, mu, w0, w2, a0, a2, g2, k_k, k_a, *, tm, width, r_col, chained):
    nblk = rows // tm
    rb0 = row0 // tm
    cb = r_col // width
    lw = 2 * LORA_PAD + G_LORA
    lb = 2 * width // lw
    rw_w = prev.shape[2]
    vec = lambda n: pl.BlockSpec((1, n), lambda i: (0, 0))
    out = jax.ShapeDtypeStruct((rows, width), F32)
    tiles = tm // SUBLANES
    above = lambda i: jnp.maximum((rb0 + i) * tiles - 1, 0)
    blk = lambda wd, c: pl.BlockSpec((tm, wd), lambda i: (rb0 + i, c))
    abv = lambda wd, c: pl.BlockSpec((SUBLANES, wd), lambda i: (above(i), c))
    body = functools.partial(_rwkv_prep_body, tm=tm, width=width, chained=chained)
    return pl.pallas_call(
        body,
        grid=(nblk,),
        in_specs=[blk(width, cb), blk(width, 0), blk(width, 1), blk(lw, lb),
                  abv(width, cb), abv(width, 0), abv(width, 1), abv(lw, lb),
                  pl.BlockSpec((None, 1, rw_w), (lambda i: (0, 0, 0)) if chained else (lambda i: (i, 0, 0))),
                  vec(rw_w), vec(width),
                  pl.BlockSpec((LORA_PAD, width), lambda i: (0, 0)),
                  vec(width),
                  pl.BlockSpec((LORA_PAD, width), lambda i: (0, 0)),
                  pl.BlockSpec((G_LORA, width), lambda i: (0, 0)),
                  vec(width), vec(width)],
        out_specs=[pl.BlockSpec((tm, width), lambda i: (i, 0))] * 7,
        out_shape=[out] * 7,
        compiler_params=_cparams(("parallel",)),
        name="rwkv_prep",
    )(ua, ub, ub, ub, ua, ub, ub, ub, prev, mu, w0, w2, a0, a2, g2, k_k, k_a)


def _wkv_pair(r, lw, k, v, kk, a, g, st, lnw, lnb, rk, *, chunk):
    C = chunk
    C2 = 2 * C

    lane = lax.broadcasted_iota(jnp.int32, (C2, LANES), 1)
    rowi = lax.broadcasted_iota(jnp.int32, (C2, LANES), 0)
    own = (lane >= HS_R) == (rowi >= C)

    def twice(x):
        return jnp.concatenate([x, x], axis=0)

    def stack(x):
        return jnp.where(own, twice(x), 0.0)

    tr = lax.broadcasted_iota(jnp.int32, (C, C), 0)
    tc = lax.broadcasted_iota(jnp.int32, (C, C), 1)
    tri = (tr >= tc).astype(BF16)
    lw_hi = lw.astype(BF16)
    rem = lw - lw_hi.astype(F32)
    lw_mid = rem.astype(BF16)
    lw_lo = (rem - lw_mid.astype(F32)).astype(BF16)
    G = _dot(tri, lw_hi) + _dot(tri, lw_mid) + _dot(tri, lw_lo)
    yield None
    GC = G[C - 1:C, :]
    e_prev = jnp.exp(G - lw)
    e_neg = jnp.exp(-G)
    e_pos = jnp.exp(G)
    e_rest = jnp.exp(GC - G)
    e_all = jnp.exp(GC)

    kk2 = stack(kk)
    nrm = jnp.sqrt(jnp.sum(kk2 * kk2, axis=1, keepdims=True))
    kk2 = kk2 / jnp.maximum(nrm, 1e-12)

    A2 = (kk2 * twice(-e_prev)).astype(BF16)
    B2 = (kk2 * twice(a * e_neg)).astype(BF16)
    K2 = stack(k * e_neg).astype(BF16)
    R2 = stack(r * e_pos).astype(BF16)
    Bt2 = (kk2 * twice(a * e_rest)).astype(BF16)
    Kt2 = stack(k * e_rest).astype(BF16)
    V2f = stack(v)
    V2 = V2f.astype(BF16)

    AR = jnp.concatenate([A2, R2], axis=0)
    mr = lax.broadcasted_iota(jnp.int32, (2 * C2, 2 * C2), 0)
    mc = lax.broadcasted_iota(jnp.int32, (2 * C2, 2 * C2), 1)
    tr_r = mr % C2
    tr_c = mc % C2
    keep = tr_r + jnp.where(mr < C2, 0, 1) > tr_c
    m_all = jnp.where(keep, _dot_nt(AR, jnp.concatenate([B2, K2], axis=0)), 0.0)
    m_ab = m_all[:C2, :C2]
    m_rb = m_all[C2:, :C2].astype(BF16)
    m_xk = m_all[:, C2:].astype(BF16)
    eye = (lax.broadcasted_iota(jnp.int32, (C2, C2), 0) == lax.broadcasted_iota(jnp.int32, (C2, C2), 1)).astype(F32)
    yield None

    t_inv = eye + m_ab
    pb = m_ab.astype(BF16)
    p = _dot(pb, pb)
    yield None
    n = 2
    while 2 * n < C:
        pb = p.astype(BF16)
        both = _dot(jnp.concatenate([pb, t_inv.astype(BF16)], axis=0), pb)
        p = both[:C2]
        t_inv = t_inv + both[C2:]
        n *= 2
        yield None
    t_inv = t_inv + _dot(t_inv.astype(BF16), p.astype(BF16))
    yield None

    stb = st.astype(BF16)
    xy = _dot(AR, stb) + _dot(m_xk, V2)
    yield None
    u2 = _dot(t_inv.astype(BF16), xy[:C2].astype(BF16))
    yield None
    u2b = u2.astype(BF16)
    y2 = xy[C2:] + _dot(m_rb, u2b)

    decay_col = jnp.broadcast_to(e_all, (LANES, LANES)).T
    st_new = decay_col * st + _dot_tn(jnp.concatenate([Bt2, Kt2], axis=0), jnp.concatenate([u2b, V2], axis=0))
    yield None

    inv_hs = 1.0 / HS_R
    mean = jnp.sum(y2, axis=1, keepdims=True) * inv_hs
    d = jnp.where(own, y2 - mean, 0.0)
    var = jnp.sum(d * d, axis=1, keepdims=True) * inv_hs
    yn2 = d * lax.rsqrt(var + GN_EPS)
    bonus2 = jnp.sum(stack(r * k * rk), axis=1, keepdims=True) * V2f
    yn = yn2[:C] + yn2[C:]
    bonus = bonus2[:C] + bonus2[C:]
    yield (yn * lnw + lnb + bonus) * g, st_new


def _wkv_body(r_ref, lw_ref, k_ref, v_ref, kk_ref, a_ref, g_ref, s0_ref, lnw_ref, lnb_ref, rk_ref,
              o_ref, st_ref, st_scr, *, chunk, pairs):
    c = pl.program_id(2)

    @pl.when(c == 0)
    def _():
        st_scr[...] = s0_ref[...]

    lanes = [slice(pp * LANES, (pp + 1) * LANES) for pp in range(pairs)]
    gens = [_wkv_pair(r_ref[:, sl], lw_ref[:, sl], k_ref[:, sl], v_ref[:, sl], kk_ref[:, sl],
                      a_ref[:, sl], g_ref[:, sl], st_scr[pp], lnw_ref[:, sl], lnb_ref[:, sl],
                      rk_ref[:, sl], chunk=chunk) for pp, sl in enumerate(lanes)]
    results = [None] * pairs
    while any(res is None for res in results):
        results = [next(gen) for gen in gens]
    for pp, (o, st_new) in enumerate(results):
        st_scr[pp] = st_new
        o_ref[:, lanes[pp]] = o.astype(o_ref.dtype)

    @pl.when(c == pl.num_programs(2) - 1)
    def _():
        st_ref[...] = st_scr[...]


def _wkv(r, lw, k, v, kk, a, g, s0, lnw, lnb, rk, *, seqs, chunk, pairs):
    rows, width = r.shape
    t = rows // seqs
    nchunk = t // chunk
    npair = width // LANES
    wblk = pairs * LANES
    blk = pl.BlockSpec((chunk, wblk), lambda b, p, c: (b * nchunk + c, p))
    vec = pl.BlockSpec((1, wblk), lambda b, p, c: (0, p))
    stspec = pl.BlockSpec((None, pairs, LANES, LANES), lambda b, p, c: (b, p, 0, 0))
    return pl.pallas_call(
        functools.partial(_wkv_body, chunk=chunk, pairs=pairs),
        grid=(seqs, npair // pairs, nchunk),
        in_specs=[blk] * 7 + [stspec, vec, vec, vec],
        out_specs=[blk, stspec],
        out_shape=[jax.ShapeDtypeStruct((rows, width), BF16),
                   jax.ShapeDtypeStruct((seqs, npair, LANES, LANES), F32)],
        scratch_shapes=[pltpu.VMEM((pairs, LANES, LANES), F32)],
        compiler_params=_cparams(("parallel", "parallel", "arbitrary")),
        name="wkv7_chunked",
    )(r, lw, k, v, kk, a, g, s0, lnw, lnb, rk)


def _two_maps(q):
    lane = lax.broadcasted_iota(jnp.int32, q.shape, 1)
    q = q * (ATTN_SCALE * LOG2E)
    return jnp.where(lane < DK, q, 0.0), jnp.where(lane >= DK, q, 0.0)


def _attn_prompt_body(qi_ref, ki_ref, lam_ref, q_ref, k_ref, v_ref, sw_ref, o_ref,
                      q2_scr, m_scr, l_scr, acc_scr, *, tq, post_scale):
    step = pl.program_id(1)
    qi = qi_ref[step]
    ki = ki_ref[step]

    @pl.when(ki == 0)
    def _():
        q1, q2 = _two_maps(q_ref[...])
        q2_scr[...] = jnp.concatenate([q1, q2], axis=0).astype(BF16)
        m_scr[...] = jnp.full_like(m_scr, NEG_INF)
        l_scr[...] = jnp.zeros_like(l_scr)
        acc_scr[...] = jnp.zeros_like(acc_scr)

    def update(diagonal):
        for kb in range(tq // ATTN_KEY_SUB):
            k0 = kb * ATTN_KEY_SUB
            kblk = k_ref[k0:k0 + ATTN_KEY_SUB, :].astype(BF16)
            vtblk = v_ref[k0:k0 + ATTN_KEY_SUB, :].T.astype(BF16)
            for c in range(2 * tq // ATTN_Q_COLS):
                q0 = (c * ATTN_Q_COLS) % tq
                if diagonal and k0 > q0 + ATTN_Q_COLS - 1:
                    continue
                cs = slice(c * ATTN_Q_COLS, (c + 1) * ATTN_Q_COLS)
                st = _dot_nt(kblk, q2_scr[cs, :])
                if diagonal and k0 + ATTN_KEY_SUB - 1 > q0:
                    krow = k0 + lax.broadcasted_iota(jnp.int32, st.shape, 0)
                    qcol = q0 + lax.broadcasted_iota(jnp.int32, st.shape, 1)
                    st = jnp.where(krow <= qcol, st, NEG_INF)
                m_old = m_scr[:, cs]
                m_new = jnp.maximum(m_old, jnp.max(st, axis=0, keepdims=True))
                alpha = jnp.exp2(m_old - m_new)
                p = jnp.exp2(st - m_new)
                l_scr[:, cs] = alpha * l_scr[:, cs] + jnp.sum(p, axis=0, keepdims=True)
                acc_scr[:, cs] = alpha * acc_scr[:, cs] + _dot(vtblk, p.astype(BF16))
                m_scr[:, cs] = m_new

    @pl.when(ki < qi)
    def _():
        update(False)

    @pl.when(ki == qi)
    def _():
        update(True)
        acc = acc_scr[...]
        l = l_scr[...]
        ot = acc[:, :tq] / l[:, :tq] - lam_ref[:, 0:1] * (acc[:, tq:] / l[:, tq:])
        ms = jnp.mean(ot * ot, axis=0, keepdims=True)
        on = ot * lax.rsqrt(ms + SUBLN_EPS)
        o_ref[...] = (on.T * sw_ref[...] * post_scale).astype(o_ref.dtype)


def _attn_prompt(u, lam, subw, *, t, heads, tq, post_scale):
    nq = t // tq
    pairs = [(i, j) for i in range(nq) for j in range(i + 1)]
    qi_tab = jnp.array([p[0] for p in pairs], jnp.int32)
    ki_tab = jnp.array([p[1] for p in pairs], jnp.int32)
    vec = pl.BlockSpec((1, LANES), lambda h, s, qi, ki: (0, 0))
    grid_spec = pltpu.PrefetchScalarGridSpec(
        num_scalar_prefetch=2,
        grid=(heads, len(pairs)),
        in_specs=[vec,
                  pl.BlockSpec((tq, LANES), lambda h, s, qi, ki: (qi[s], h)),
                  pl.BlockSpec((tq, LANES), lambda h, s, qi, ki: (ki[s], heads + h)),
                  pl.BlockSpec((tq, LANES), lambda h, s, qi, ki: (ki[s], 2 * heads + h)),
                  vec],
        out_specs=pl.BlockSpec((tq, LANES), lambda h, s, qi, ki: (qi[s], h)),
        scratch_shapes=[pltpu.VMEM((2 * tq, LANES), BF16),
                        pltpu.VMEM((1, 2 * tq), F32),
                        pltpu.VMEM((1, 2 * tq), F32),
                        pltpu.VMEM((LANES, 2 * tq), F32)],
    )
    return pl.pallas_call(
        functools.partial(_attn_prompt_body, tq=tq, post_scale=post_scale),
        grid_spec=grid_spec,
        out_shape=jax.ShapeDtypeStruct((t, heads * LANES), BF16),
        compiler_params=_cparams(("parallel", "arbitrary")),
        name="diff_attn_prompt",
    )(qi_tab, ki_tab, lam, u, u, u, subw)


def _attn_sample_body(pt_ref, lam_ref, q_ref, *refs, heads, tn, post_scale, pages):
    groups = heads // HEAD_TILE
    kc_refs = refs[:pages]
    vc_refs = refs[pages:2 * pages]
    kn_ref, vn_ref, bias_ref, biasn_ref, sw_ref, o_ref, q2_scr, m_scr, l_scr, acc_scr = refs[2 * pages:]
    j = pl.program_id(1)
    n_steps = pl.num_programs(1) - 1
    half = HEAD_TILE * tn

    @pl.when(j == 0)
    def _():
        for h in range(heads):
            g, hh = divmod(h, HEAD_TILE)
            q1, q2 = _two_maps(q_ref[:, h * LANES:(h + 1) * LANES])
            q2_scr[g, hh * tn:(hh + 1) * tn] = q1
            q2_scr[g, half + hh * tn:half + (hh + 1) * tn] = q2
        m_scr[...] = jnp.full_like(m_scr, NEG_INF)
        l_scr[...] = jnp.zeros_like(l_scr)
        acc_scr[...] = jnp.zeros_like(acc_scr)

    def attend(k_pages, v_pages, b_ref):
        bias = b_ref[...]
        def flat(ref, g):
            tile = ref[:, g * HEAD_TILE:(g + 1) * HEAD_TILE, :]
            return tile.reshape(PAGE * HEAD_TILE, LANES).astype(BF16)

        for g in range(groups):
            q2 = q2_scr[g].astype(BF16)
            ss = [_dot_nt(q2, flat(kp, g)) + bias for kp in k_pages]
            m_old = m_scr[g]
            m_new = m_old
            for s in ss:
                m_new = jnp.maximum(m_new, jnp.max(s, axis=1, keepdims=True))
            alpha = jnp.exp2(m_old - m_new)
            l_new = alpha * l_scr[g]
            acc_new = alpha * acc_scr[g]
            for s, vp in zip(ss, v_pages):
                p = jnp.exp2(s - m_new)
                l_new = l_new + jnp.sum(p, axis=1, keepdims=True)
                acc_new = acc_new + _dot(p.astype(BF16), flat(vp, g))
            l_scr[g] = l_new
            acc_scr[g] = acc_new
            m_scr[g] = m_new

    @pl.when(j < n_steps)
    def _():
        attend(kc_refs, vc_refs, bias_ref)

    @pl.when(j == n_steps)
    def _():
        attend([kn_ref], [vn_ref], biasn_ref)
        for g in range(groups):
            acc = acc_scr[g]
            l = l_scr[g]
            o = acc[:half] / l[:half] - lam_ref[...] * (acc[half:] / l[half:])
            ms = jnp.mean(o * o, axis=-1, keepdims=True)
            o = o * lax.rsqrt(ms + SUBLN_EPS) * sw_ref[...] * post_scale
            for hh in range(HEAD_TILE):
                h = g * HEAD_TILE + hh
                o_ref[:, h * LANES:(h + 1) * LANES] = o[hh * tn:(hh + 1) * tn]


def _sample_bias(tn, causal):
    r = jnp.arange(2 * HEAD_TILE * tn)[:, None]
    c = jnp.arange(PAGE * HEAD_TILE)[None, :]
    ok = (c % HEAD_TILE) == ((r % (HEAD_TILE * tn)) // tn)
    if causal:
        ok = ok & ((c // HEAD_TILE) <= (r % tn))
    return jnp.where(ok, 0.0, NEG_INF).astype(F32)


def _attn_sample(q, cache_k, cache_v, k_new, v_new, page_table, lam, subw, *, layer, post_scale):
    b, n_pages = page_table.shape
    heads = cache_k.shape[3]
    groups = heads // HEAD_TILE
    tn = q.shape[0] // b
    rows = 2 * HEAD_TILE * tn
    pt = page_table.reshape(-1)
    pages = _pick_tile(n_pages, (SAMPLE_PAGES, 1))
    n_steps = n_pages // pages

    def page_spec(k):
        def page_map(i, j, pt_ref):
            return (layer, pt_ref[i * n_pages + jnp.minimum(j, n_steps - 1) * pages + k], 0, 0, 0)
        return pl.BlockSpec((None, None, PAGE, heads, LANES), page_map)

    page_specs = [page_spec(k) for k in range(pages)]
    new_spec = pl.BlockSpec((None, PAGE, heads, LANES), lambda i, j, pt_ref: (i, 0, 0, 0))
    vec = pl.BlockSpec((1, LANES), lambda i, j, pt_ref: (0, 0))
    bias_spec = pl.BlockSpec((rows, PAGE * HEAD_TILE), lambda i, j, pt_ref: (0, 0))
    grid_spec = pltpu.PrefetchScalarGridSpec(
        num_scalar_prefetch=1,
        grid=(b, n_steps + 1),
        in_specs=[vec,
                  pl.BlockSpec((tn, heads * LANES), lambda i, j, pt_ref: (i, 0)),
                  *page_specs, *page_specs, new_spec, new_spec, bias_spec, bias_spec, vec],
        out_specs=pl.BlockSpec((tn, heads * LANES), lambda i, j, pt_ref: (i, 0)),
        scratch_shapes=[pltpu.VMEM((groups, rows, LANES), F32),
                        pltpu.VMEM((groups, rows, 1), F32),
                        pltpu.VMEM((groups, rows, 1), F32),
                        pltpu.VMEM((groups, rows, LANES), F32)],
    )
    return pl.pallas_call(
        functools.partial(_attn_sample_body, heads=heads, tn=tn, post_scale=post_scale, pages=pages),
        grid_spec=grid_spec,
        out_shape=jax.ShapeDtypeStruct((b * tn, heads * LANES), F32),
        compiler_params=_cparams(("parallel", "arbitrary")),
        name="diff_attn_sample",
    )(pt, lam, q, *([cache_k] * pages), *([cache_v] * pages), k_new, v_new,
      _sample_bias(tn, False), _sample_bias(tn, True), subw)


def _pick_tile(n, candidates):
    for c in candidates:
        if n % c == 0:
            return c
    raise ValueError(f"no tile for {n}")


def _pair_states(s):
    b, h, _, _ = s.shape
    st = jnp.swapaxes(s, -1, -2).reshape(b, h // 2, 2, HS_R, HS_R)
    z = jnp.zeros_like(st[:, :, 0])
    top = jnp.concatenate([st[:, :, 0], z], axis=-1)
    bot = jnp.concatenate([z, st[:, :, 1]], axis=-1)
    return jnp.concatenate([top, bot], axis=-2)


def _head_states(sp):
    b, hp, _, _ = sp.shape
    s0 = sp[:, :, :HS_R, :HS_R]
    s1 = sp[:, :, HS_R:, HS_R:]
    st = jnp.stack([s0, s1], axis=2).reshape(b, 2 * hp, HS_R, HS_R)
    return jnp.swapaxes(st, -1, -2)


def _layer(l, depth, xp, xs, cache_k, cache_v, state_wkv, state_shift, page_table,
           w_in, mu_shift, w_decay0, w_decay2, a0, a2, g2, k_k, k_a, r_k, lnx_w, lnx_b,
           lambda_q1, lambda_k1, lambda_q2, lambda_k2, subln_w,
           w_pa, w_pb, w_gate, b_gate, w_o, ln1_g, ln1_b,
           w_ffn_gate, w_ffn_up, w_ffn_down, ln2_g, ln2_b):
    bp, sp, d = xp.shape
    bs, ts, _ = xs.shape
    assert bp == 1
    heads = cache_k.shape[3]
    att_w = heads * LANES
    rw_w = w_decay0.shape[1]
    h_r = rw_w // HS_R
    rw_in = 3 * rw_w + W_LORA + A_LORA + G_LORA
    assert w_in.shape[2] == 3 * att_w + rw_in and att_w == rw_w
    alpha = (2 * depth) ** 0.25
    lam_init = 0.8 - 0.6 * math.exp(-0.3 * l)
    post_scale = 1.0 - lam_init

    lam = (jnp.exp(jnp.sum(lambda_q1[l] * lambda_k1[l])) - jnp.exp(jnp.sum(lambda_q2[l] * lambda_k2[l]))
           + lam_init)
    lam = jnp.broadcast_to(lam.astype(F32), (1, LANES))

    n_p = bp * sp
    n_s = bs * ts
    m = n_p + n_s
    xp2 = xp.reshape(n_p, d)
    xs2 = xs.reshape(n_s, d)
    xb = jnp.concatenate([xp2.astype(BF16), xs2.astype(BF16)], axis=0)

    o_rw = 3 * att_w
    c_wl = o_rw + rw_w
    c_k = c_wl + W_LORA
    c_al = c_k + 2 * rw_w
    c_gl = c_al + A_LORA
    lora_w = 2 * LORA_PAD + G_LORA
    rw_cols = 3 * rw_w + lora_w

    def rest_cols(t, off, axis=1):
        take = lambda a, b: lax.slice_in_dim(t, a - off, b - off, axis=axis)
        zshape = list(t.shape)
        zshape[axis] = LORA_PAD - W_LORA
        z = jnp.zeros(zshape, t.dtype)
        return jnp.concatenate([take(c_k, c_al), take(c_wl, c_k), z, take(c_al, c_gl), z,
                                take(c_gl, c_gl + G_LORA)], axis=axis)

    def prep_cols(t):
        return jnp.concatenate([t[:, :rw_w], rest_cols(t, o_rw)], axis=1)

    wt = w_in[l].T
    wt_rest = rest_cols(wt, 0, axis=0)
    tm = _pick_tile(m, (768, 512, 384, 256, 128, 64, 32, 16, 8))
    tm_big = _pick_tile(m, (1408, 768, 512, 384, 256, 128, 64, 32, 16, 8))
    ua = _matmul_nt(xb, wt, n=c_wl, tm=tm_big, tn=_pick_tile(c_wl, (256, 128)), out_dtype=F32,
                    name="in_proj_qkvr")
    ub = _matmul_nt(xb, wt_rest, n=wt_rest.shape[0], tm=tm, tn=_pick_tile(wt_rest.shape[0], (512, 256, 128)),
                    out_dtype=F32, name="in_proj_rest")

    mu = prep_cols(mu_shift[l][None, :])
    pad_rows = lambda t: jnp.concatenate([t, jnp.zeros((LORA_PAD - t.shape[0], t.shape[1]), t.dtype)], axis=0)
    w2 = pad_rows(w_decay2[l]).astype(BF16)
    a2b = pad_rows(a2[l]).astype(BF16)
    g2b = g2[l].astype(BF16)
    row = lambda t: t[l][None, :]
    prep_args = (mu, row(w_decay0), w2, row(a0), a2b, g2b, row(k_k), row(k_a))

    tp = _pick_tile(sp, (128, 64, 32, 16, 8))
    prev_p = jnp.zeros((bp, 1, rw_cols), F32)
    prep_p = _rwkv_prep(ua, ub, 0, sp, prev_p, *prep_args, tm=tp, width=rw_w, r_col=o_rw, chained=True)

    prev_s = prep_cols(state_shift[l]).reshape(bs, 1, rw_cols)
    prep_s = _rwkv_prep(ua, ub, n_p, n_s, prev_s, *prep_args, tm=ts, width=rw_w, r_col=o_rw, chained=False)

    wkv_params = (row(lnx_w), row(lnx_b), r_k[l].reshape(1, rw_w))
    s0_p = jnp.zeros((bp, h_r // 2, LANES, LANES), F32)
    pairs = _pick_tile(h_r // 2, (WKV_PAIRS, 2, 1))
    o_rw_p, st_p = _wkv(*prep_p, s0_p, *wkv_params, seqs=bp, chunk=WKV_CHUNK, pairs=pairs)

    t_pad = -(-ts // WKV_CHUNK) * WKV_CHUNK
    pad_t = lambda t: jnp.pad(t.reshape(bs, ts, rw_w), ((0, 0), (0, t_pad - ts), (0, 0))).reshape(bs * t_pad, rw_w)
    s0_s = _pair_states(state_wkv[l].astype(F32))
    o_rw_s, st_s = _wkv(*[pad_t(t) for t in prep_s], s0_s, *wkv_params, seqs=bs, chunk=WKV_CHUNK, pairs=pairs)
    o_rw_s = o_rw_s.reshape(bs, t_pad, rw_w)[:, :ts].reshape(n_s, rw_w)
    o_rwkv = jnp.concatenate([o_rw_p, o_rw_s], axis=0)

    new_wkv_p = _head_states(st_p).astype(state_wkv.dtype)
    new_wkv_s = _head_states(st_s).astype(state_wkv.dtype)

    def shift_row(rows):
        ta = ua[rows]
        tb = ub[rows]
        w_ = rw_w
        return jnp.concatenate([ta[:, o_rw:], tb[:, 2 * w_:2 * w_ + W_LORA], tb[:, :2 * w_],
                                tb[:, 2 * w_ + LORA_PAD:2 * w_ + LORA_PAD + A_LORA],
                                tb[:, 2 * w_ + 2 * LORA_PAD:]], axis=1)

    new_shift_p = shift_row(jnp.array([sp - 1]))
    new_shift_s = shift_row(n_p + ts - 1 + ts * jnp.arange(bs))

    subw = subln_w[l][None, :]
    tq = _pick_tile(sp, (2048, 1024, 512, 256, 128))
    o_att_p = _attn_prompt(ua, lam, subw, t=sp, heads=heads, tq=tq, post_scale=post_scale)

    q_s = lax.slice(ua, (n_p, 0), (m, att_w))
    k_s = lax.slice(ua, (n_p, att_w), (m, 2 * att_w)).reshape(bs, ts, heads, DV)
    v_s = lax.slice(ua, (n_p, 2 * att_w), (m, 3 * att_w)).reshape(bs, ts, heads, DV)
    pad_new = lambda t: jnp.pad(t, ((0, 0), (0, PAGE - ts), (0, 0), (0, 0)))
    o_att_s = _attn_sample(q_s, cache_k, cache_v, pad_new(k_s), pad_new(v_s), page_table, lam, subw,
                           layer=l, post_scale=post_scale)
    o_att = jnp.concatenate([o_att_p, o_att_s.astype(BF16)], axis=0)

    wg = w_gate[l].astype(BF16)
    bg = b_gate[l][None, :]
    mixin = _gated_merge(xb, o_att, o_rwkv, wg, bg, w_pa[l].astype(BF16), w_pb[l].astype(BF16),
                         tm=tm, tn=_pick_tile(d, (256, 128)))
    mix = _matmul(mixin, w_o[l], tm=tm, tn=_pick_tile(d, (512, 256, 128)), out_dtype=F32, name="out_proj")
    tl = _pick_tile(math.gcd(n_p, n_s), (256, 128, 64, 32, 16, 8))
    x1, x1b = _add_norm(mix, [(xp2, 0), (xs2, n_p)], row(ln1_g), row(ln1_b), alpha=alpha, tm=tl,
                        outs=[(m, 0, F32), (m, 0, BF16)])

    d_ff = w_ffn_gate.shape[2]
    hmid = _swiglu(x1b, w_ffn_gate[l], w_ffn_up[l], tm=tm_big, tn=_pick_tile(d_ff, (256, 128)))
    down = _matmul(hmid, w_ffn_down[l].astype(BF16), tm=tm, tn=_pick_tile(d, (256, 128)), out_dtype=F32,
                   name="ffn_down")
    y_p, y_s = _add_norm(down, [(x1, 0)], row(ln2_g), row(ln2_b), alpha=alpha, tm=tl,
                         outs=[(n_p, 0, F32), (n_s, n_p, F32)])

    k_all = lax.slice(ua, (0, att_w), (m, 2 * att_w))
    v_all = lax.slice(ua, (0, 2 * att_w), (m, 3 * att_w))
    outs = dict(
        xp=y_p.reshape(bp, sp, d), xs=y_s.reshape(bs, ts, d),
        kp=k_all[:n_p].reshape(bp, sp, heads, DV), vp=v_all[:n_p].reshape(bp, sp, heads, DV),
        wp=new_wkv_p, sp=new_shift_p,
        ks=k_s, vs=v_s, ws=new_wkv_s, ss=new_shift_s)
    return outs


def kernel(x_prompt, x_sample, cache_k, cache_v, state_wkv, state_shift, page_table, w_in, mu_shift, w_decay0, w_decay2, a0, a2, g2, k_k, k_a, r_k, lnx_w, lnx_b, lambda_q1, lambda_k1, lambda_q2, lambda_k2, subln_w, w_pa, w_pb, w_gate, b_gate, w_o, ln1_g, ln1_b, w_ffn_gate, w_ffn_up, w_ffn_down, ln2_g, ln2_b):
    depth = w_in.shape[0]
    xp, xs = x_prompt, x_sample
    acc = {n: [] for n in ("kp", "vp", "wp", "sp", "ks", "vs", "ws", "ss")}
    for l in range(depth):
        o = _layer(l, depth, xp, xs, cache_k, cache_v, state_wkv, state_shift, page_table,
                   w_in, mu_shift, w_decay0, w_decay2, a0, a2, g2, k_k, k_a, r_k, lnx_w, lnx_b,
                   lambda_q1, lambda_k1, lambda_q2, lambda_k2, subln_w,
                   w_pa, w_pb, w_gate, b_gate, w_o, ln1_g, ln1_b,
                   w_ffn_gate, w_ffn_up, w_ffn_down, ln2_g, ln2_b)
        xp, xs = o["xp"], o["xs"]
        for n in acc:
            acc[n].append(o[n])
    st = lambda n: jnp.stack(acc[n])
    return (xp, xs, st("kp"), st("vp"), st("wp"), st("sp"), st("ks"), st("vs"), st("ws"), st("ss"))
```

```python
import functools
import math

import jax
import jax.numpy as jnp
from jax import lax
from jax.experimental import pallas as pl
from jax.experimental.pallas import tpu as pltpu

F32 = jnp.float32
BF16 = jnp.bfloat16

LANES = 128
SUBLANES = 8
VMEM_LIMIT = 56 * 1024 * 1024

DK = 64
DV = 128
HS_R = 64
W_LORA = 96
A_LORA = 96
G_LORA = 256
LORA_PAD = 128
ATTN_SCALE = DK ** -0.5
LOG2E = math.log2(math.e)
ATTN_KEY_SUB = 256
ATTN_Q_COLS = 128
SAMPLE_PAGES = 4
SUBLN_EPS = 1e-5
NEG_INF = -1e30
GN_EPS = 64e-5
LN_EPS = 1e-5
PAGE = 128
HEAD_TILE = 8
WKV_CHUNK = 64
WKV_PAIRS = 16


def _cparams(sem):
    return pltpu.CompilerParams(dimension_semantics=sem, vmem_limit_bytes=VMEM_LIMIT)


def _dot(a, b):
    return jnp.dot(a, b, preferred_element_type=F32)


def _dot_nt(a, b):
    return lax.dot_general(a, b, (((1,), (1,)), ((), ())), preferred_element_type=F32)


def _dot_tn(a, b):
    return lax.dot_general(a, b, (((0,), (0,)), ((), ())), preferred_element_type=F32)


def _sigmoid(x):
    return 1.0 / (1.0 + jnp.exp(-x))


def _mm_body(x_ref, w_ref, o_ref):
    o_ref[...] = _dot(x_ref[...], w_ref[...].astype(BF16)).astype(o_ref.dtype)


def _matmul(x, w, *, tm, tn, out_dtype, name, n=None):
    m, k = x.shape
    n = w.shape[1] if n is None else n
    return pl.pallas_call(
        _mm_body,
        grid=(m // tm, n // tn),
        in_specs=[pl.BlockSpec((tm, k), lambda i, j: (i, 0)),
                  pl.BlockSpec((k, tn), lambda i, j: (0, j))],
        out_specs=pl.BlockSpec((tm, tn), lambda i, j: (i, j)),
        out_shape=jax.ShapeDtypeStruct((m, n), out_dtype),
        compiler_params=_cparams(("parallel", "parallel")),
        name=name,
    )(x, w)


def _mm_nt_body(x_ref, wt_ref, o_ref):
    o_ref[...] = _dot_nt(x_ref[...], wt_ref[...].astype(BF16)).astype(o_ref.dtype)


def _matmul_nt(x, wt, *, n, tm, tn, out_dtype, name):
    m, k = x.shape
    return pl.pallas_call(
        _mm_nt_body,
        grid=(m // tm, n // tn),
        in_specs=[pl.BlockSpec((tm, k), lambda i, j: (i, 0)),
                  pl.BlockSpec((tn, k), lambda i, j: (j, 0))],
        out_specs=pl.BlockSpec((tm, tn), lambda i, j: (i, j)),
        out_shape=jax.ShapeDtypeStruct((m, n), out_dtype),
        compiler_params=_cparams(("parallel", "parallel")),
        name=name,
    )(x, wt)


def _gated_merge_body(x_ref, oa_ref, ob_ref, wga_ref, wgb_ref, ba_ref, bb_ref, wpa_ref, wpb_ref, o_ref):
    x = x_ref[...]
    ga = _sigmoid(_dot(x, wga_ref[...]) + ba_ref[...])
    gb = _sigmoid(_dot(x, wgb_ref[...]) + bb_ref[...])
    o = ga * _dot(oa_ref[...], wpa_ref[...]) + gb * _dot(ob_ref[...], wpb_ref[...])
    o_ref[...] = o.astype(o_ref.dtype)


def _gated_merge(x, oa, ob, wg, bg, wpa, wpb, *, tm, tn):
    m, d = x.shape
    ka = oa.shape[1]
    kb = ob.shape[1]
    n = wpa.shape[1]
    nb = n // tn
    row = lambda w: pl.BlockSpec((tm, w), lambda i, j: (i, 0))
    col = lambda k, off: pl.BlockSpec((k, tn), lambda i, j: (0, j + off))
    return pl.pallas_call(
        _gated_merge_body,
        grid=(m // tm, nb),
        in_specs=[row(d), row(ka), row(kb), col(d, 0), col(d, nb), col(1, 0), col(1, nb), col(ka, 0), col(kb, 0)],
        out_specs=pl.BlockSpec((tm, tn), lambda i, j: (i, j)),
        out_shape=jax.ShapeDtypeStruct((m, n), BF16),
        compiler_params=_cparams(("parallel", "parallel")),
        name="gated_merge",
    )(x, oa, ob, wg, wg, bg, bg, wpa, wpb)


def _swiglu_body(x_ref, wg_ref, wu_ref, o_ref):
    x = x_ref[...]
    g = _dot(x, wg_ref[...].astype(BF16))
    u = _dot(x, wu_ref[...].astype(BF16))
    o_ref[...] = (g * _sigmoid(g) * u).astype(o_ref.dtype)


def _swiglu(x, wg, wu, *, tm, tn):
    m, k = x.shape
    n = wg.shape[1]
    return pl.pallas_call(
        _swiglu_body,
        grid=(m // tm, n // tn),
        in_specs=[pl.BlockSpec((tm, k), lambda i, j: (i, 0)),
                  pl.BlockSpec((k, tn), lambda i, j: (0, j)),
                  pl.BlockSpec((k, tn), lambda i, j: (0, j))],
        out_specs=pl.BlockSpec((tm, tn), lambda i, j: (i, j)),
        out_shape=jax.ShapeDtypeStruct((m, n), BF16),
        compiler_params=_cparams(("parallel", "parallel")),
        name="swiglu",
    )(x, wg, wu)


def _add_norm_body(*refs, alpha, res_first, out_first, out_blocks):
    n_res = len(res_first)
    x_ref = refs[0]
    res_refs = refs[1:1 + n_res]
    g_ref, b_ref = refs[1 + n_res:3 + n_res]
    out_refs = refs[3 + n_res:]
    i = pl.program_id(0)
    r = res_refs[0][...]
    for k in range(1, n_res):
        r = jnp.where(i >= res_first[k], res_refs[k][...], r)
    x = alpha * r + x_ref[...]
    mu = jnp.mean(x, axis=-1, keepdims=True)
    d = x - mu
    var = jnp.mean(d * d, axis=-1, keepdims=True)
    y = d * lax.rsqrt(var + LN_EPS) * g_ref[...] + b_ref[...]
    for o_ref, first, nblk in zip(out_refs, out_first, out_blocks):
        @pl.when((i >= first) & (i < first + nblk))
        def _():
            o_ref[...] = y.astype(o_ref.dtype)


def _add_norm(x, resids, g, b, *, alpha, tm, outs):
    m, d = x.shape
    row_spec = lambda first, rows: pl.BlockSpec(
        (tm, d), lambda i: (jnp.clip(i - first // tm, 0, rows // tm - 1), 0))
    vec = pl.BlockSpec((1, d), lambda i: (0, 0))
    body = functools.partial(
        _add_norm_body, alpha=alpha,
        res_first=tuple(first // tm for _, first in resids),
        out_first=tuple(first // tm for _, first, _ in outs),
        out_blocks=tuple(rows // tm for rows, _, _ in outs))
    return pl.pallas_call(
        body,
        grid=(m // tm,),
        in_specs=[row_spec(0, m)] + [row_spec(first, r.shape[0]) for r, first in resids] + [vec, vec],
        out_specs=[row_spec(first, rows) for rows, first, _ in outs],
        out_shape=[jax.ShapeDtypeStruct((rows, d), dt) for rows, _, dt in outs],
        compiler_params=_cparams(("arbitrary",)),
        name="add_norm",
    )(x, *[r for r, _ in resids], g, b)


def _rwkv_prep_body(ur_ref, uk_ref, uv_ref, ul_ref, ar_ref, ak_ref, av_ref, al_ref, prev_ref, mu_ref,
                    w0_ref, w2_ref, a0_ref, a2_ref, g2_ref, kk_ref, ka_ref,
                    r_out, lw_out, k_out, v_out, kk_out, a_out, g_out, *, tm, width, chained):
    first = lax.broadcasted_iota(jnp.int32, (tm, 1), 0) == 0
    start = pl.program_id(0) == 0

    def mixed(u_ref, above_ref, lo, hi):
        u = u_ref[...]
        prev = prev_ref[:, lo:hi]
        if chained:
            prev = jnp.where(start, prev, above_ref[SUBLANES - 1:SUBLANES, :])
        shifted = jnp.where(first, jnp.broadcast_to(prev, u.shape), pltpu.roll(u, shift=1, axis=0))
        return u + (shifted - u) * mu_ref[:, lo:hi]

    w = width
    r = mixed(ur_ref, ar_ref, 0, w)
    k = mixed(uk_ref, ak_ref, w, 2 * w)
    v = mixed(uv_ref, av_ref, 2 * w, 3 * w)
    lora = mixed(ul_ref, al_ref, 3 * w, 3 * w + 2 * LORA_PAD + G_LORA)
    w_l = lora[:, 0:LORA_PAD]
    a_l = lora[:, LORA_PAD:2 * LORA_PAD]
    g_l = lora[:, 2 * LORA_PAD:]

    z = w0_ref[...] + _dot(jnp.tanh(w_l).astype(BF16), w2_ref[...])
    nz = -z
    softplus = jnp.maximum(nz, 0.0) + jnp.log(1.0 + jnp.exp(-jnp.abs(nz)))
    wdec = -softplus - 0.5
    a = _sigmoid(a0_ref[...] + _dot(a_l.astype(BF16), a2_ref[...]))
    g = _dot(_sigmoid(g_l).astype(BF16), g2_ref[...])

    r_out[...] = r.astype(r_out.dtype)
    lw_out[...] = -jnp.exp(wdec)
    k_out[...] = (k * (1.0 + (a - 1.0) * ka_ref[...])).astype(k_out.dtype)
    v_out[...] = v.astype(v_out.dtype)
    kk_out[...] = (k * kk_ref[...]).astype(kk_out.dtype)
    a_out[...] = a.astype(a_out.dtype)
    g_out[...] = g.astype(g_out.dtype)


def _rwkv_prep(ua, ub, row0, rows, prev, mu, w0, w2, a0, a2, g2, k_k, k_a, *, tm, width, r_col, chained, store):
    nblk = rows // tm
    rb0 = row0 // tm
    cb = r_col // width
    lw = 2 * LORA_PAD + G_LORA
    lb = 2 * width // lw
    rw_w = prev.shape[2]
    vec = lambda n: pl.BlockSpec((1, n), lambda i: (0, 0))
    out = lambda dt: jax.ShapeDtypeStruct((rows, width), dt)
    tiles = tm // SUBLANES
    above = lambda i: jnp.maximum((rb0 + i) * tiles - 1, 0)
    blk = lambda wd, c: pl.BlockSpec((tm, wd), lambda i: (rb0 + i, c))
    abv = lambda wd, c: pl.BlockSpec((SUBLANES, wd), lambda i: (above(i), c))
    body = functools.partial(_rwkv_prep_body, tm=tm, width=width, chained=chained)
    return pl.pallas_call(
        body,
        grid=(nblk,),
        in_specs=[blk(width, cb), blk(width, 0), blk(width, 1), blk(lw, lb),
                  abv(width, cb), abv(width, 0), abv(width, 1), abv(lw, lb),
                  pl.BlockSpec((None, 1, rw_w), (lambda i: (0, 0, 0)) if chained else (lambda i: (i, 0, 0))),
                  vec(rw_w), vec(width),
                  pl.BlockSpec((LORA_PAD, width), lambda i: (0, 0)),
                  vec(width),
                  pl.BlockSpec((LORA_PAD, width), lambda i: (0, 0)),
                  pl.BlockSpec((G_LORA, width), lambda i: (0, 0)),
                  vec(width), vec(width)],
        out_specs=[pl.BlockSpec((tm, width), lambda i: (i, 0))] * 7,
        out_shape=[out(store), out(F32)] + [out(store)] * 5,
        compiler_params=_cparams(("parallel",)),
        name="rwkv_prep",
    )(ua, ub, ub, ub, ua, ub, ub, ub, prev, mu, w0, w2, a0, a2, g2, k_k, k_a)


def _wkv_pair(r, lw, k, v, kk, a, g, st, lnw, lnb, rk, *, chunk):
    C = chunk
    C2 = 2 * C

    lane = lax.broadcasted_iota(jnp.int32, (C2, LANES), 1)
    rowi = lax.broadcasted_iota(jnp.int32, (C2, LANES), 0)
    own = (lane >= HS_R) == (rowi >= C)

    def twice(x):
        return jnp.concatenate([x, x], axis=0)

    def stack(x):
        return jnp.where(own, twice(x), 0.0)

    tr = lax.broadcasted_iota(jnp.int32, (C, C), 0)
    tc = lax.broadcasted_iota(jnp.int32, (C, C), 1)
    tri = (tr >= tc).astype(BF16)
    lw_hi = lw.astype(BF16)
    rem = lw - lw_hi.astype(F32)
    lw_mid = rem.astype(BF16)
    lw_lo = (rem - lw_mid.astype(F32)).astype(BF16)
    G = _dot(tri, lw_hi) + _dot(tri, lw_mid) + _dot(tri, lw_lo)
    yield None
    GC = G[C - 1:C, :]
    e_prev = jnp.exp(G - lw)
    e_neg = jnp.exp(-G)
    e_pos = jnp.exp(G)
    e_rest = jnp.exp(GC - G)
    e_all = jnp.exp(GC)

    kk2 = stack(kk)
    nrm = jnp.sqrt(jnp.sum(kk2 * kk2, axis=1, keepdims=True))
    kk2 = kk2 / jnp.maximum(nrm, 1e-12)

    A2 = (kk2 * twice(-e_prev)).astype(BF16)
    B2 = (kk2 * twice(a * e_neg)).astype(BF16)
    K2 = stack(k * e_neg).astype(BF16)
    R2 = stack(r * e_pos).astype(BF16)
    Bt2 = (kk2 * twice(a * e_rest)).astype(BF16)
    Kt2 = stack(k * e_rest).astype(BF16)
    V2f = stack(v)
    V2 = V2f.astype(BF16)

    AR = jnp.concatenate([A2, R2], axis=0)
    mr = lax.broadcasted_iota(jnp.int32, (2 * C2, 2 * C2), 0)
    mc = lax.broadcasted_iota(jnp.int32, (2 * C2, 2 * C2), 1)
    tr_r = mr % C2
    tr_c = mc % C2
    keep = tr_r + jnp.where(mr < C2, 0, 1) > tr_c
    m_all = jnp.where(keep, _dot_nt(AR, jnp.concatenate([B2, K2], axis=0)), 0.0)
    m_ab = m_all[:C2, :C2]
    m_rb = m_all[C2:, :C2].astype(BF16)
    m_xk = m_all[:, C2:].astype(BF16)
    eye = (lax.broadcasted_iota(jnp.int32, (C2, C2), 0) == lax.broadcasted_iota(jnp.int32, (C2, C2), 1)).astype(F32)
    yield None

    t_inv = eye + m_ab
    pb = m_ab.astype(BF16)
    p = _dot(pb, pb)
    yield None
    n = 2
    while 2 * n < C:
        pb = p.astype(BF16)
        both = _dot(jnp.concatenate([pb, t_inv.astype(BF16)], axis=0), pb)
        p = both[:C2]
        t_inv = t_inv + both[C2:]
        n *= 2
        yield None
    t_inv = t_inv + _dot(t_inv.astype(BF16), p.astype(BF16))
    yield None

    stb = st.astype(BF16)
    xy = _dot(AR, stb) + _dot(m_xk, V2)
    yield None
    u2 = _dot(t_inv.astype(BF16), xy[:C2].astype(BF16))
    yield None
    u2b = u2.astype(BF16)
    y2 = xy[C2:] + _dot(m_rb, u2b)

    decay_col = jnp.broadcast_to(e_all, (LANES, LANES)).T
    st_new = decay_col * st + _dot_tn(jnp.concatenate([Bt2, Kt2], axis=0), jnp.concatenate([u2b, V2], axis=0))
    yield None

    inv_hs = 1.0 / HS_R
    mean = jnp.sum(y2, axis=1, keepdims=True) * inv_hs
    d = jnp.where(own, y2 - mean, 0.0)
    var = jnp.sum(d * d, axis=1, keepdims=True) * inv_hs
    yn2 = d * lax.rsqrt(var + GN_EPS)
    bonus2 = jnp.sum(stack(r * k * rk), axis=1, keepdims=True) * V2f
    yn = yn2[:C] + yn2[C:]
    bonus = bonus2[:C] + bonus2[C:]
    yield (yn * lnw + lnb + bonus) * g, st_new


def _wkv_body(r_ref, lw_ref, k_ref, v_ref, kk_ref, a_ref, g_ref, s0_ref, lnw_ref, lnb_ref, rk_ref,
              o_ref, st_ref, st_scr, *, chunk, pairs):
    c = pl.program_id(2)

    @pl.when(c == 0)
    def _():
        st_scr[...] = s0_ref[...]

    lanes = [slice(pp * LANES, (pp + 1) * LANES) for pp in range(pairs)]
    f32 = lambda ref, sl: ref[:, sl].astype(F32)
    gens = [_wkv_pair(f32(r_ref, sl), lw_ref[:, sl], f32(k_ref, sl), f32(v_ref, sl), f32(kk_ref, sl),
                      f32(a_ref, sl), f32(g_ref, sl), st_scr[pp], lnw_ref[:, sl], lnb_ref[:, sl],
                      rk_ref[:, sl], chunk=chunk) for pp, sl in enumerate(lanes)]
    results = [None] * pairs
    while any(res is None for res in results):
        results = [next(gen) for gen in gens]
    for pp, (o, st_new) in enumerate(results):
        st_scr[pp] = st_new
        o_ref[:, lanes[pp]] = o.astype(o_ref.dtype)

    @pl.when(c == pl.num_programs(2) - 1)
    def _():
        st_ref[...] = st_scr[...]


def _wkv(r, lw, k, v, kk, a, g, s0, lnw, lnb, rk, *, seqs, chunk, pairs):
    rows, width = r.shape
    t = rows // seqs
    nchunk = t // chunk
    npair = width // LANES
    wblk = pairs * LANES
    blk = pl.BlockSpec((chunk, wblk), lambda b, p, c: (b * nchunk + c, p))
    vec = pl.BlockSpec((1, wblk), lambda b, p, c: (0, p))
    stspec = pl.BlockSpec((None, pairs, LANES, LANES), lambda b, p, c: (b, p, 0, 0))
    return pl.pallas_call(
        functools.partial(_wkv_body, chunk=chunk, pairs=pairs),
        grid=(seqs, npair // pairs, nchunk),
        in_specs=[blk] * 7 + [stspec, vec, vec, vec],
        out_specs=[blk, stspec],
        out_shape=[jax.ShapeDtypeStruct((rows, width), BF16),
                   jax.ShapeDtypeStruct((seqs, npair, LANES, LANES), F32)],
        scratch_shapes=[pltpu.VMEM((pairs, LANES, LANES), F32)],
        compiler_params=_cparams(("parallel", "parallel", "arbitrary")),
        name="wkv7_chunked",
    )(r, lw, k, v, kk, a, g, s0, lnw, lnb, rk)


def _two_maps(q):
    lane = lax.broadcasted_iota(jnp.int32, q.shape, 1)
    q = q * (ATTN_SCALE * LOG2E)
    return jnp.where(lane < DK, q, 0.0), jnp.where(lane >= DK, q, 0.0)


def _attn_prompt_body(qi_ref, ki_ref, lam_ref, q_ref, k_ref, v_ref, sw_ref, o_ref,
                      q2_scr, m_scr, l_scr, acc_scr, *, tq, post_scale):
    step = pl.program_id(1)
    qi = qi_ref[step]
    ki = ki_ref[step]

    @pl.when(ki == 0)
    def _():
        q1, q2 = _two_maps(q_ref[...])
        q2_scr[...] = jnp.concatenate([q1, q2], axis=0).astype(BF16)
        m_scr[...] = jnp.full_like(m_scr, NEG_INF)
        l_scr[...] = jnp.zeros_like(l_scr)
        acc_scr[...] = jnp.zeros_like(acc_scr)

    def update(diagonal):
        for kb in range(tq // ATTN_KEY_SUB):
            k0 = kb * ATTN_KEY_SUB
            kblk = k_ref[k0:k0 + ATTN_KEY_SUB, :].astype(BF16)
            vtblk = v_ref[k0:k0 + ATTN_KEY_SUB, :].T.astype(BF16)
            for c in range(2 * tq // ATTN_Q_COLS):
                q0 = (c * ATTN_Q_COLS) % tq
                if diagonal and k0 > q0 + ATTN_Q_COLS - 1:
                    continue
                cs = slice(c * ATTN_Q_COLS, (c + 1) * ATTN_Q_COLS)
                st = _dot_nt(kblk, q2_scr[cs, :])
                if diagonal and k0 + ATTN_KEY_SUB - 1 > q0:
                    krow = k0 + lax.broadcasted_iota(jnp.int32, st.shape, 0)
                    qcol = q0 + lax.broadcasted_iota(jnp.int32, st.shape, 1)
                    st = jnp.where(krow <= qcol, st, NEG_INF)
                m_old = m_scr[:, cs]
                m_new = jnp.maximum(m_old, jnp.max(st, axis=0, keepdims=True))
                alpha = jnp.exp2(m_old - m_new)
                p = jnp.exp2(st - m_new)
                l_scr[:, cs] = alpha * l_scr[:, cs] + jnp.sum(p, axis=0, keepdims=True)
                acc_scr[:, cs] = alpha * acc_scr[:, cs] + _dot(vtblk, p.astype(BF16))
                m_scr[:, cs] = m_new

    @pl.when(ki < qi)
    def _():
        update(False)

    @pl.when(ki == qi)
    def _():
        update(True)
        acc = acc_scr[...]
        l = l_scr[...]
        ot = acc[:, :tq] / l[:, :tq] - lam_ref[:, 0:1] * (acc[:, tq:] / l[:, tq:])
        ms = jnp.mean(ot * ot, axis=0, keepdims=True)
        on = ot * lax.rsqrt(ms + SUBLN_EPS)
        o_ref[...] = (on.T * sw_ref[...] * post_scale).astype(o_ref.dtype)


def _attn_prompt(u, lam, subw, *, t, heads, tq, post_scale):
    nq = t // tq
    pairs = [(i, j) for i in range(nq) for j in range(i + 1)]
    qi_tab = jnp.array([p[0] for p in pairs], jnp.int32)
    ki_tab = jnp.array([p[1] for p in pairs], jnp.int32)
    vec = pl.BlockSpec((1, LANES), lambda h, s, qi, ki: (0, 0))
    grid_spec = pltpu.PrefetchScalarGridSpec(
        num_scalar_prefetch=2,
        grid=(heads, len(pairs)),
        in_specs=[vec,
                  pl.BlockSpec((tq, LANES), lambda h, s, qi, ki: (qi[s], h)),
                  pl.BlockSpec((tq, LANES), lambda h, s, qi, ki: (ki[s], heads + h)),
                  pl.BlockSpec((tq, LANES), lambda h, s, qi, ki: (ki[s], 2 * heads + h)),
                  vec],
        out_specs=pl.BlockSpec((tq, LANES), lambda h, s, qi, ki: (qi[s], h)),
        scratch_shapes=[pltpu.VMEM((2 * tq, LANES), BF16),
                        pltpu.VMEM((1, 2 * tq), F32),
                        pltpu.VMEM((1, 2 * tq), F32),
                        pltpu.VMEM((LANES, 2 * tq), F32)],
    )
    return pl.pallas_call(
        functools.partial(_attn_prompt_body, tq=tq, post_scale=post_scale),
        grid_spec=grid_spec,
        out_shape=jax.ShapeDtypeStruct((t, heads * LANES), BF16),
        compiler_params=_cparams(("parallel", "arbitrary")),
        name="diff_attn_prompt",
    )(qi_tab, ki_tab, lam, u, u, u, subw)


def _attn_sample_body(pt_ref, lam_ref, q_ref, *refs, heads, tn, post_scale, pages):
    groups = heads // HEAD_TILE
    kc_refs = refs[:pages]
    vc_refs = refs[pages:2 * pages]
    kn_ref, vn_ref, bias_ref, biasn_ref, sw_ref, o_ref, q2_scr, m_scr, l_scr, acc_scr = refs[2 * pages:]
    j = pl.program_id(1)
    n_steps = pl.num_programs(1) - 1
    half = HEAD_TILE * tn

    @pl.when(j == 0)
    def _():
        for h in range(heads):
            g, hh = divmod(h, HEAD_TILE)
            q1, q2 = _two_maps(q_ref[:, h * LANES:(h + 1) * LANES])
            q2_scr[g, hh * tn:(hh + 1) * tn] = q1
            q2_scr[g, half + hh * tn:half + (hh + 1) * tn] = q2
        m_scr[...] = jnp.full_like(m_scr, NEG_INF)
        l_scr[...] = jnp.zeros_like(l_scr)
        acc_scr[...] = jnp.zeros_like(acc_scr)

    def attend(k_pages, v_pages, b_ref):
        bias = b_ref[...]
        def flat(ref, g):
            tile = ref[:, g * HEAD_TILE:(g + 1) * HEAD_TILE, :]
            return tile.reshape(PAGE * HEAD_TILE, LANES).astype(BF16)

        for g in range(groups):
            q2 = q2_scr[g].astype(BF16)
            ss = [_dot_nt(q2, flat(kp, g)) + bias for kp in k_pages]
            m_old = m_scr[g]
            m_new = m_old
            for s in ss:
                m_new = jnp.maximum(m_new, jnp.max(s, axis=1, keepdims=True))
            alpha = jnp.exp2(m_old - m_new)
            l_new = alpha * l_scr[g]
            acc_new = alpha * acc_scr[g]
            for s, vp in zip(ss, v_pages):
                p = jnp.exp2(s - m_new)
                l_new = l_new + jnp.sum(p, axis=1, keepdims=True)
                acc_new = acc_new + _dot(p.astype(BF16), flat(vp, g))
            l_scr[g] = l_new
            acc_scr[g] = acc_new
            m_scr[g] = m_new

    @pl.when(j < n_steps)
    def _():
        attend(kc_refs, vc_refs, bias_ref)

    @pl.when(j == n_steps)
    def _():
        attend([kn_ref], [vn_ref], biasn_ref)
        for g in range(groups):
            acc = acc_scr[g]
            l = l_scr[g]
            o = acc[:half] / l[:half] - lam_ref[...] * (acc[half:] / l[half:])
            ms = jnp.mean(o * o, axis=-1, keepdims=True)
            o = o * lax.rsqrt(ms + SUBLN_EPS) * sw_ref[...] * post_scale
            for hh in range(HEAD_TILE):
                h = g * HEAD_TILE + hh
                o_ref[:, h * LANES:(h + 1) * LANES] = o[hh * tn:(hh + 1) * tn]


def _sample_bias(tn, causal):
    r = jnp.arange(2 * HEAD_TILE * tn)[:, None]
    c = jnp.arange(PAGE * HEAD_TILE)[None, :]
    ok = (c % HEAD_TILE) == ((r % (HEAD_TILE * tn)) // tn)
    if causal:
        ok = ok & ((c // HEAD_TILE) <= (r % tn))
    return jnp.where(ok, 0.0, NEG_INF).astype(F32)


def _attn_sample(q, cache_k, cache_v, k_new, v_new, page_table, lam, subw, *, layer, post_scale):
    b, n_pages = page_table.shape
    heads = cache_k.shape[3]
    groups = heads // HEAD_TILE
    tn = q.shape[0] // b
    rows = 2 * HEAD_TILE * tn
    pt = page_table.reshape(-1)
    pages = _pick_tile(n_pages, (SAMPLE_PAGES, 1))
    n_steps = n_pages // pages

    def page_spec(k):
        def page_map(i, j, pt_ref):
            return (layer, pt_ref[i * n_pages + jnp.minimum(j, n_steps - 1) * pages + k], 0, 0, 0)
        return pl.BlockSpec((None, None, PAGE, heads, LANES), page_map)

    page_specs = [page_spec(k) for k in range(pages)]
    new_spec = pl.BlockSpec((None, PAGE, heads, LANES), lambda i, j, pt_ref: (i, 0, 0, 0))
    vec = pl.BlockSpec((1, LANES), lambda i, j, pt_ref: (0, 0))
    bias_spec = pl.BlockSpec((rows, PAGE * HEAD_TILE), lambda i, j, pt_ref: (0, 0))
    grid_spec = pltpu.PrefetchScalarGridSpec(
        num_scalar_prefetch=1,
        grid=(b, n_steps + 1),
        in_specs=[vec,
                  pl.BlockSpec((tn, heads * LANES), lambda i, j, pt_ref: (i, 0)),
                  *page_specs, *page_specs, new_spec, new_spec, bias_spec, bias_spec, vec],
        out_specs=pl.BlockSpec((tn, heads * LANES), lambda i, j, pt_ref: (i, 0)),
        scratch_shapes=[pltpu.VMEM((groups, rows, LANES), F32),
                        pltpu.VMEM((groups, rows, 1), F32),
                        pltpu.VMEM((groups, rows, 1), F32),
                        pltpu.VMEM((groups, rows, LANES), F32)],
    )
    return pl.pallas_call(
        functools.partial(_attn_sample_body, heads=heads, tn=tn, post_scale=post_scale, pages=pages),
        grid_spec=grid_spec,
        out_shape=jax.ShapeDtypeStruct((b * tn, heads * LANES), F32),
        compiler_params=_cparams(("parallel", "arbitrary")),
        name="diff_attn_sample",
    )(pt, lam, q, *([cache_k] * pages), *([cache_v] * pages), k_new, v_new,
      _sample_bias(tn, False), _sample_bias(tn, True), subw)


def _pick_tile(n, candidates):
    for c in candidates:
        if n % c == 0:
            return c
    raise ValueError(f"no tile for {n}")


def _pair_states(s):
    b, h, _, _ = s.shape
    st = jnp.swapaxes(s, -1, -2).reshape(b, h // 2, 2, HS_R, HS_R)
    z = jnp.zeros_like(st[:, :, 0])
    top = jnp.concatenate([st[:, :, 0], z], axis=-1)
    bot = jnp.concatenate([z, st[:, :, 1]], axis=-1)
    return jnp.concatenate([top, bot], axis=-2)


def _head_states(sp):
    b, hp, _, _ = sp.shape
    s0 = sp[:, :, :HS_R, :HS_R]
    s1 = sp[:, :, HS_R:, HS_R:]
    st = jnp.stack([s0, s1], axis=2).reshape(b, 2 * hp, HS_R, HS_R)
    return jnp.swapaxes(st, -1, -2)


def _layer(l, depth, xp, xs, cache_k, cache_v, state_wkv, state_shift, page_table,
           w_in, mu_shift, w_decay0, w_decay2, a0, a2, g2, k_k, k_a, r_k, lnx_w, lnx_b,
           lambda_q1, lambda_k1, lambda_q2, lambda_k2, subln_w,
           w_pa, w_pb, w_gate, b_gate, w_o, ln1_g, ln1_b,
           w_ffn_gate, w_ffn_up, w_ffn_down, ln2_g, ln2_b):
    bp, sp, d = xp.shape
    bs, ts, _ = xs.shape
    assert bp == 1
    heads = cache_k.shape[3]
    att_w = heads * LANES
    rw_w = w_decay0.shape[1]
    h_r = rw_w // HS_R
    rw_in = 3 * rw_w + W_LORA + A_LORA + G_LORA
    assert w_in.shape[2] == 3 * att_w + rw_in and att_w == rw_w
    alpha = (2 * depth) ** 0.25
    lam_init = 0.8 - 0.6 * math.exp(-0.3 * l)
    post_scale = 1.0 - lam_init

    lam = (jnp.exp(jnp.sum(lambda_q1[l] * lambda_k1[l])) - jnp.exp(jnp.sum(lambda_q2[l] * lambda_k2[l]))
           + lam_init)
    lam = jnp.broadcast_to(lam.astype(F32), (1, LANES))

    n_p = bp * sp
    n_s = bs * ts
    m = n_p + n_s
    xp2 = xp.reshape(n_p, d)
    xs2 = xs.reshape(n_s, d)
    xb = jnp.concatenate([xp2.astype(BF16), xs2.astype(BF16)], axis=0)

    o_rw = 3 * att_w
    c_wl = o_rw + rw_w
    c_k = c_wl + W_LORA
    c_al = c_k + 2 * rw_w
    c_gl = c_al + A_LORA
    lora_w = 2 * LORA_PAD + G_LORA
    rw_cols = 3 * rw_w + lora_w

    def rest_cols(t, off, axis=1):
        take = lambda a, b: lax.slice_in_dim(t, a - off, b - off, axis=axis)
        zshape = list(t.shape)
        zshape[axis] = LORA_PAD - W_LORA
        z = jnp.zeros(zshape, t.dtype)
        return jnp.concatenate([take(c_k, c_al), take(c_wl, c_k), z, take(c_al, c_gl), z,
                                take(c_gl, c_gl + G_LORA)], axis=axis)

    def prep_cols(t):
        return jnp.concatenate([t[:, :rw_w], rest_cols(t, o_rw)], axis=1)

    wt = w_in[l].T
    wt_rest = rest_cols(wt, 0, axis=0)
    tm = _pick_tile(m, (768, 512, 384, 256, 128, 64, 32, 16, 8))
    tm_big = _pick_tile(m, (1408, 768, 512, 384, 256, 128, 64, 32, 16, 8))
    ua = _matmul_nt(xb, wt, n=c_wl, tm=tm_big, tn=_pick_tile(c_wl, (256, 128)), out_dtype=F32,
                    name="in_proj_qkvr")
    ub = _matmul_nt(xb, wt_rest, n=wt_rest.shape[0], tm=tm, tn=_pick_tile(wt_rest.shape[0], (512, 256, 128)),
                    out_dtype=F32, name="in_proj_rest")

    mu = prep_cols(mu_shift[l][None, :])
    pad_rows = lambda t: jnp.concatenate([t, jnp.zeros((LORA_PAD - t.shape[0], t.shape[1]), t.dtype)], axis=0)
    w2 = pad_rows(w_decay2[l]).astype(BF16)
    a2b = pad_rows(a2[l]).astype(BF16)
    g2b = g2[l].astype(BF16)
    row = lambda t: t[l][None, :]
    prep_args = (mu, row(w_decay0), w2, row(a0), a2b, g2b, row(k_k), row(k_a))

    tp = _pick_tile(sp, (128, 64, 32, 16, 8))
    prev_p = jnp.zeros((bp, 1, rw_cols), F32)
    prep_p = _rwkv_prep(ua, ub, 0, sp, prev_p, *prep_args, tm=tp, width=rw_w, r_col=o_rw, chained=True,
                        store=BF16)

    prev_s = prep_cols(state_shift[l]).reshape(bs, 1, rw_cols)
    prep_s = _rwkv_prep(ua, ub, n_p, n_s, prev_s, *prep_args, tm=ts, width=rw_w, r_col=o_rw, chained=False,
                        store=F32)

    wkv_params = (row(lnx_w), row(lnx_b), r_k[l].reshape(1, rw_w))
    s0_p = jnp.zeros((bp, h_r // 2, LANES, LANES), F32)
    pairs = _pick_tile(h_r // 2, (WKV_PAIRS, 2, 1))
    o_rw_p, st_p = _wkv(*prep_p, s0_p, *wkv_params, seqs=bp, chunk=WKV_CHUNK, pairs=pairs)

    t_pad = -(-ts // WKV_CHUNK) * WKV_CHUNK
    pad_t = lambda t: jnp.pad(t.reshape(bs, ts, rw_w), ((0, 0), (0, t_pad - ts), (0, 0))).reshape(bs * t_pad, rw_w)
    s0_s = _pair_states(state_wkv[l].astype(F32))
    o_rw_s, st_s = _wkv(*[pad_t(t) for t in prep_s], s0_s, *wkv_params, seqs=bs, chunk=WKV_CHUNK, pairs=pairs)
    o_rw_s = o_rw_s.reshape(bs, t_pad, rw_w)[:, :ts].reshape(n_s, rw_w)
    o_rwkv = jnp.concatenate([o_rw_p, o_rw_s], axis=0)

    new_wkv_p = _head_states(st_p).astype(state_wkv.dtype)
    new_wkv_s = _head_states(st_s).astype(state_wkv.dtype)

    def shift_row(rows):
        ta = ua[rows]
        tb = ub[rows]
        w_ = rw_w
        return jnp.concatenate([ta[:, o_rw:], tb[:, 2 * w_:2 * w_ + W_LORA], tb[:, :2 * w_],
                                tb[:, 2 * w_ + LORA_PAD:2 * w_ + LORA_PAD + A_LORA],
                                tb[:, 2 * w_ + 2 * LORA_PAD:]], axis=1)

    new_shift_p = shift_row(jnp.array([sp - 1]))
    new_shift_s = shift_row(n_p + ts - 1 + ts * jnp.arange(bs))

    subw = subln_w[l][None, :]
    tq = _pick_tile(sp, (2048, 1024, 512, 256, 128))
    o_att_p = _attn_prompt(ua, lam, subw, t=sp, heads=heads, tq=tq, post_scale=post_scale)

    q_s = lax.slice(ua, (n_p, 0), (m, att_w))
    k_s = lax.slice(ua, (n_p, att_w), (m, 2 * att_w)).reshape(bs, ts, heads, DV)
    v_s = lax.slice(ua, (n_p, 2 * att_w), (m, 3 * att_w)).reshape(bs, ts, heads, DV)
    pad_new = lambda t: jnp.pad(t, ((0, 0), (0, PAGE - ts), (0, 0), (0, 0)))
    o_att_s = _attn_sample(q_s, cache_k, cache_v, pad_new(k_s), pad_new(v_s), page_table, lam, subw,
                           layer=l, post_scale=post_scale)
    o_att = jnp.concatenate([o_att_p, o_att_s.astype(BF16)], axis=0)

    wg = w_gate[l].astype(BF16)
    bg = b_gate[l][None, :]
    mixin = _gated_merge(xb, o_att, o_rwkv, wg, bg, w_pa[l].astype(BF16), w_pb[l].astype(BF16),
                         tm=tm, tn=_pick_tile(d, (256, 128)))
    mix = _matmul(mixin, w_o[l], tm=tm, tn=_pick_tile(d, (512, 256, 128)), out_dtype=F32, name="out_proj")
    tl = _pick_tile(math.gcd(n_p, n_s), (256, 128, 64, 32, 16, 8))
    x1, x1b = _add_norm(mix, [(xp2, 0), (xs2, n_p)], row(ln1_g), row(ln1_b), alpha=alpha, tm=tl,
                        outs=[(m, 0, F32), (m, 0, BF16)])

    d_ff = w_ffn_gate.shape[2]
    hmid = _swiglu(x1b, w_ffn_gate[l], w_ffn_up[l], tm=tm_big, tn=_pick_tile(d_ff, (256, 128)))
    down = _matmul(hmid, w_ffn_down[l].astype(BF16), tm=tm, tn=_pick_tile(d, (256, 128)), out_dtype=F32,
                   name="ffn_down")
    y_p, y_s = _add_norm(down, [(x1, 0)], row(ln2_g), row(ln2_b), alpha=alpha, tm=tl,
                         outs=[(n_p, 0, F32), (n_s, n_p, F32)])

    k_all = lax.slice(ua, (0, att_w), (m, 2 * att_w))
    v_all = lax.slice(ua, (0, 2 * att_w), (m, 3 * att_w))
    outs = dict(
        xp=y_p.reshape(bp, sp, d), xs=y_s.reshape(bs, ts, d),
        kp=k_all[:n_p].reshape(bp, sp, heads, DV), vp=v_all[:n_p].reshape(bp, sp, heads, DV),
        wp=new_wkv_p, sp=new_shift_p,
        ks=k_s, vs=v_s, ws=new_wkv_s, ss=new_shift_s)
    return outs


def kernel(x_prompt, x_sample, cache_k, cache_v, state_wkv, state_shift, page_table, w_in, mu_shift, w_decay0, w_decay2, a0, a2, g2, k_k, k_a, r_k, lnx_w, lnx_b, lambda_q1, lambda_k1, lambda_q2, lambda_k2, subln_w, w_pa, w_pb, w_gate, b_gate, w_o, ln1_g, ln1_b, w_ffn_gate, w_ffn_up, w_ffn_down, ln2_g, ln2_b):
    depth = w_in.shape[0]
    xp, xs = x_prompt, x_sample
    acc = {n: [] for n in ("kp", "vp", "wp", "sp", "ks", "vs", "ws", "ss")}
    for l in range(depth):
        o = _layer(l, depth, xp, xs, cache_k, cache_v, state_wkv, state_shift, page_table,
                   w_in, mu_shift, w_decay0, w_decay2, a0, a2, g2, k_k, k_a, r_k, lnx_w, lnx_b,
                   lambda_q1, lambda_k1, lambda_q2, lambda_k2, subln_w,
                   w_pa, w_pb, w_gate, b_gate, w_o, ln1_g, ln1_b,
                   w_ffn_gate, w_ffn_up, w_ffn_down, ln2_g, ln2_b)
        xp, xs = o["xp"], o["xs"]
        for n in acc:
            acc[n].append(o[n])
    st = lambda n: jnp.stack(acc[n])
    return (xp, xs, st("kp"), st("vp"), st("wp"), st("sp"), st("ks"), st("vs"), st("ws"), st("ss"))
```
